```python
import math
import jax, jax.numpy as jnp
from jax import lax
import numpy as np

D_MODEL = 2048
BATCH = 8
SEQ = 8192
DEPTH = 1

HEAD_DIM = 128
N_Q_HEADS = 8
N_KV_HEADS = 2
GQA_GROUP = N_Q_HEADS // N_KV_HEADS
ATTN_WIDTH = N_Q_HEADS * HEAD_DIM
KV_WIDTH = N_KV_HEADS * HEAD_DIM
WINDOW = 128
BLOCK = 128
SPAN = BLOCK + 2 * WINDOW
N_BUCKETS = 32
MAX_DISTANCE = 128
POOL_SIZES = (2, 4, 8, 16)
N_POOL_GROUPS = len(POOL_SIZES)
POOL_WIDTH = D_MODEL // 2
POOL_GROUP_WIDTH = POOL_WIDTH // N_POOL_GROUPS
SPLIT_WIDTHS = (ATTN_WIDTH, KV_WIDTH, KV_WIDTH, ATTN_WIDTH, POOL_WIDTH, POOL_WIDTH)
SPLIT_POINTS = tuple(int(s) for s in np.cumsum(SPLIT_WIDTHS)[:-1])
IN_WIDTH = int(sum(SPLIT_WIDTHS))
N_BRANCHES = 2
EPS = 1e-6
NEG_INF = -1e30

kernel_name = "hybrid_swa_pool_gated_encoder"


def rmsnorm(x, g):
    xf = x.astype(jnp.float32)
    xf = xf * lax.rsqrt(jnp.mean(xf * xf, axis=-1, keepdims=True) + EPS)
    return xf.astype(x.dtype) * g


def t5_bucket(rel):
    half = N_BUCKETS // 2
    max_exact = half // 2
    ret = jnp.where(rel > 0, half, 0)
    n = jnp.abs(rel)
    nf = jnp.maximum(n, 1).astype(jnp.float32)
    large = max_exact + (jnp.log(nf / max_exact) / math.log(MAX_DISTANCE / max_exact)
                         * (half - max_exact)).astype(jnp.int32)
    large = jnp.minimum(large, half - 1)
    return ret + jnp.where(n < max_exact, n, large)


def windowed_gqa(q, k, v, rel_table, sink):
    B, S, _ = q.shape
    nblk = S // BLOCK
    qb = q.reshape(B, nblk, BLOCK, N_KV_HEADS, GQA_GROUP, HEAD_DIM)
    pad = ((0, 0), (WINDOW, WINDOW), (0, 0), (0, 0))
    kp = jnp.pad(k.reshape(B, S, N_KV_HEADS, HEAD_DIM), pad)
    vp = jnp.pad(v.reshape(B, S, N_KV_HEADS, HEAD_DIM), pad)
    idx = jnp.arange(nblk)[:, None] * BLOCK + jnp.arange(SPAN)[None, :]
    kw = kp[:, idx]
    vw = vp[:, idx]
    logits = jnp.einsum('bnqkgd,bntkd->bnkgqt', qb, kw).astype(jnp.float32) * (HEAD_DIM ** -0.5)
    rel = jnp.arange(SPAN)[None, :] - WINDOW - jnp.arange(BLOCK)[:, None]
    bias = rel_table[t5_bucket(rel)]
    bias = jnp.transpose(bias, (2, 0, 1)).reshape(N_KV_HEADS, GQA_GROUP, BLOCK, SPAN)
    key_pos = idx - WINDOW
    key_ok = (key_pos >= 0) & (key_pos < S)
    valid = (jnp.abs(rel) <= WINDOW)[None] & key_ok[:, None, :]
    logits = jnp.where(valid[None, :, None, None], logits + bias.astype(jnp.float32), NEG_INF)
    sink_l = jnp.broadcast_to(sink.astype(jnp.float32).reshape(N_KV_HEADS, GQA_GROUP, 1, 1),
                              logits.shape[:-1] + (1,))
    probs = jax.nn.softmax(jnp.concatenate([logits, sink_l], axis=-1), axis=-1)[..., :-1]
    out = jnp.einsum('bnkgqt,bntkd->bnqkgd', probs.astype(vw.dtype), vw)
    return out.reshape(B, S, ATTN_WIDTH)


def multiscale_pool(u, w_grp, scale):
    B, S, _ = u.shape
    ug = u.reshape(B, S, N_POOL_GROUPS, POOL_GROUP_WIDTH).astype(jnp.float32)
    cs = jnp.pad(jnp.cumsum(ug, axis=1), ((0, 0), (1, 0), (0, 0), (0, 0)))
    pos = jnp.arange(S)
    outs = []
    for gi, w in enumerate(POOL_SIZES):
        lo = jnp.clip(pos - w // 2, 0, S)
        hi = jnp.clip(pos + w // 2, 0, S)
        cnt = (hi - lo).astype(jnp.float32)[None, :, None]
        outs.append((cs[:, hi, gi] - cs[:, lo, gi]) / cnt - ug[:, :, gi])
    pooled = jnp.stack(outs, axis=2).astype(u.dtype)
    mixed = jnp.einsum('bsgc,gcd->bsgd', pooled, w_grp).reshape(B, S, POOL_WIDTH)
    return mixed * scale


def hybrid_layer(x, c, rel_table, w_ada, b_ada, pre_g, post_g, w_in, sink,
                 w_pool_grp, pool_scale, w_br_attn, w_br_pool, w_merge, b_merge, w_out):
    mod = jax.nn.silu(c) @ w_ada + b_ada
    shift, scale, gate = jnp.split(mod, 3, axis=-1)
    h = rmsnorm(x, pre_g) * (1.0 + scale[:, None]) + shift[:, None]
    proj = h @ w_in
    q, k, v, gate_a, u, gate_p = jnp.split(proj, SPLIT_POINTS, axis=-1)
    y_a = windowed_gqa(q, k, v, rel_table, sink) * jax.nn.silu(gate_a)
    y_p = multiscale_pool(u, w_pool_grp, pool_scale) * jax.nn.silu(gate_p)
    g = jax.nn.sigmoid(h @ w_merge + b_merge)
    g_a, g_p = jnp.split(g, N_BRANCHES, axis=-1)
    merged = g_a * (y_a @ w_br_attn) + g_p * (y_p @ w_br_pool)
    o = merged @ w_out
    return x + gate[:, None] * rmsnorm(o, post_g)


def _fwd_setup_inputs(seed: int = 0) -> dict:
    key = jax.random.key(seed)
    ks = jax.random.split(key, 18)
    D = D_MODEL
    nrm = lambda k, shape, s: jax.random.normal(k, shape, jnp.float32) * s
    return {
        "x": nrm(ks[0], (BATCH, SEQ, D), 1.0),
        "c": nrm(ks[1], (BATCH, D), 1.0),
        "rel_bias_table": nrm(ks[2], (N_BUCKETS, N_Q_HEADS), 0.5),
        "w_ada": nrm(ks[3], (DEPTH, D, 3 * D), 0.1 * D ** -0.5),
        "b_ada": nrm(ks[4], (DEPTH, 3 * D), 0.02),
        "pre_norm_g": 1.0 + nrm(ks[5], (DEPTH, D), 0.05),
        "post_norm_g": 1.0 + nrm(ks[6], (DEPTH, D), 0.05),
        "w_in": nrm(ks[7], (DEPTH, D, IN_WIDTH), D ** -0.5),
        "attn_sink": nrm(ks[8], (DEPTH, N_Q_HEADS), 0.5),
        "w_pool_group": nrm(ks[9], (DEPTH, N_POOL_GROUPS, POOL_GROUP_WIDTH, POOL_GROUP_WIDTH),
                            POOL_GROUP_WIDTH ** -0.5),
        "pool_scale": 1.0 + nrm(ks[10], (DEPTH, POOL_WIDTH), 0.1),
        "w_branch_attn": nrm(ks[11], (DEPTH, ATTN_WIDTH, D), ATTN_WIDTH ** -0.5),
        "w_branch_pool": nrm(ks[12], (DEPTH, POOL_WIDTH, D), POOL_WIDTH ** -0.5),
        "w_merge": nrm(ks[13], (DEPTH, D, N_BRANCHES * D), D ** -0.5),
        "b_merge": nrm(ks[14], (DEPTH, N_BRANCHES * D), 0.02),
        "w_out": nrm(ks[15], (DEPTH, D, D), D ** -0.5),
    }


def _fwd_reference(x, c, rel_bias_table, w_ada, b_ada, pre_norm_g, post_norm_g, w_in, attn_sink,
              w_pool_group, pool_scale, w_branch_attn, w_branch_pool, w_merge, b_merge, w_out):
    for l in range(DEPTH):
        x = hybrid_layer(x, c, rel_bias_table, w_ada[l], b_ada[l], pre_norm_g[l], post_norm_g[l],
                         w_in[l], attn_sink[l], w_pool_group[l], pool_scale[l],
                         w_branch_attn[l], w_branch_pool[l], w_merge[l], b_merge[l], w_out[l])
    return x


import jax as _jax
import jax.numpy as _jnp

TWIN_FORMAT = 'train_step'
FWD_PARAMS = ['x', 'c', 'rel_bias_table', 'w_ada', 'b_ada', 'pre_norm_g', 'post_norm_g', 'w_in', 'attn_sink', 'w_pool_group', 'pool_scale', 'w_branch_attn', 'w_branch_pool', 'w_merge', 'b_merge', 'w_out']
TWIN_WEIGHTS = ['rel_bias_table', 'w_ada', 'b_ada', 'pre_norm_g', 'post_norm_g', 'w_in', 'attn_sink', 'w_pool_group', 'pool_scale', 'w_branch_attn', 'w_branch_pool', 'w_merge', 'b_merge', 'w_out']
TWIN_DIFF_INPUT = 'x'
TWIN_INPUTS = ['x', 'c', 'rel_bias_table', 'w_ada', 'b_ada', 'pre_norm_g', 'post_norm_g', 'w_in', 'attn_sink', 'w_pool_group', 'pool_scale', 'w_branch_attn', 'w_branch_pool', 'w_merge', 'b_merge', 'w_out', 'loss_target', 'm_rel_bias_table', 'm_w_ada', 'm_b_ada', 'm_pre_norm_g', 'm_post_norm_g', 'm_w_in', 'm_attn_sink', 'm_w_pool_group', 'm_pool_scale', 'm_w_branch_attn', 'm_w_branch_pool', 'm_w_merge', 'm_b_merge', 'm_w_out', 'v_rel_bias_table', 'v_w_ada', 'v_b_ada', 'v_pre_norm_g', 'v_post_norm_g', 'v_w_in', 'v_attn_sink', 'v_w_pool_group', 'v_pool_scale', 'v_w_branch_attn', 'v_w_branch_pool', 'v_w_merge', 'v_b_merge', 'v_w_out']
TWIN_OUTPUTS = ['loss', 'grad_x', 'grad_rel_bias_table', 'grad_w_ada', 'grad_b_ada', 'grad_pre_norm_g', 'grad_post_norm_g', 'grad_w_in', 'grad_attn_sink', 'grad_w_pool_group', 'grad_pool_scale', 'grad_w_branch_attn', 'grad_w_branch_pool', 'grad_w_merge', 'grad_b_merge', 'grad_w_out', 'delta_rel_bias_table', 'delta_w_ada', 'delta_b_ada', 'delta_pre_norm_g', 'delta_post_norm_g', 'delta_w_in', 'delta_attn_sink', 'delta_w_pool_group', 'delta_pool_scale', 'delta_w_branch_attn', 'delta_w_branch_pool', 'delta_w_merge', 'delta_b_merge', 'delta_w_out', 'new_m_rel_bias_table', 'new_m_w_ada', 'new_m_b_ada', 'new_m_pre_norm_g', 'new_m_post_norm_g', 'new_m_w_in', 'new_m_attn_sink', 'new_m_w_pool_group', 'new_m_pool_scale', 'new_m_w_branch_attn', 'new_m_w_branch_pool', 'new_m_w_merge', 'new_m_b_merge', 'new_m_w_out', 'new_v_rel_bias_table', 'new_v_w_ada', 'new_v_b_ada', 'new_v_pre_norm_g', 'new_v_post_norm_g', 'new_v_w_in', 'new_v_attn_sink', 'new_v_w_pool_group', 'new_v_pool_scale', 'new_v_w_branch_attn', 'new_v_w_branch_pool', 'new_v_w_merge', 'new_v_b_merge', 'new_v_w_out']
TWIN_LEAF_KINDS = {'loss': 'loss', 'grad_x': 'grad_x', 'grad_rel_bias_table': 'grad_w', 'grad_w_ada': 'grad_w', 'grad_b_ada': 'grad_w', 'grad_pre_norm_g': 'grad_w', 'grad_post_norm_g': 'grad_w', 'grad_w_in': 'grad_w', 'grad_attn_sink': 'grad_w', 'grad_w_pool_group': 'grad_w', 'grad_pool_scale': 'grad_w', 'grad_w_branch_attn': 'grad_w', 'grad_w_branch_pool': 'grad_w', 'grad_w_merge': 'grad_w', 'grad_b_merge': 'grad_w', 'grad_w_out': 'grad_w', 'delta_rel_bias_table': 'delta_w', 'delta_w_ada': 'delta_w', 'delta_b_ada': 'delta_w', 'delta_pre_norm_g': 'delta_w', 'delta_post_norm_g': 'delta_w', 'delta_w_in': 'delta_w', 'delta_attn_sink': 'delta_w', 'delta_w_pool_group': 'delta_w', 'delta_pool_scale': 'delta_w', 'delta_w_branch_attn': 'delta_w', 'delta_w_branch_pool': 'delta_w', 'delta_w_merge': 'delta_w', 'delta_b_merge': 'delta_w', 'delta_w_out': 'delta_w', 'new_m_rel_bias_table': 'new_m', 'new_m_w_ada': 'new_m', 'new_m_b_ada': 'new_m', 'new_m_pre_norm_g': 'new_m', 'new_m_post_norm_g': 'new_m', 'new_m_w_in': 'new_m', 'new_m_attn_sink': 'new_m', 'new_m_w_pool_group': 'new_m', 'new_m_pool_scale': 'new_m', 'new_m_w_branch_attn': 'new_m', 'new_m_w_branch_pool': 'new_m', 'new_m_w_merge': 'new_m', 'new_m_b_merge': 'new_m', 'new_m_w_out': 'new_m', 'new_v_rel_bias_table': 'new_v', 'new_v_w_ada': 'new_v', 'new_v_b_ada': 'new_v', 'new_v_pre_norm_g': 'new_v', 'new_v_post_norm_g': 'new_v', 'new_v_w_in': 'new_v', 'new_v_attn_sink': 'new_v', 'new_v_w_pool_group': 'new_v', 'new_v_pool_scale': 'new_v', 'new_v_w_branch_attn': 'new_v', 'new_v_w_branch_pool': 'new_v', 'new_v_w_merge': 'new_v', 'new_v_b_merge': 'new_v', 'new_v_w_out': 'new_v'}


def _forward(args):
    return _fwd_reference(*[args[k] for k in FWD_PARAMS])


def _output_shape():
    def fwd():
        inp = _fwd_setup_inputs(0)
        return _fwd_reference(*[inp[k] for k in FWD_PARAMS])
    out = _jax.eval_shape(fwd)
    return out.shape, out.dtype

N_MICROBATCH = 1
ADAM_LR = 0.001
ADAM_B1 = 0.9
ADAM_B2 = 0.999
ADAM_EPS = 1e-08
ADAM_WD = 0.01
ADAM_STEP = 10
PER_EXAMPLE_BATCH_AXIS = {'x': 0, 'c': 0, 'loss_target': 0}
SHARED_INPUTS = []
_WEIGHT_DTYPES = {'rel_bias_table': _jnp.float32, 'w_ada': _jnp.float32, 'b_ada': _jnp.float32, 'pre_norm_g': _jnp.float32, 'post_norm_g': _jnp.float32, 'w_in': _jnp.float32, 'attn_sink': _jnp.float32, 'w_pool_group': _jnp.float32, 'pool_scale': _jnp.float32, 'w_branch_attn': _jnp.float32, 'w_branch_pool': _jnp.float32, 'w_merge': _jnp.float32, 'b_merge': _jnp.float32, 'w_out': _jnp.float32}
MOMENT_SCALE = {'rel_bias_table': 4.602580e-03, 'w_ada': 2.076695e-01, 'b_ada': 5.561763e-01, 'pre_norm_g': 1.802790e-02, 'post_norm_g': 1.435015e-01, 'w_in': 1.121125e-02, 'attn_sink': 5.281330e-05, 'w_pool_group': 1.686482e-02, 'pool_scale': 1.559081e-02, 'w_branch_attn': 1.922537e-03, 'w_branch_pool': 1.190181e-02, 'w_merge': 3.138160e-03, 'b_merge': 3.171110e-03, 'w_out': 1.198383e-02}


def _to_microbatches(a, axis):
    t = _jnp.moveaxis(a, axis, 0)
    t = t.reshape((N_MICROBATCH, t.shape[0] // N_MICROBATCH) + t.shape[1:])
    return _jnp.moveaxis(t, 1, axis + 1)


def setup_inputs(seed: int = 0) -> dict:
    inp = _fwd_setup_inputs(seed)
    key = _jax.random.fold_in(_jax.random.key(seed), 7919)
    shape, _ = _output_shape()
    out = dict(inp)
    out["loss_target"] = _jax.random.normal(_jax.random.fold_in(key, 0), shape, _jnp.float32)
    for i, name in enumerate(TWIN_WEIGHTS):
        w = inp[name].astype(_jnp.float32)
        if MOMENT_SCALE is None:
            s = _jnp.sqrt(_jnp.mean(_jnp.square(w)) + 1e-30)
        else:
            s = MOMENT_SCALE[name]
        km, kv = _jax.random.split(_jax.random.fold_in(key, i + 1))
        out[name] = w
        out["m_" + name] = s * _jax.random.normal(km, w.shape, _jnp.float32)
        out["v_" + name] = (s * s) * _jax.random.uniform(kv, w.shape, _jnp.float32, 0.5, 1.5)
    if N_MICROBATCH > 1:
        for name, axis in PER_EXAMPLE_BATCH_AXIS.items():
            out[name] = _to_microbatches(out[name], axis)
    return {'x': out['x'], 'c': out['c'], 'rel_bias_table': out['rel_bias_table'], 'w_ada': out['w_ada'], 'b_ada': out['b_ada'], 'pre_norm_g': out['pre_norm_g'], 'post_norm_g': out['post_norm_g'], 'w_in': out['w_in'], 'attn_sink': out['attn_sink'], 'w_pool_group': out['w_pool_group'], 'pool_scale': out['pool_scale'], 'w_branch_attn': out['w_branch_attn'], 'w_branch_pool': out['w_branch_pool'], 'w_merge': out['w_merge'], 'b_merge': out['b_merge'], 'w_out': out['w_out'], 'loss_target': out['loss_target'], 'm_rel_bias_table': out['m_rel_bias_table'], 'm_w_ada': out['m_w_ada'], 'm_b_ada': out['m_b_ada'], 'm_pre_norm_g': out['m_pre_norm_g'], 'm_post_norm_g': out['m_post_norm_g'], 'm_w_in': out['m_w_in'], 'm_attn_sink': out['m_attn_sink'], 'm_w_pool_group': out['m_w_pool_group'], 'm_pool_scale': out['m_pool_scale'], 'm_w_branch_attn': out['m_w_branch_attn'], 'm_w_branch_pool': out['m_w_branch_pool'], 'm_w_merge': out['m_w_merge'], 'm_b_merge': out['m_b_merge'], 'm_w_out': out['m_w_out'], 'v_rel_bias_table': out['v_rel_bias_table'], 'v_w_ada': out['v_w_ada'], 'v_b_ada': out['v_b_ada'], 'v_pre_norm_g': out['v_pre_norm_g'], 'v_post_norm_g': out['v_post_norm_g'], 'v_w_in': out['v_w_in'], 'v_attn_sink': out['v_attn_sink'], 'v_w_pool_group': out['v_w_pool_group'], 'v_pool_scale': out['v_pool_scale'], 'v_w_branch_attn': out['v_w_branch_attn'], 'v_w_branch_pool': out['v_w_branch_pool'], 'v_w_merge': out['v_w_merge'], 'v_b_merge': out['v_b_merge'], 'v_w_out': out['v_w_out']}


def _loss(weights, diff, rest, loss_target):
    with _jax.named_scope("forward"):
        args = {**rest, TWIN_DIFF_INPUT: diff, **{k: w.astype(_WEIGHT_DTYPES[k]) for k, w in weights.items()}}
        y = _forward(args)
    with _jax.named_scope("loss_head"):
        err = _jnp.square(y.astype(_jnp.float32) - loss_target)
        return 0.5 * _jnp.sum(_jnp.mean(err, axis=-1)) if err.ndim else 0.5 * err


def _adamw(w, g, m, v):
    m = ADAM_B1 * m + (1.0 - ADAM_B1) * g
    v = ADAM_B2 * v + (1.0 - ADAM_B2) * _jnp.square(g)
    m_hat = m / (1.0 - ADAM_B1 ** ADAM_STEP)
    v_hat = v / (1.0 - ADAM_B2 ** ADAM_STEP)
    delta = -ADAM_LR * (m_hat / (_jnp.sqrt(v_hat) + ADAM_EPS) + ADAM_WD * w)
    return delta, m, v


def reference(x, c, rel_bias_table, w_ada, b_ada, pre_norm_g, post_norm_g, w_in, attn_sink, w_pool_group, pool_scale, w_branch_attn, w_branch_pool, w_merge, b_merge, w_out, loss_target, m_rel_bias_table, m_w_ada, m_b_ada, m_pre_norm_g, m_post_norm_g, m_w_in, m_attn_sink, m_w_pool_group, m_pool_scale, m_w_branch_attn, m_w_branch_pool, m_w_merge, m_b_merge, m_w_out, v_rel_bias_table, v_w_ada, v_b_ada, v_pre_norm_g, v_post_norm_g, v_w_in, v_attn_sink, v_w_pool_group, v_pool_scale, v_w_branch_attn, v_w_branch_pool, v_w_merge, v_b_merge, v_w_out):
    given = dict(x=x, c=c, rel_bias_table=rel_bias_table, w_ada=w_ada, b_ada=b_ada, pre_norm_g=pre_norm_g, post_norm_g=post_norm_g, w_in=w_in, attn_sink=attn_sink, w_pool_group=w_pool_group, pool_scale=pool_scale, w_branch_attn=w_branch_attn, w_branch_pool=w_branch_pool, w_merge=w_merge, b_merge=b_merge, w_out=w_out, loss_target=loss_target, m_rel_bias_table=m_rel_bias_table, m_w_ada=m_w_ada, m_b_ada=m_b_ada, m_pre_norm_g=m_pre_norm_g, m_post_norm_g=m_post_norm_g, m_w_in=m_w_in, m_attn_sink=m_attn_sink, m_w_pool_group=m_w_pool_group, m_pool_scale=m_pool_scale, m_w_branch_attn=m_w_branch_attn, m_w_branch_pool=m_w_branch_pool, m_w_merge=m_w_merge, m_b_merge=m_b_merge, m_w_out=m_w_out, v_rel_bias_table=v_rel_bias_table, v_w_ada=v_w_ada, v_b_ada=v_b_ada, v_pre_norm_g=v_pre_norm_g, v_post_norm_g=v_post_norm_g, v_w_in=v_w_in, v_attn_sink=v_attn_sink, v_w_pool_group=v_w_pool_group, v_pool_scale=v_pool_scale, v_w_branch_attn=v_w_branch_attn, v_w_branch_pool=v_w_branch_pool, v_w_merge=v_w_merge, v_b_merge=v_b_merge, v_w_out=v_w_out)
    weights = {n: given[n] for n in TWIN_WEIGHTS}
    shared = {n: given[n] for n in SHARED_INPUTS}
    per_example = {n: given[n] for n in ['x', 'c']}
    grad_fn = _jax.value_and_grad(_loss, argnums=(0, 1))

    def one_microbatch(ex, loss_target):
        ex = dict(ex)
        diff = ex.pop(TWIN_DIFF_INPUT)
        return grad_fn(weights, diff, {**shared, **ex}, loss_target)

    if N_MICROBATCH == 1:
        loss, (grad_w, grad_x) = one_microbatch(per_example, given["loss_target"])
    else:
        def body(carry, xs):
            loss_sum, grad_sum = carry
            l_k, (gw_k, gx_k) = one_microbatch(xs[0], xs[1])
            with _jax.named_scope("update"):
                return (loss_sum + l_k, _jax.tree.map(_jnp.add, grad_sum, gw_k)), gx_k

        init = (_jnp.zeros((), _jnp.float32), _jax.tree.map(_jnp.zeros_like, weights))
        (loss, grad_w), grad_x = _jax.lax.scan(body, init, (per_example, given["loss_target"]))
    with _jax.named_scope("update"):
        delta_w, new_m, new_v = {}, {}, {}
        for n in TWIN_WEIGHTS:
            delta_w[n], new_m[n], new_v[n] = _adamw(weights[n], grad_w[n], given["m_" + n], given["v_" + n])
    return (loss, grad_x, *[grad_w[n] for n in TWIN_WEIGHTS], *[delta_w[n] for n in TWIN_WEIGHTS],
            *[new_m[n] for n in TWIN_WEIGHTS], *[new_v[n] for n in TWIN_WEIGHTS])
```

```python
import functools
import math

import jax
import jax.numpy as jnp
from jax import lax
from jax.experimental import pallas as pl
from jax.experimental.pallas import tpu as pltpu

F32 = jnp.float32
BF16 = jnp.bfloat16

HEAD_DIM = 128
N_Q_HEADS = 8
N_KV_HEADS = 2
GQA_GROUP = N_Q_HEADS // N_KV_HEADS
ATTN_WIDTH = N_Q_HEADS * HEAD_DIM
KV_WIDTH = N_KV_HEADS * HEAD_DIM
WINDOW = 128
QBLK = 128
SPAN = QBLK + 2 * WINDOW
N_BUCKETS = 32
MAX_DISTANCE = 128
POOL_SIZES = (2, 4, 8, 16)
EPS = 1e-6
NEG_INF = -1e30
ATTN_SCALE = HEAD_DIM ** -0.5

ADAM_LR = 0.001
ADAM_B1 = 0.9
ADAM_B2 = 0.999
ADAM_EPS = 1e-08
ADAM_WD = 0.01
ADAM_STEP = 10

N_DEV = 8
MESH_AXES = ("x", "y", "c")
MESH = pl.DeviceIdType.MESH
POOL_HALO = 16
V7X_VMEM_LIMIT = 56 * 1024 * 1024

NT_DIMS = (((1,), (1,)), ((), ()))
TN_DIMS = (((0,), (0,)), ((), ()))


def _params(sem=None, vmem=V7X_VMEM_LIMIT):
    return pltpu.CompilerParams(dimension_semantics=sem, vmem_limit_bytes=vmem)


def _silu(v):
    return v * jax.nn.sigmoid(v)


def _dsilu(v):
    s = jax.nn.sigmoid(v)
    return s * (1.0 + v * (1.0 - s))


def _row_spec(tm, width):
    return pl.BlockSpec((tm, width), lambda i: (i, 0))


def _const_spec(shape):
    return pl.BlockSpec(shape, lambda *_: (0,) * len(shape))


ANY_SPEC = pl.BlockSpec(memory_space=pl.ANY)


def _load_blocked(w_blk, w_vmem, sems, axis):
    n = w_blk.shape[0]
    width = w_blk.shape[1 + axis]
    copies = []
    for d in range(n):
        dst = w_vmem.at[pl.ds(d * width, width), :] if axis == 0 else w_vmem.at[:, pl.ds(d * width, width)]
        copies.append(pltpu.make_async_copy(w_blk.at[d], dst, sems.at[d]))
    for cp in copies:
        cp.start()
    for cp in copies:
        cp.wait()


def _mesh_pos():
    return lax.axis_index("x"), lax.axis_index("y"), lax.axis_index("c")


def _allgather_small(blk, name):
    m_per, n = blk.shape

    def body(x_ref, out_ref, send_sems, recv_sems, local_sem):
        x, y, c = _mesh_pos()
        me, sibling = (x, y, c), (x, y, 1 - c)
        chips = [(1 - x, y), (x, 1 - y), (1 - x, 1 - y)]

        def rows(px, py, pc):
            return out_ref.at[pl.ds((4 * px + 2 * py + pc) * m_per, m_per), :]

        def copy(k, block, to, src=None):
            return pltpu.make_async_remote_copy(
                src_ref=rows(*block) if src is None else src, dst_ref=rows(*block),
                send_sem=send_sems.at[k], recv_sem=recv_sems.at[k], device_id=to, device_id_type=MESH)

        mine = pltpu.make_async_copy(x_ref, rows(*me), local_sem)
        mine.start()
        first = [copy(0, me, sibling, src=x_ref)]
        first += [copy(1 + j, me, (*chip, c), src=x_ref) for j, chip in enumerate(chips)]
        for cp in first:
            cp.start()
        passed = [copy(4 + j, (*chip, c), sibling) for j, chip in enumerate(chips)]
        for j, chip in enumerate(chips):
            copy(1 + j, (*chip, c), me).wait_recv()
            passed[j].start()
        copy(0, sibling, me).wait_recv()
        for j, chip in enumerate(chips):
            copy(4 + j, (*chip, 1 - c), me).wait_recv()
        for cp in first + passed:
            cp.wait_send()
        mine.wait()

    return pl.pallas_call(
        body, name=name,
        out_shape=jax.ShapeDtypeStruct((N_DEV * m_per, n), blk.dtype),
        in_specs=[pl.BlockSpec(memory_space=pltpu.VMEM)],
        out_specs=pl.BlockSpec(memory_space=pltpu.VMEM),
        scratch_shapes=[pltpu.SemaphoreType.DMA((7,)), pltpu.SemaphoreType.DMA((7,)), pltpu.SemaphoreType.DMA],
    )(blk)


def _allgather_hbm(shards, name):
    n_arr = len(shards)

    def body(*refs):
        ins, outs = refs[:n_arr], refs[n_arr:2 * n_arr]
        send_sems, recv_sems, local_sems = refs[2 * n_arr:]
        x, y, c = _mesh_pos()
        me, sibling = (x, y, c), (x, y, 1 - c)
        chips = [(1 - x, y), (x, 1 - y), (1 - x, 1 - y)]

        def slot(i, p):
            return outs[i].at[4 * p[0] + 2 * p[1] + p[2]]

        def copy(i, k, block, to, src=None):
            return pltpu.make_async_remote_copy(
                src_ref=slot(i, block) if src is None else src, dst_ref=slot(i, block),
                send_sem=send_sems.at[7 * i + k], recv_sem=recv_sems.at[7 * i + k],
                device_id=to, device_id_type=MESH)

        mine = [pltpu.make_async_copy(ins[i], slot(i, me), local_sems.at[i]) for i in range(n_arr)]
        for cp in mine:
            cp.start()
        first = []
        for i in range(n_arr):
            first.append(copy(i, 0, me, sibling, src=ins[i]))
            first += [copy(i, 1 + j, me, (*chip, c), src=ins[i]) for j, chip in enumerate(chips)]
        for cp in first:
            cp.start()
        passed = []
        for i in range(n_arr):
            for j, chip in enumerate(chips):
                copy(i, 1 + j, (*chip, c), me).wait_recv()
                fwd = copy(i, 4 + j, (*chip, c), sibling)
                fwd.start()
                passed.append(fwd)
        for i in range(n_arr):
            copy(i, 0, sibling, me).wait_recv()
            for j, chip in enumerate(chips):
                copy(i, 4 + j, (*chip, 1 - c), me).wait_recv()
        for cp in first + passed:
            cp.wait_send()
        for cp in mine:
            cp.wait()

    return pl.pallas_call(
        body, name=name,
        out_shape=[jax.ShapeDtypeStruct((N_DEV,) + s.shape, s.dtype) for s in shards],
        in_specs=[ANY_SPEC] * n_arr,
        out_specs=[ANY_SPEC] * n_arr,
        scratch_shapes=[pltpu.SemaphoreType.DMA((7 * n_arr,)), pltpu.SemaphoreType.DMA((7 * n_arr,)),
                        pltpu.SemaphoreType.DMA((n_arr,))],
    )(*shards)


def _alltoall_hbm(blocked, name):
    n_arr = len(blocked)
    flips = [(fx, fy, fc) for fx in (0, 1) for fy in (0, 1) for fc in (0, 1)][1:]

    def body(*refs):
        ins, outs = refs[:n_arr], refs[n_arr:2 * n_arr]
        send_sems, recv_sems, local_sems = refs[2 * n_arr:]
        x, y, c = _mesh_pos()
        me_idx = 4 * x + 2 * y + c

        def peer(f):
            return (1 - x if f[0] else x, 1 - y if f[1] else y, 1 - c if f[2] else c)

        def copy(i, k, dst_slot):
            p = peer(flips[k])
            return pltpu.make_async_remote_copy(
                src_ref=ins[i].at[4 * p[0] + 2 * p[1] + p[2]], dst_ref=outs[i].at[dst_slot],
                send_sem=send_sems.at[7 * i + k], recv_sem=recv_sems.at[7 * i + k],
                device_id=p, device_id_type=MESH)

        mine = [pltpu.make_async_copy(ins[i].at[me_idx], outs[i].at[me_idx], local_sems.at[i]) for i in range(n_arr)]
        for cp in mine:
            cp.start()
        sends = [copy(i, k, me_idx) for i in range(n_arr) for k in range(7)]
        for cp in sends:
            cp.start()
        for i in range(n_arr):
            for k in range(7):
                p = peer(flips[k])
                copy(i, k, 4 * p[0] + 2 * p[1] + p[2]).wait_recv()
        for cp in sends:
            cp.wait_send()
        for cp in mine:
            cp.wait()

    return pl.pallas_call(
        body, name=name,
        out_shape=[jax.ShapeDtypeStruct(b.shape, b.dtype) for b in blocked],
        in_specs=[ANY_SPEC] * n_arr,
        out_specs=[ANY_SPEC] * n_arr,
        scratch_shapes=[pltpu.SemaphoreType.DMA((7 * n_arr,)), pltpu.SemaphoreType.DMA((7 * n_arr,)),
                        pltpu.SemaphoreType.DMA((n_arr,))],
    )(*blocked)


def _ada_fwd(c16, w_shard, b_cols):
    rows, d = c16.shape
    nc = w_shard.shape[1]

    def body(c_ref, w_ref, b_ref, o_ref):
        sc = _silu(c_ref[...]).astype(BF16)
        o_ref[...] = jnp.dot(sc, w_ref[...].astype(BF16), preferred_element_type=F32) + b_ref[...]

    return pl.pallas_call(
        body, name="ada_fwd", out_shape=jax.ShapeDtypeStruct((rows, nc), F32),
        in_specs=[pl.BlockSpec(memory_space=pltpu.VMEM)] * 3, out_specs=pl.BlockSpec(memory_space=pltpu.VMEM),
        compiler_params=_params(),
    )(c16, w_shard, b_cols)


def _adamw(w, g, m, v):
    m2 = ADAM_B1 * m + (1.0 - ADAM_B1) * g
    v2 = ADAM_B2 * v + (1.0 - ADAM_B2) * (g * g)
    m_hat = m2 / (1.0 - ADAM_B1 ** ADAM_STEP)
    v_hat = v2 / (1.0 - ADAM_B2 ** ADAM_STEP)
    delta = -ADAM_LR * (m_hat / (jnp.sqrt(v_hat) + ADAM_EPS) + ADAM_WD * w)
    return delta, m2, v2


def _ada_bwd(c16, dmod16, w, m, v):
    rows, d = c16.shape
    nc = w.shape[1]
    tr = min(512, d)

    def body(c_ref, dm_ref, w_ref, m_ref, v_ref, g_ref, dl_ref, m2_ref, v2_ref):
        sc = _silu(c_ref[...]).astype(BF16)
        g = lax.dot_general(sc, dm_ref[...].astype(BF16), TN_DIMS, preferred_element_type=F32)
        g_ref[...] = g
        dl_ref[...], m2_ref[...], v2_ref[...] = _adamw(w_ref[...], g, m_ref[...], v_ref[...])

    spec = _row_spec(tr, nc)
    return pl.pallas_call(
        body, name="ada_bwd", grid=(d // tr,),
        out_shape=[jax.ShapeDtypeStruct((d, nc), F32)] * 4,
        in_specs=[pl.BlockSpec((rows, tr), lambda i: (0, i)), _const_spec((rows, nc)), spec, spec, spec],
        out_specs=[spec] * 4, compiler_params=_params(("arbitrary",)),
    )(c16, dmod16, w, m, v)


def _split3(v):
    hi = v.astype(BF16)
    r1 = v - hi.astype(F32)
    mid = r1.astype(BF16)
    lo = (r1 - mid.astype(F32)).astype(BF16)
    return hi, mid, lo


BIAS_CHUNK = QBLK * SPAN // 8


def _bias_build(table_t, idx):
    n = idx.shape[1]

    def body(t_ref, i_ref, o_ref):
        ids = i_ref[...]
        onehot = (lax.broadcasted_iota(jnp.int32, (N_BUCKETS, BIAS_CHUNK), 0) == ids).astype(BF16)
        t16 = jnp.concatenate([t_ref[...], jnp.zeros_like(t_ref[...])], axis=0)
        acc = jnp.zeros((2 * N_Q_HEADS, BIAS_CHUNK), F32)
        for part in _split3(t16):
            acc = acc + jnp.dot(part, onehot, preferred_element_type=F32)
        o_ref[...] = jnp.where(ids >= 0, acc[:N_Q_HEADS], NEG_INF)

    return pl.pallas_call(
        body, name="bias_build", grid=(n // BIAS_CHUNK,),
        out_shape=jax.ShapeDtypeStruct((N_Q_HEADS, n), F32),
        in_specs=[_const_spec((N_Q_HEADS, N_BUCKETS)), pl.BlockSpec((1, BIAS_CHUNK), lambda i: (0, i))],
        out_specs=pl.BlockSpec((N_Q_HEADS, BIAS_CHUNK), lambda i: (0, i)),
        compiler_params=_params(("arbitrary",)),
    )(table_t, idx)


def _bias_reduce(dbias, idx):
    n = idx.shape[1]

    def body(d_ref, i_ref, o_ref):
        @pl.when(pl.program_id(0) == 0)
        def _():
            o_ref[...] = jnp.zeros_like(o_ref)

        onehot = (lax.broadcasted_iota(jnp.int32, (N_BUCKETS, BIAS_CHUNK), 0) == i_ref[...]).astype(BF16)
        d16 = jnp.concatenate([d_ref[...], jnp.zeros_like(d_ref[...])], axis=0)
        acc = jnp.zeros((2 * N_Q_HEADS, N_BUCKETS), F32)
        for part in _split3(d16):
            acc = acc + lax.dot_general(part, onehot, NT_DIMS, preferred_element_type=F32)
        o_ref[...] += acc[:N_Q_HEADS]

    return pl.pallas_call(
        body, name="bias_reduce", grid=(n // BIAS_CHUNK,),
        out_shape=jax.ShapeDtypeStruct((N_Q_HEADS, N_BUCKETS), F32),
        in_specs=[pl.BlockSpec((N_Q_HEADS, BIAS_CHUNK), lambda i: (0, i)), pl.BlockSpec((1, BIAS_CHUNK), lambda i: (0, i))],
        out_specs=_const_spec((N_Q_HEADS, N_BUCKETS)),
        compiler_params=_params(("arbitrary",)),
    )(dbias, idx)


def _t5_bucket(rel):
    half = N_BUCKETS // 2
    max_exact = half // 2
    ret = jnp.where(rel > 0, half, 0)
    n = jnp.abs(rel)
    nf = jnp.maximum(n, 1).astype(jnp.float32)
    large = max_exact + (jnp.log(nf / max_exact) / math.log(MAX_DISTANCE / max_exact)
                         * (half - max_exact)).astype(jnp.int32)
    large = jnp.minimum(large, half - 1)
    return ret + jnp.where(n < max_exact, n, large)


def _fwd_in(x2, vec, w_full):
    s, d = x2.shape
    pw = d // 2
    tm = min(512, s)
    widths = (ATTN_WIDTH + 2 * KV_WIDTH, ATTN_WIDTH, pw, pw)
    dtypes = (BF16, BF16, F32, BF16)

    def body(x_ref, vec_ref, w_hbm, h_ref, qkv_ref, ga_ref, u_ref, gp_ref, w_vmem, sem):
        @pl.when(pl.program_id(0) == 0)
        def _():
            cp = pltpu.make_async_copy(w_hbm, w_vmem, sem)
            cp.start()
            cp.wait()

        xf = x_ref[...]
        r = lax.rsqrt(jnp.mean(xf * xf, axis=-1, keepdims=True) + EPS)
        h = ((xf * r) * vec_ref[2:3, :]) * vec_ref[0:1, :] + vec_ref[1:2, :]
        hb = h.astype(BF16)
        h_ref[...] = hb
        off = 0
        for o_ref, wd in zip((qkv_ref, ga_ref, u_ref, gp_ref), widths):
            for c0 in range(0, wd, 512):
                c1 = min(c0 + 512, wd)
                o_ref[:, c0:c1] = jnp.dot(hb, w_vmem[:, off + c0:off + c1],
                                          preferred_element_type=F32).astype(o_ref.dtype)
            off += wd

    return pl.pallas_call(
        body, name="fwd_in", grid=(s // tm,),
        out_shape=[jax.ShapeDtypeStruct((s, d), BF16)] + [jax.ShapeDtypeStruct((s, wd), dt) for wd, dt in zip(widths, dtypes)],
        in_specs=[_row_spec(tm, d), _const_spec((8, d)), ANY_SPEC],
        out_specs=[_row_spec(tm, d)] + [_row_spec(tm, wd) for wd in widths],
        scratch_shapes=[pltpu.VMEM(w_full.shape, BF16), pltpu.SemaphoreType.DMA],
        compiler_params=_params(("arbitrary",)),
    )(x2, vec, w_full)


def _merge_fwd(h, w_blk, b_merge):
    s, d = h.shape
    n = w_blk.shape[0] * w_blk.shape[2]
    tm = min(512, s)

    def body(h_ref, w_hbm, b_ref, g_ref, w_vmem, sems):
        @pl.when(pl.program_id(0) == 0)
        def _():
            _load_blocked(w_hbm, w_vmem, sems, 1)

        hb = h_ref[...]
        for c0 in range(0, n, 512):
            z = jnp.dot(hb, w_vmem[:, c0:c0 + 512], preferred_element_type=F32) + b_ref[:, c0:c0 + 512]
            g_ref[:, c0:c0 + 512] = jax.nn.sigmoid(z).astype(BF16)

    return pl.pallas_call(
        body, name="merge_fwd", grid=(s // tm,),
        out_shape=jax.ShapeDtypeStruct((s, n), BF16),
        in_specs=[_row_spec(tm, d), ANY_SPEC, _const_spec((1, n))],
        out_specs=_row_spec(tm, n),
        scratch_shapes=[pltpu.VMEM((d, n), BF16), pltpu.SemaphoreType.DMA((N_DEV,))],
        compiler_params=_params(("arbitrary",)),
    )(h, w_blk, b_merge)


def _attn_probs(n, s_len, q, kv, bias_ref, sink_ref, k):
    kpos = n * QBLK - WINDOW + lax.broadcasted_iota(jnp.int32, (1, SPAN), 1)
    key_ok = (kpos >= 0) & (kpos < s_len)
    heads = [GQA_GROUP * k + g for g in range(GQA_GROUP)]
    kk = kv[:, k * HEAD_DIM:(k + 1) * HEAD_DIM]
    vk = kv[:, KV_WIDTH + k * HEAD_DIM:KV_WIDTH + (k + 1) * HEAD_DIM]
    qk = jnp.concatenate([q[:, h * HEAD_DIM:(h + 1) * HEAD_DIM] for h in heads], axis=0)
    logits = lax.dot_general(qk, kk, NT_DIMS, preferred_element_type=F32) * ATTN_SCALE
    bias = jnp.concatenate([bias_ref[h] for h in heads], axis=0)
    logits = jnp.where(key_ok, logits + bias, NEG_INF)
    sink = jnp.concatenate([jnp.full((QBLK, 1), sink_ref[h], F32) for h in heads], axis=0)
    mx = jnp.maximum(jnp.max(logits, axis=-1, keepdims=True), sink)
    e = jnp.exp(logits - mx)
    e_sink = jnp.exp(sink - mx)
    inv = 1.0 / (jnp.sum(e, axis=-1, keepdims=True) + e_sink)
    return qk, kk, vk, e * inv, e_sink * inv


def _attn_fwd(qkv, kvp, ga, bias, sink):
    s = qkv.shape[0]
    nblk = s // QBLK
    kv_specs = [pl.BlockSpec((QBLK, 2 * KV_WIDTH), (lambda n, j=j: (n + j, 0))) for j in range(3)]

    def body(sink_ref, q_ref, k0, k1, k2, ga_ref, bias_ref, a_ref, ya_ref):
        n = pl.program_id(0)
        kv = jnp.concatenate([k0[...], k1[...], k2[...]], axis=0)
        q = q_ref[...]
        for k in range(N_KV_HEADS):
            _, _, vk, p, _ = _attn_probs(n, s, q, kv, bias_ref, sink_ref, k)
            o = jnp.dot(p.astype(BF16), vk, preferred_element_type=F32)
            for g in range(GQA_GROUP):
                c0 = (GQA_GROUP * k + g) * HEAD_DIM
                og = o[g * QBLK:(g + 1) * QBLK]
                a_ref[:, c0:c0 + HEAD_DIM] = og.astype(BF16)
                ya_ref[:, c0:c0 + HEAD_DIM] = (og * _silu(ga_ref[:, c0:c0 + HEAD_DIM].astype(F32))).astype(BF16)

    return pl.pallas_call(
        body, name="attn_fwd", grid=(nblk,),
        out_shape=[jax.ShapeDtypeStruct((s, ATTN_WIDTH), BF16)] * 2,
        in_specs=[pl.BlockSpec(memory_space=pltpu.SMEM), _row_spec(QBLK, ATTN_WIDTH)] + kv_specs
                 + [_row_spec(QBLK, ATTN_WIDTH), _const_spec((N_Q_HEADS, QBLK, SPAN))],
        out_specs=[_row_spec(QBLK, ATTN_WIDTH)] * 2,
        compiler_params=_params(("arbitrary",)),
    )(sink, qkv, kvp, kvp, kvp, ga, bias)


def _roll_rows(a, k):
    return pltpu.roll(a, k % a.shape[0], axis=0)


def _window_sum(xe, w, mirrored):
    acc = xe + _roll_rows(xe, -1 if mirrored else 1)
    width = 2
    while width < w:
        half = width // 2
        acc = _roll_rows(acc, half) + _roll_rows(acc, -half)
        width *= 2
    return acc


def _halo_specs(tm, width, s):
    nh = tm // POOL_HALO
    last = s // POOL_HALO - 1
    prev = pl.BlockSpec((POOL_HALO, width), lambda i: (jnp.maximum(i * nh - 1, 0), 0))
    nxt = pl.BlockSpec((POOL_HALO, width), lambda i: (jnp.minimum((i + 1) * nh, last), 0))
    return [prev, _row_spec(tm, width), nxt]


def _with_halo(prev_ref, cur_ref, next_ref, i, n_tiles):
    prev = jnp.where(i > 0, prev_ref[...].astype(F32), 0.0)
    nxt = jnp.where(i < n_tiles - 1, next_ref[...].astype(F32), 0.0)
    return jnp.concatenate([prev, cur_ref[...].astype(F32), nxt], axis=0)


def _pool_counts(i, tm, s, w):
    pos = i * tm + lax.broadcasted_iota(jnp.int32, (tm, 1), 0)
    lo = jnp.maximum(pos - w // 2, 0)
    hi = jnp.minimum(pos + w // 2, s)
    return (hi - lo).astype(F32)


def _pool_fwd(u, gp, wp, scale):
    s, pw = u.shape
    c = pw // len(POOL_SIZES)
    tm = min(512, s)
    nt = s // tm

    def body(up, uc, un, gp_ref, wp_ref, sc_ref, yp_ref):
        i = pl.program_id(0)
        ue = _with_halo(up, uc, un, i, nt)
        for g, w in enumerate(POOL_SIZES):
            cols = slice(g * c, (g + 1) * c)
            xg = ue[:, cols]
            ws = _window_sum(xg, w, False)[POOL_HALO:POOL_HALO + tm]
            pooled = ws / _pool_counts(i, tm, s, w) - xg[POOL_HALO:POOL_HALO + tm]
            mixed = jnp.dot(pooled.astype(BF16), wp_ref[g], preferred_element_type=F32)
            yp_ref[:, cols] = (mixed * sc_ref[:, cols] * _silu(gp_ref[:, cols].astype(F32))).astype(BF16)

    return pl.pallas_call(
        body, name="pool_fwd", grid=(nt,),
        out_shape=jax.ShapeDtypeStruct((s, pw), BF16),
        in_specs=_halo_specs(tm, pw, s) + [_row_spec(tm, pw), _const_spec(wp.shape), _const_spec((1, pw))],
        out_specs=_row_spec(tm, pw),
        compiler_params=_params(("arbitrary",)),
    )(u, u, u, gp, wp, scale)


def _branch_fwd(ya, yp, g, wa_blk, wp_blk):
    s = ya.shape[0]
    d = g.shape[1] // 2
    ka, kp = ya.shape[1], yp.shape[1]
    tm = min(512, s)

    def body(ya_ref, yp_ref, g_ref, wa_hbm, wp_hbm, ba_ref, bp_ref, m_ref, wa, wp, sems):
        @pl.when(pl.program_id(0) == 0)
        def _():
            _load_blocked(wa_hbm, wa, sems.at[pl.ds(0, N_DEV)], 1)
            _load_blocked(wp_hbm, wp, sems.at[pl.ds(N_DEV, N_DEV)], 1)

        for c0 in range(0, d, 512):
            cols = slice(c0, c0 + 512)
            ba = jnp.dot(ya_ref[...], wa[:, cols], preferred_element_type=F32)
            bp = jnp.dot(yp_ref[...], wp[:, cols], preferred_element_type=F32)
            ba_ref[:, cols] = ba.astype(BF16)
            bp_ref[:, cols] = bp.astype(BF16)
            ga = g_ref[:, c0:c0 + 512].astype(F32)
            gb = g_ref[:, d + c0:d + c0 + 512].astype(F32)
            m_ref[:, cols] = (ga * ba + gb * bp).astype(BF16)

    return pl.pallas_call(
        body, name="branch_fwd", grid=(s // tm,),
        out_shape=[jax.ShapeDtypeStruct((s, d), BF16)] * 3,
        in_specs=[_row_spec(tm, ka), _row_spec(tm, kp), _row_spec(tm, 2 * d), ANY_SPEC, ANY_SPEC],
        out_specs=[_row_spec(tm, d)] * 3,
        scratch_shapes=[pltpu.VMEM((ka, d), BF16), pltpu.VMEM((kp, d), BF16), pltpu.SemaphoreType.DMA((2 * N_DEV,))],
        compiler_params=_params(("arbitrary",)),
    )(ya, yp, g, wa_blk, wp_blk)


def _out_fused(merged, x2, t2, g, ba, bp, vec, wo_blk):
    s, d = x2.shape
    tm = min(256, s)

    def body(m_ref, x_ref, t_ref, g_ref, ba_ref, bp_ref, vec_ref, wo_hbm,
             do_ref, dgm_ref, dba_ref, dbp_ref, dy_ref, acc_ref, dbm_ref, wo, sems):
        @pl.when(pl.program_id(0) == 0)
        def _():
            _load_blocked(wo_hbm, wo, sems, 0)
            acc_ref[...] = jnp.zeros_like(acc_ref)
            dbm_ref[...] = jnp.zeros_like(dbm_ref)

        gate = vec_ref[0:1, :]
        post_g = vec_ref[1:2, :]
        o = jnp.dot(m_ref[...], wo[...], preferred_element_type=F32)
        r2 = lax.rsqrt(jnp.mean(o * o, axis=-1, keepdims=True) + EPS)
        on = o * r2
        z = on * post_g
        diff = x_ref[...] + gate * z - t_ref[...]
        dy = diff * (1.0 / d)
        dy_ref[...] = dy
        dz = dy * gate
        don = dz * post_g
        do = r2 * (don - on * jnp.mean(don * on, axis=-1, keepdims=True))
        acc_ref[0:1, :] += jnp.sum(diff * diff, axis=0, keepdims=True)
        acc_ref[1:2, :] += jnp.sum(dy * z, axis=0, keepdims=True)
        acc_ref[2:3, :] += jnp.sum(dz * on, axis=0, keepdims=True)
        dob = do.astype(BF16)
        do_ref[...] = dob
        for c0 in range(0, d, 512):
            cols = slice(c0, c0 + 512)
            dm = lax.dot_general(dob, wo[c0:c0 + 512, :], NT_DIMS, preferred_element_type=F32)
            ga = g_ref[:, c0:c0 + 512].astype(F32)
            gb = g_ref[:, d + c0:d + c0 + 512].astype(F32)
            b_a = ba_ref[:, cols].astype(F32)
            b_p = bp_ref[:, cols].astype(F32)
            dba_ref[:, cols] = (dm * ga).astype(BF16)
            dbp_ref[:, cols] = (dm * gb).astype(BF16)
            dga = dm * b_a * ga * (1.0 - ga)
            dgb = dm * b_p * gb * (1.0 - gb)
            dgm_ref[:, c0:c0 + 512] = dga.astype(BF16)
            dgm_ref[:, d + c0:d + c0 + 512] = dgb.astype(BF16)
            dbm_ref[:, c0:c0 + 512] += jnp.sum(dga, axis=0, keepdims=True)
            dbm_ref[:, d + c0:d + c0 + 512] += jnp.sum(dgb, axis=0, keepdims=True)

    return pl.pallas_call(
        body, name="out_fused", grid=(s // tm,),
        out_shape=[jax.ShapeDtypeStruct((s, d), BF16), jax.ShapeDtypeStruct((s, 2 * d), BF16),
                   jax.ShapeDtypeStruct((s, d), BF16), jax.ShapeDtypeStruct((s, d), BF16),
                   jax.ShapeDtypeStruct((s, d), F32), jax.ShapeDtypeStruct((8, d), F32),
                   jax.ShapeDtypeStruct((1, 2 * d), F32)],
        in_specs=[_row_spec(tm, d), _row_spec(tm, d), _row_spec(tm, d), _row_spec(tm, 2 * d),
                  _row_spec(tm, d), _row_spec(tm, d), _const_spec((8, d)), ANY_SPEC],
        out_specs=[_row_spec(tm, d), _row_spec(tm, 2 * d), _row_spec(tm, d), _row_spec(tm, d),
                   _row_spec(tm, d), _const_spec((8, d)), _const_spec((1, 2 * d))],
        scratch_shapes=[pltpu.VMEM((d, d), BF16), pltpu.SemaphoreType.DMA((N_DEV,))],
        compiler_params=_params(("arbitrary",)),
    )(merged, x2, t2, g, ba, bp, vec, wo_blk)


def _wgrad(a, b, tn, shard_w, name):
    s, k = a.shape
    n = b.shape[1]
    tk = min(512, s)
    ns = s // tk
    per = tn // shard_w

    def body(a_ref, b_ref, o_ref, acc):
        step = pl.program_id(1)

        @pl.when(step == 0)
        def _():
            acc[...] = jnp.zeros_like(acc)

        acc[...] += lax.dot_general(a_ref[...], b_ref[...], TN_DIMS, preferred_element_type=F32)

        @pl.when(step == ns - 1)
        def _():
            for r in range(per):
                o_ref[r] = acc[:, r * shard_w:(r + 1) * shard_w].astype(BF16)

    return pl.pallas_call(
        body, name=name, grid=(n // tn, ns),
        out_shape=jax.ShapeDtypeStruct((n // shard_w, k, shard_w), BF16),
        in_specs=[pl.BlockSpec((tk, k), lambda j, t: (t, 0)), pl.BlockSpec((tk, tn), lambda j, t: (t, j))],
        out_specs=pl.BlockSpec((per, k, shard_w), lambda j, t: (j, 0, 0)),
        scratch_shapes=[pltpu.VMEM((k, tn), F32)],
        compiler_params=_params(("arbitrary", "arbitrary")),
    )(a, b)


def _branch_bwd(dba, dbp, a, ga, wa_blk, wp_blk):
    s, d = dba.shape
    ka = wa_blk.shape[1]
    kp = wp_blk.shape[1]
    tm = min(512, s)

    def body(dba_ref, dbp_ref, a_ref, ga_ref, wa_hbm, wp_hbm, da_ref, dga_ref, dyp_ref, wa, wp, sems):
        @pl.when(pl.program_id(0) == 0)
        def _():
            _load_blocked(wa_hbm, wa, sems.at[pl.ds(0, N_DEV)], 1)
            _load_blocked(wp_hbm, wp, sems.at[pl.ds(N_DEV, N_DEV)], 1)

        for c0 in range(0, ka, 512):
            cols = slice(c0, c0 + 512)
            dya = lax.dot_general(dba_ref[...], wa[c0:c0 + 512, :], NT_DIMS, preferred_element_type=F32)
            gaf = ga_ref[:, cols].astype(F32)
            da_ref[:, cols] = (dya * _silu(gaf)).astype(BF16)
            dga_ref[:, cols] = (dya * a_ref[:, cols].astype(F32) * _dsilu(gaf)).astype(BF16)
        for c0 in range(0, kp, 512):
            dyp = lax.dot_general(dbp_ref[...], wp[c0:c0 + 512, :], NT_DIMS, preferred_element_type=F32)
            dyp_ref[:, c0:c0 + 512] = dyp.astype(BF16)

    return pl.pallas_call(
        body, name="branch_bwd", grid=(s // tm,),
        out_shape=[jax.ShapeDtypeStruct((s, ka), BF16), jax.ShapeDtypeStruct((s, ka), BF16),
                   jax.ShapeDtypeStruct((s, kp), BF16)],
        in_specs=[_row_spec(tm, d), _row_spec(tm, d), _row_spec(tm, ka), _row_spec(tm, ka), ANY_SPEC, ANY_SPEC],
        out_specs=[_row_spec(tm, ka), _row_spec(tm, ka), _row_spec(tm, kp)],
        scratch_shapes=[pltpu.VMEM((ka, d), BF16), pltpu.VMEM((kp, d), BF16), pltpu.SemaphoreType.DMA((2 * N_DEV,))],
        compiler_params=_params(("arbitrary",)),
    )(dba, dbp, a, ga, wa_blk, wp_blk)


def _attn_bwd(qkv, kvp, a, da, bias, sink):
    s = qkv.shape[0]
    nblk = s // QBLK
    clamp = lambda n: jnp.minimum(n, nblk - 1)
    q_spec = pl.BlockSpec((QBLK, ATTN_WIDTH), lambda n: (clamp(n), 0))
    kv_specs = [pl.BlockSpec((QBLK, 2 * KV_WIDTH), (lambda n, j=j: (clamp(n) + j, 0))) for j in range(3)]

    def body(sink_ref, q_ref, k0, k1, k2, a_ref, da_ref, bias_ref, dq_ref, dkv_ref, db_ref, ds_ref, acc, sacc):
        n = pl.program_id(0)

        @pl.when(n == 0)
        def _():
            acc[...] = jnp.zeros_like(acc)
            sacc[...] = jnp.zeros_like(sacc)
            db_ref[...] = jnp.zeros_like(db_ref)

        @pl.when(n < nblk)
        def _():
            kv = jnp.concatenate([k0[...], k1[...], k2[...]], axis=0)
            q = q_ref[...]
            dks, dvs = [], []
            for k in range(N_KV_HEADS):
                heads = [GQA_GROUP * k + g for g in range(GQA_GROUP)]
                qk, kk, vk, p, p_sink = _attn_probs(n, s, q, kv, bias_ref, sink_ref, k)
                o = jnp.concatenate([a_ref[:, h * HEAD_DIM:(h + 1) * HEAD_DIM] for h in heads], axis=0)
                do = jnp.concatenate([da_ref[:, h * HEAD_DIM:(h + 1) * HEAD_DIM] for h in heads], axis=0)
                dp = lax.dot_general(do, vk, NT_DIMS, preferred_element_type=F32)
                delta = jnp.sum(do.astype(F32) * o.astype(F32), axis=-1, keepdims=True)
                dlog = p * (dp - delta)
                sacc[k] += -(p_sink * delta)
                db_ref[k] += dlog
                dlb = (dlog * ATTN_SCALE).astype(BF16)
                dq = jnp.dot(dlb, kk, preferred_element_type=F32)
                for g, h in enumerate(heads):
                    dq_ref[:, h * HEAD_DIM:(h + 1) * HEAD_DIM] = dq[g * QBLK:(g + 1) * QBLK].astype(BF16)
                dks.append(lax.dot_general(dlb, qk, TN_DIMS, preferred_element_type=F32))
                dvs.append(lax.dot_general(p.astype(BF16), do, TN_DIMS, preferred_element_type=F32))
            contrib = jnp.concatenate(dks + dvs, axis=1)
            acc[n % 3] += contrib[0:QBLK]
            acc[(n + 1) % 3] += contrib[QBLK:2 * QBLK]
            acc[(n + 2) % 3] = contrib[2 * QBLK:3 * QBLK]

        dkv_ref[...] = acc[n % 3].astype(BF16)

        @pl.when(n == nblk)
        def _():
            for k in range(N_KV_HEADS):
                for g in range(GQA_GROUP):
                    h = GQA_GROUP * k + g
                    tot = jnp.sum(sacc[k, g * QBLK:(g + 1) * QBLK, :])
                    ds_ref[h:h + 1, :] = jnp.full((1, 128), tot, F32)

    return pl.pallas_call(
        body, name="attn_bwd", grid=(nblk + 1,),
        out_shape=[jax.ShapeDtypeStruct((s, ATTN_WIDTH), BF16), jax.ShapeDtypeStruct((s, 2 * KV_WIDTH), BF16),
                   jax.ShapeDtypeStruct((N_KV_HEADS, GQA_GROUP * QBLK, SPAN), F32),
                   jax.ShapeDtypeStruct((N_Q_HEADS, 128), F32)],
        in_specs=[pl.BlockSpec(memory_space=pltpu.SMEM), q_spec] + kv_specs
                 + [q_spec, q_spec, _const_spec((N_Q_HEADS, QBLK, SPAN))],
        out_specs=[q_spec, pl.BlockSpec((QBLK, 2 * KV_WIDTH), lambda n: (jnp.maximum(n - 1, 0), 0)),
                   _const_spec((N_KV_HEADS, GQA_GROUP * QBLK, SPAN)), _const_spec((N_Q_HEADS, 128))],
        scratch_shapes=[pltpu.VMEM((3, QBLK, 2 * KV_WIDTH), F32), pltpu.VMEM((N_KV_HEADS, GQA_GROUP * QBLK, 1), F32)],
        compiler_params=_params(("arbitrary",)),
    )(sink, qkv, kvp, kvp, kvp, a, da, bias)


def _pool_bwd_rows(u, gp, dyp, wp, scale):
    s, pw = u.shape
    c = pw // len(POOL_SIZES)
    tm = min(512, s)
    nt = s // tm

    def body(up, uc, un, gp_ref, dyp_ref, wp_ref, sc_ref, e_ref, dgp_ref, gwp_ref, dsc_ref):
        i = pl.program_id(0)

        @pl.when(i == 0)
        def _():
            gwp_ref[...] = jnp.zeros_like(gwp_ref)
            dsc_ref[...] = jnp.zeros_like(dsc_ref)

        ue = _with_halo(up, uc, un, i, nt)
        for g, w in enumerate(POOL_SIZES):
            cols = slice(g * c, (g + 1) * c)
            xg = ue[:, cols]
            cnt = _pool_counts(i, tm, s, w)
            ws = _window_sum(xg, w, False)[POOL_HALO:POOL_HALO + tm]
            pooled = (ws / cnt - xg[POOL_HALO:POOL_HALO + tm]).astype(BF16)
            mixed = jnp.dot(pooled, wp_ref[g], preferred_element_type=F32)
            sc = sc_ref[:, cols]
            gpf = gp_ref[:, cols].astype(F32)
            dyp_g = dyp_ref[:, cols].astype(F32)
            dp = dyp_g * _silu(gpf)
            dgp_ref[:, cols] = (dyp_g * (mixed * sc) * _dsilu(gpf)).astype(BF16)
            dsc_ref[:, cols] += jnp.sum(dp * mixed, axis=0, keepdims=True)
            dmixed = (dp * sc).astype(BF16)
            dpooled = lax.dot_general(dmixed, wp_ref[g], NT_DIMS, preferred_element_type=F32)
            gwp_ref[g] += lax.dot_general(pooled, dmixed, TN_DIMS, preferred_element_type=F32)
            e_ref[:, cols] = dpooled / cnt

    return pl.pallas_call(
        body, name="pool_bwd_rows", grid=(nt,),
        out_shape=[jax.ShapeDtypeStruct((s, pw), F32), jax.ShapeDtypeStruct((s, pw), BF16),
                   jax.ShapeDtypeStruct(wp.shape, F32), jax.ShapeDtypeStruct((1, pw), F32)],
        in_specs=_halo_specs(tm, pw, s) + [_row_spec(tm, pw), _row_spec(tm, pw), _const_spec(wp.shape), _const_spec((1, pw))],
        out_specs=[_row_spec(tm, pw), _row_spec(tm, pw), _const_spec(wp.shape), _const_spec((1, pw))],
        compiler_params=_params(("arbitrary",)),
    )(u, u, u, gp, dyp, wp, scale)


def _pool_bwd_window(e):
    s, pw = e.shape
    c = pw // len(POOL_SIZES)
    tm = min(512, s)
    nt = s // tm

    def body(ep, ec, en, du_ref):
        i = pl.program_id(0)
        ee = _with_halo(ep, ec, en, i, nt)
        for g, w in enumerate(POOL_SIZES):
            cols = slice(g * c, (g + 1) * c)
            eg = ee[:, cols]
            ws = _window_sum(eg, w, True)[POOL_HALO:POOL_HALO + tm]
            du_ref[:, cols] = (ws - eg[POOL_HALO:POOL_HALO + tm] * _pool_counts(i, tm, s, w)).astype(BF16)

    return pl.pallas_call(
        body, name="pool_bwd_window", grid=(nt,),
        out_shape=jax.ShapeDtypeStruct((s, pw), BF16),
        in_specs=_halo_specs(tm, pw, s), out_specs=_row_spec(tm, pw),
        compiler_params=_params(("arbitrary",)),
    )(e, e, e)


def _dh_merge(dgm, w_blk):
    s, n = dgm.shape
    d = w_blk.shape[1]
    tm = min(512, s)

    def body(dg_ref, w_hbm, o_ref, w_vmem, sems):
        @pl.when(pl.program_id(0) == 0)
        def _():
            _load_blocked(w_hbm, w_vmem, sems, 1)

        for r0 in range(0, d, 512):
            o_ref[:, r0:r0 + 512] = lax.dot_general(dg_ref[...], w_vmem[r0:r0 + 512, :], NT_DIMS,
                                                    preferred_element_type=F32)

    return pl.pallas_call(
        body, name="dh_merge", grid=(s // tm,),
        out_shape=jax.ShapeDtypeStruct((s, d), F32),
        in_specs=[_row_spec(tm, n), ANY_SPEC], out_specs=_row_spec(tm, d),
        scratch_shapes=[pltpu.VMEM((d, n), BF16), pltpu.SemaphoreType.DMA((N_DEV,))],
        compiler_params=_params(("arbitrary",)),
    )(dgm, w_blk)


def _dh_final(dh1, pieces, w_full, x2, dy, vec):
    s, d = x2.shape
    tm = min(256, s)
    widths = [p.shape[1] for p in pieces]
    n_p = len(pieces)

    def body(*refs):
        dh1_ref = refs[0]
        p_refs = refs[1:1 + n_p]
        w_hbm, x_ref, dy_ref, vec_ref, gx_ref, acc_ref, w_vmem, sem = refs[1 + n_p:]

        @pl.when(pl.program_id(0) == 0)
        def _():
            cp = pltpu.make_async_copy(w_hbm, w_vmem, sem)
            cp.start()
            cp.wait()
            acc_ref[...] = jnp.zeros_like(acc_ref)

        scale1 = vec_ref[0:1, :]
        pre_g = vec_ref[2:3, :]
        xf = x_ref[...]
        r = lax.rsqrt(jnp.mean(xf * xf, axis=-1, keepdims=True) + EPS)
        xn = xf * r
        hn = xn * pre_g
        for r0 in range(0, d, 512):
            rows = slice(r0, r0 + 512)
            dh = dh1_ref[:, rows]
            off = 0
            for p_ref, wd in zip(p_refs, widths):
                dh = dh + lax.dot_general(p_ref[...], w_vmem[r0:r0 + 512, off:off + wd], NT_DIMS,
                                          preferred_element_type=F32)
                off += wd
            acc_ref[0:1, rows] += jnp.sum(dh, axis=0, keepdims=True)
            acc_ref[1:2, rows] += jnp.sum(dh * hn[:, rows], axis=0, keepdims=True)
            dhn = dh * scale1[:, rows]
            acc_ref[2:3, rows] += jnp.sum(dhn * xn[:, rows], axis=0, keepdims=True)
            gx_ref[:, rows] = dhn * pre_g[:, rows]
        dxn = gx_ref[...]
        dx = r * (dxn - xn * jnp.mean(dxn * xn, axis=-1, keepdims=True))
        gx_ref[...] = dy_ref[...] + dx

    return pl.pallas_call(
        body, name="dh_final", grid=(s // tm,),
        out_shape=[jax.ShapeDtypeStruct((s, d), F32), jax.ShapeDtypeStruct((8, d), F32)],
        in_specs=[_row_spec(tm, d)] + [_row_spec(tm, wd) for wd in widths]
                 + [ANY_SPEC, _row_spec(tm, d), _row_spec(tm, d), _const_spec((8, d))],
        out_specs=[_row_spec(tm, d), _const_spec((8, d))],
        scratch_shapes=[pltpu.VMEM(w_full.shape, BF16), pltpu.SemaphoreType.DMA],
        compiler_params=_params(("arbitrary",)),
    )(dh1, *pieces, w_full, x2, dy, vec)


def _reduce_adamw(recv, w, m, v, name):
    r, c = w.shape
    tr = min(256, r)

    def body(p_ref, w_ref, m_ref, v_ref, g_ref, dl_ref, m2_ref, v2_ref):
        g = p_ref[0].astype(F32)
        for dev in range(1, N_DEV):
            g = g + p_ref[dev].astype(F32)
        g_ref[...] = g
        dl_ref[...], m2_ref[...], v2_ref[...] = _adamw(w_ref[...], g, m_ref[...], v_ref[...])

    spec = _row_spec(tr, c)
    return pl.pallas_call(
        body, name=name, grid=(r // tr,),
        out_shape=[jax.ShapeDtypeStruct((r, c), F32)] * 4,
        in_specs=[pl.BlockSpec((N_DEV, tr, c), lambda i: (0, i, 0)), spec, spec, spec],
        out_specs=[spec] * 4, compiler_params=_params(("arbitrary",)),
    )(recv, w, m, v)


def _small_reduce_adamw(gathered, w, m, v):
    rows, c = w.shape

    def body(p_ref, w_ref, m_ref, v_ref, g_ref, dl_ref, m2_ref, v2_ref):
        g = p_ref[0:rows, :]
        for dev in range(1, N_DEV):
            g = g + p_ref[dev * rows:(dev + 1) * rows, :]
        g_ref[...] = g
        dl_ref[...], m2_ref[...], v2_ref[...] = _adamw(w_ref[...], g, m_ref[...], v_ref[...])

    vm = pl.BlockSpec(memory_space=pltpu.VMEM)
    return pl.pallas_call(
        body, name="small_reduce_adamw", out_shape=[jax.ShapeDtypeStruct((rows, c), F32)] * 4,
        in_specs=[vm] * 4, out_specs=[vm] * 4, compiler_params=_params(),
    )(gathered, w, m, v)


def _pack_small(parts, total):
    flat = jnp.concatenate([p.reshape(-1).astype(F32) for p in parts])
    flat = jnp.pad(flat, (0, total - flat.shape[0]))
    return flat.reshape(8, total // 8)


def _unpack_small(packed, shapes):
    flat = packed.reshape(-1)
    out, off = [], 0
    for shp in shapes:
        size = math.prod(shp)
        out.append(flat[off:off + size].reshape(shp))
        off += size
    return out


def kernel(x, c, rel_bias_table, w_ada, b_ada, pre_norm_g, post_norm_g, w_in, attn_sink, w_pool_group, pool_scale, w_branch_attn, w_branch_pool, w_merge, b_merge, w_out, loss_target, m_rel_bias_table, m_w_ada, m_b_ada, m_pre_norm_g, m_post_norm_g, m_w_in, m_attn_sink, m_w_pool_group, m_pool_scale, m_w_branch_attn, m_w_branch_pool, m_w_merge, m_b_merge, m_w_out, v_rel_bias_table, v_w_ada, v_b_ada, v_pre_norm_g, v_post_norm_g, v_w_in, v_attn_sink, v_w_pool_group, v_pool_scale, v_w_branch_attn, v_w_branch_pool, v_w_merge, v_b_merge, v_w_out):
    xi, yi, ci = lax.axis_index("x"), lax.axis_index("y"), lax.axis_index("c")
    me = 4 * xi + 2 * yi + ci
    x2, t2 = x[0], loss_target[0]
    s, d = x2.shape
    pw = d // 2
    n_groups = len(POOL_SIZES)
    cg = pw // n_groups
    in_w = w_in.shape[2] * N_DEV
    ada_cols = w_ada.shape[2]

    c_all = _allgather_small(jnp.broadcast_to(c, (8, d)), "gather_c").reshape(N_DEV, 8, d)[:, 0, :]
    c16 = jnp.concatenate([c_all, jnp.zeros_like(c_all)], axis=0)
    b_cols = lax.dynamic_slice_in_dim(b_ada, me * ada_cols, ada_cols, axis=1)
    mod_part = _ada_fwd(c16, w_ada[0], b_cols)[:8]
    mod_all = _allgather_small(mod_part, "gather_mod").reshape(N_DEV, 8, ada_cols)
    mod = lax.dynamic_index_in_dim(mod_all, me, axis=1, keepdims=False).reshape(3 * d)
    shift, scale, gate = mod[:d], mod[d:2 * d], mod[2 * d:]
    zrow = jnp.zeros((d,), F32)
    vec_in = jnp.stack([1.0 + scale, shift, pre_norm_g[0], zrow, zrow, zrow, zrow, zrow])
    vec_out = jnp.stack([gate, post_norm_g[0], zrow, zrow, zrow, zrow, zrow, zrow])

    shards = [w_in[0], w_merge[0], w_branch_attn[0], w_branch_pool[0], w_out[0],
              w_pool_group[0].reshape(n_groups * w_pool_group.shape[2], cg)]
    w_in_blk, w_merge_blk, w_ba_blk, w_bp_blk, w_out_blk, w_pool_blk = _allgather_hbm(
        [w.astype(BF16) for w in shards], "gather_weights")
    w_in_full = w_in_blk.transpose(1, 0, 2).reshape(d, in_w)
    rows_per = w_pool_group.shape[2]
    w_pool_full = w_pool_blk.reshape(N_DEV, n_groups, rows_per, cg).transpose(1, 0, 2, 3).reshape(n_groups, cg, cg)

    rel = jnp.arange(SPAN)[None, :] - WINDOW - jnp.arange(QBLK)[:, None]
    idx = jnp.where(jnp.abs(rel) <= WINDOW, _t5_bucket(rel), -1).astype(jnp.int32).reshape(1, QBLK * SPAN)
    bias = _bias_build(rel_bias_table.T, idx).reshape(N_Q_HEADS, QBLK, SPAN)
    sink = attn_sink[0]

    h, qkv, ga, u, gp = _fwd_in(x2, vec_in, w_in_full)
    g = _merge_fwd(h, w_merge_blk, b_merge)
    kvp = jnp.pad(qkv[:, ATTN_WIDTH:], ((WINDOW, WINDOW), (0, 0)))
    a, ya = _attn_fwd(qkv, kvp, ga, bias, sink)
    yp = _pool_fwd(u, gp, w_pool_full, pool_scale)
    ba, bp, merged = _branch_fwd(ya, yp, g, w_ba_blk, w_bp_blk)
    do, dgm, dba, dbp, dy, acc_out, db_merge = _out_fused(merged, x2, t2, g, ba, bp, vec_out, w_out_blk)
    loss = lax.psum(0.5 * jnp.sum(acc_out[0]) / d, MESH_AXES)

    gw_out = _wgrad(merged, do, d, d, "wgrad_out")[0].reshape(N_DEV, d // N_DEV, d)
    gw_ba = _wgrad(ya, dba, min(d, 1024), d // N_DEV, "wgrad_branch_attn")
    gw_bp = _wgrad(yp, dbp, min(d, 1024), d // N_DEV, "wgrad_branch_pool")
    gw_merge = _wgrad(h, dgm, min(2 * d, 1024), 2 * d // N_DEV, "wgrad_merge")
    da, dga, dyp = _branch_bwd(dba, dbp, a, ga, w_ba_blk, w_bp_blk)
    dq, dkv, dbias, dsink = _attn_bwd(qkv, kvp, a, da, bias, sink)
    e, dgp, gw_pool, dpool_scale = _pool_bwd_rows(u, gp, dyp, w_pool_full, pool_scale)
    du = _pool_bwd_window(e)
    dh1 = _dh_merge(dgm, w_merge_blk)
    pieces = [dq, dkv, dga, du, dgp]
    gx, acc_in = _dh_final(dh1, pieces, w_in_full, x2, dy, vec_in)
    gw_in = jnp.concatenate([_wgrad(h, p, p.shape[1], p.shape[1], "wgrad_in_%d" % i)[0]
                             for i, p in enumerate(pieces)], axis=1)
    gw_in_blk = gw_in.reshape(d, N_DEV, in_w // N_DEV).transpose(1, 0, 2)
    gw_pool_blk = gw_pool.astype(BF16).reshape(n_groups, N_DEV, rows_per, cg).transpose(1, 0, 2, 3).reshape(
        N_DEV, n_groups * rows_per, cg)
    d_table = _bias_reduce(dbias.reshape(N_Q_HEADS, QBLK * SPAN), idx).T

    recv = _alltoall_hbm([gw_in_blk, gw_merge, gw_ba, gw_bp, gw_out, gw_pool_blk], "exchange_grads")
    names = ["w_in", "w_merge", "w_branch_attn", "w_branch_pool", "w_out", "w_pool_group"]
    big_w = [w_in, w_merge, w_branch_attn, w_branch_pool, w_out, w_pool_group]
    big_m = [m_w_in, m_w_merge, m_w_branch_attn, m_w_branch_pool, m_w_out, m_w_pool_group]
    big_v = [v_w_in, v_w_merge, v_w_branch_attn, v_w_branch_pool, v_w_out, v_w_pool_group]
    big = {}
    for nm, rc, w_, m_, v_ in zip(names, recv, big_w, big_m, big_v):
        shp2 = rc.shape[1:]
        outs = _reduce_adamw(rc, w_.reshape(shp2), m_.reshape(shp2), v_.reshape(shp2), "adamw_" + nm)
        big[nm] = [o.reshape(w_.shape) for o in outs]

    dmod = jnp.concatenate([acc_in[0], acc_in[1], acc_out[1]])
    small_names = ["b_ada", "pre_norm_g", "post_norm_g", "pool_scale", "b_merge", "rel_bias_table", "attn_sink"]
    small_g = [dmod, acc_in[2], acc_out[2], dpool_scale, db_merge, d_table, dsink[:, 0]]
    small_w = [b_ada, pre_norm_g, post_norm_g, pool_scale, b_merge, rel_bias_table, attn_sink]
    small_m = [m_b_ada, m_pre_norm_g, m_post_norm_g, m_pool_scale, m_b_merge, m_rel_bias_table, m_attn_sink]
    small_v = [v_b_ada, v_pre_norm_g, v_post_norm_g, v_pool_scale, v_b_merge, v_rel_bias_table, v_attn_sink]
    n_small = sum(math.prod(w_.shape) for w_ in small_w)
    total = -(-n_small // 1024) * 1024
    gathered = _allgather_small(_pack_small(small_g, total), "gather_small_grads")
    sm = _small_reduce_adamw(gathered, _pack_small(small_w, total), _pack_small(small_m, total),
                             _pack_small(small_v, total))
    shapes = [w_.shape for w_ in small_w]
    small = {nm: vals for nm, vals in zip(small_names, zip(*[_unpack_small(o, shapes) for o in sm]))}

    dmod_all = gathered.reshape(N_DEV, -1)[:, :3 * d]
    dmod_mine = lax.dynamic_slice_in_dim(dmod_all, me * ada_cols, ada_cols, axis=1)
    dmod16 = jnp.concatenate([dmod_mine, jnp.zeros_like(dmod_mine)], axis=0)
    ada = [o[None] for o in _ada_bwd(c16, dmod16, w_ada[0], m_w_ada[0], v_w_ada[0])]

    order = ["rel_bias_table", "w_ada", "b_ada", "pre_norm_g", "post_norm_g", "w_in", "attn_sink", "w_pool_group",
             "pool_scale", "w_branch_attn", "w_branch_pool", "w_merge", "b_merge", "w_out"]
    results = dict(big)
    results.update({nm: list(vals) for nm, vals in small.items()})
    results["w_ada"] = ada
    outs = [loss, gx[None]]
    for k in range(4):
        outs += [results[nm][k] for nm in order]
    return tuple(outs)
```

```python
import functools
import math

import jax
import jax.numpy as jnp
from jax import lax
from jax.experimental import pallas as pl
from jax.experimental.pallas import tpu as pltpu

F32 = jnp.float32
BF16 = jnp.bfloat16

HEAD_DIM = 128
N_Q_HEADS = 8
N_KV_HEADS = 2
GQA_GROUP = N_Q_HEADS // N_KV_HEADS
ATTN_WIDTH = N_Q_HEADS * HEAD_DIM
KV_WIDTH = N_KV_HEADS * HEAD_DIM
WINDOW = 128
QBLK = 128
SPAN = QBLK + 2 * WINDOW
N_BUCKETS = 32
MAX_DISTANCE = 128
POOL_SIZES = (2, 4, 8, 16)
EPS = 1e-6
NEG_INF = -1e30
ATTN_SCALE = HEAD_DIM ** -0.5

ADAM_LR = 0.001
ADAM_B1 = 0.9
ADAM_B2 = 0.999
ADAM_EPS = 1e-08
ADAM_WD = 0.01
ADAM_STEP = 10

N_DEV = 8
MESH_AXES = ("x", "y", "c")
MESH = pl.DeviceIdType.MESH
POOL_HALO = 16
V7X_VMEM_LIMIT = 56 * 1024 * 1024

NT_DIMS = (((1,), (1,)), ((), ()))
TN_DIMS = (((0,), (0,)), ((), ()))


def _params(sem=None, vmem=V7X_VMEM_LIMIT):
    return pltpu.CompilerParams(dimension_semantics=sem, vmem_limit_bytes=vmem)


def _silu(v):
    return v * jax.nn.sigmoid(v)


def _dsilu(v):
    s = jax.nn.sigmoid(v)
    return s * (1.0 + v * (1.0 - s))


def _row_spec(tm, width):
    return pl.BlockSpec((tm, width), lambda i: (i, 0))


def _const_spec(shape):
    return pl.BlockSpec(shape, lambda *_: (0,) * len(shape))


ANY_SPEC = pl.BlockSpec(memory_space=pl.ANY)


def _load_blocked(w_blk, w_vmem, sems, axis):
    n = w_blk.shape[0]
    width = w_blk.shape[1 + axis]
    copies = []
    for d in range(n):
        dst = w_vmem.at[pl.ds(d * width, width), :] if axis == 0 else w_vmem.at[:, pl.ds(d * width, width)]
        copies.append(pltpu.make_async_copy(w_blk.at[d], dst, sems.at[d]))
    for cp in copies:
        cp.start()
    for cp in copies:
        cp.wait()


def _mesh_pos():
    return lax.axis_index("x"), lax.axis_index("y"), lax.axis_index("c")


def _allgather_small(blk, name):
    m_per, n = blk.shape

    def body(x_ref, out_ref, send_sems, recv_sems, local_sem):
        x, y, c = _mesh_pos()
        me, sibling = (x, y, c), (x, y, 1 - c)
        chips = [(1 - x, y), (x, 1 - y), (1 - x, 1 - y)]

        def rows(px, py, pc):
            return out_ref.at[pl.ds((4 * px + 2 * py + pc) * m_per, m_per), :]

        def copy(k, block, to, src=None):
            return pltpu.make_async_remote_copy(
                src_ref=rows(*block) if src is None else src, dst_ref=rows(*block),
                send_sem=send_sems.at[k], recv_sem=recv_sems.at[k], device_id=to, device_id_type=MESH)

        mine = pltpu.make_async_copy(x_ref, rows(*me), local_sem)
        mine.start()
        first = [copy(0, me, sibling, src=x_ref)]
        first += [copy(1 + j, me, (*chip, c), src=x_ref) for j, chip in enumerate(chips)]
        for cp in first:
            cp.start()
        passed = [copy(4 + j, (*chip, c), sibling) for j, chip in enumerate(chips)]
        for j, chip in enumerate(chips):
            copy(1 + j, (*chip, c), me).wait_recv()
            passed[j].start()
        copy(0, sibling, me).wait_recv()
        for j, chip in enumerate(chips):
            copy(4 + j, (*chip, 1 - c), me).wait_recv()
        for cp in first + passed:
            cp.wait_send()
        mine.wait()

    return pl.pallas_call(
        body, name=name,
        out_shape=jax.ShapeDtypeStruct((N_DEV * m_per, n), blk.dtype),
        in_specs=[pl.BlockSpec(memory_space=pltpu.VMEM)],
        out_specs=pl.BlockSpec(memory_space=pltpu.VMEM),
        scratch_shapes=[pltpu.SemaphoreType.DMA((7,)), pltpu.SemaphoreType.DMA((7,)), pltpu.SemaphoreType.DMA],
    )(blk)


def _gather_copy(outs, send_sems, recv_sems, i, k, block, to, src=None):
    slot = outs[i].at[4 * block[0] + 2 * block[1] + block[2]]
    return pltpu.make_async_remote_copy(
        src_ref=slot if src is None else src, dst_ref=slot,
        send_sem=send_sems.at[7 * i + k], recv_sem=recv_sems.at[7 * i + k], device_id=to, device_id_type=MESH)


def _gather_places():
    x, y, c = _mesh_pos()
    return (x, y, c), (x, y, 1 - c), [(1 - x, y), (x, 1 - y), (1 - x, 1 - y)]


def _gather_start(ins, outs, send_sems, recv_sems, local_sems):
    me, sibling, chips = _gather_places()
    for i, src in enumerate(ins):
        pltpu.make_async_copy(src, outs[i].at[4 * me[0] + 2 * me[1] + me[2]], local_sems.at[i]).start()
        _gather_copy(outs, send_sems, recv_sems, i, 0, me, sibling, src=src).start()
        for j, chip in enumerate(chips):
            _gather_copy(outs, send_sems, recv_sems, i, 1 + j, me, (*chip, me[2]), src=src).start()


def _gather_forward(ins, outs, send_sems, recv_sems):
    me, sibling, chips = _gather_places()
    for i in range(len(ins)):
        for j, chip in enumerate(chips):
            _gather_copy(outs, send_sems, recv_sems, i, 1 + j, (*chip, me[2]), me).wait_recv()
            _gather_copy(outs, send_sems, recv_sems, i, 4 + j, (*chip, me[2]), sibling).start()


def _gather_finish(ins, outs, send_sems, recv_sems, local_sems):
    me, sibling, chips = _gather_places()
    for i, src in enumerate(ins):
        _gather_copy(outs, send_sems, recv_sems, i, 0, sibling, me).wait_recv()
        for j, chip in enumerate(chips):
            _gather_copy(outs, send_sems, recv_sems, i, 4 + j, (*chip, 1 - me[2]), me).wait_recv()
        _gather_copy(outs, send_sems, recv_sems, i, 0, me, sibling, src=src).wait_send()
        for j, chip in enumerate(chips):
            _gather_copy(outs, send_sems, recv_sems, i, 1 + j, me, (*chip, me[2]), src=src).wait_send()
            _gather_copy(outs, send_sems, recv_sems, i, 4 + j, (*chip, me[2]), sibling).wait_send()
        pltpu.make_async_copy(src, outs[i].at[4 * me[0] + 2 * me[1] + me[2]], local_sems.at[i]).wait()


EXCHANGE_FLIPS = [(fx, fy, fc) for fx in (0, 1) for fy in (0, 1) for fc in (0, 1)][1:]


def _exchange_copy(ins, outs, send_sems, recv_sems, i, k, arriving):
    x, y, c = _mesh_pos()
    f = EXCHANGE_FLIPS[k]
    p = (1 - x if f[0] else x, 1 - y if f[1] else y, 1 - c if f[2] else c)
    p_idx = 4 * p[0] + 2 * p[1] + p[2]
    return pltpu.make_async_remote_copy(
        src_ref=ins[i].at[p_idx], dst_ref=outs[i].at[p_idx if arriving else 4 * x + 2 * y + c],
        send_sem=send_sems.at[7 * i + k], recv_sem=recv_sems.at[7 * i + k], device_id=p, device_id_type=MESH)


def _exchange_start(ins, outs, send_sems, recv_sems, local_sems):
    x, y, c = _mesh_pos()
    me_idx = 4 * x + 2 * y + c
    for i in range(len(ins)):
        pltpu.make_async_copy(ins[i].at[me_idx], outs[i].at[me_idx], local_sems.at[i]).start()
        for k in range(7):
            _exchange_copy(ins, outs, send_sems, recv_sems, i, k, False).start()


def _exchange_finish(ins, outs, send_sems, recv_sems, local_sems):
    x, y, c = _mesh_pos()
    me_idx = 4 * x + 2 * y + c
    for i in range(len(ins)):
        for k in range(7):
            _exchange_copy(ins, outs, send_sems, recv_sems, i, k, True).wait_recv()
            _exchange_copy(ins, outs, send_sems, recv_sems, i, k, False).wait_send()
        pltpu.make_async_copy(ins[i].at[me_idx], outs[i].at[me_idx], local_sems.at[i]).wait()


class _Carried:
    def __init__(self, kind, arrays):
        self.kind = kind
        self.arrays = list(arrays)
        self.n = len(self.arrays)

    def specs(self):
        return [ANY_SPEC] * self.n

    def out_shapes(self):
        lead = (N_DEV,) if self.kind == "gather" else ()
        return [jax.ShapeDtypeStruct(lead + a.shape, a.dtype) for a in self.arrays]

    def scratch(self):
        return [pltpu.SemaphoreType.DMA((7 * self.n,)), pltpu.SemaphoreType.DMA((7 * self.n,)),
                pltpu.SemaphoreType.DMA((self.n,))]

    def run(self, step, n_steps, ins, outs, sems):
        send_sems, recv_sems, local_sems = sems
        if self.n == 0:
            return

        @pl.when(step == 0)
        def _():
            if self.kind == "gather":
                _gather_start(ins, outs, send_sems, recv_sems, local_sems)
            else:
                _exchange_start(ins, outs, send_sems, recv_sems, local_sems)

        if self.kind == "gather":
            @pl.when(step == n_steps // 2)
            def _():
                _gather_forward(ins, outs, send_sems, recv_sems)

        @pl.when(step == n_steps - 1)
        def _():
            if self.kind == "gather":
                _gather_finish(ins, outs, send_sems, recv_sems, local_sems)
            else:
                _exchange_finish(ins, outs, send_sems, recv_sems, local_sems)


NO_COMM = _Carried("exchange", [])


def _split_refs(refs, n_in, n_out, n_scratch, comm):
    a = n_in
    b = a + comm.n
    c = b + n_out
    d = c + comm.n
    e = d + n_scratch
    return refs[:a], refs[b:c], refs[d:e], refs[a:b], refs[c:d], refs[e:]


def _allgather_hbm(shards, name):
    n_arr = len(shards)

    def body(*refs):
        ins, outs = refs[:n_arr], refs[n_arr:2 * n_arr]
        send_sems, recv_sems, local_sems = refs[2 * n_arr:]
        _gather_start(ins, outs, send_sems, recv_sems, local_sems)
        _gather_forward(ins, outs, send_sems, recv_sems)
        _gather_finish(ins, outs, send_sems, recv_sems, local_sems)

    return pl.pallas_call(
        body, name=name,
        out_shape=[jax.ShapeDtypeStruct((N_DEV,) + s.shape, s.dtype) for s in shards],
        in_specs=[ANY_SPEC] * n_arr,
        out_specs=[ANY_SPEC] * n_arr,
        scratch_shapes=[pltpu.SemaphoreType.DMA((7 * n_arr,)), pltpu.SemaphoreType.DMA((7 * n_arr,)),
                        pltpu.SemaphoreType.DMA((n_arr,))],
    )(*shards)


def _ada_fwd(c16, w_shard, b_cols):
    rows, d = c16.shape
    nc = w_shard.shape[1]

    def body(c_ref, w_ref, b_ref, o_ref):
        sc = _silu(c_ref[...]).astype(BF16)
        o_ref[...] = jnp.dot(sc, w_ref[...].astype(BF16), preferred_element_type=F32) + b_ref[...]

    return pl.pallas_call(
        body, name="ada_fwd", out_shape=jax.ShapeDtypeStruct((rows, nc), F32),
        in_specs=[pl.BlockSpec(memory_space=pltpu.VMEM)] * 3, out_specs=pl.BlockSpec(memory_space=pltpu.VMEM),
        compiler_params=_params(),
    )(c16, w_shard, b_cols)


def _adamw(w, g, m, v):
    m2 = ADAM_B1 * m + (1.0 - ADAM_B1) * g
    v2 = ADAM_B2 * v + (1.0 - ADAM_B2) * (g * g)
    m_hat = m2 / (1.0 - ADAM_B1 ** ADAM_STEP)
    v_hat = v2 / (1.0 - ADAM_B2 ** ADAM_STEP)
    delta = -ADAM_LR * (m_hat / (jnp.sqrt(v_hat) + ADAM_EPS) + ADAM_WD * w)
    return delta, m2, v2


def _ada_bwd(c16, dmod16, w, m, v):
    rows, d = c16.shape
    nc = w.shape[1]
    tr = min(512, d)

    def body(c_ref, dm_ref, w_ref, m_ref, v_ref, g_ref, dl_ref, m2_ref, v2_ref):
        sc = _silu(c_ref[...]).astype(BF16)
        g = lax.dot_general(sc, dm_ref[...].astype(BF16), TN_DIMS, preferred_element_type=F32)
        g_ref[...] = g
        dl_ref[...], m2_ref[...], v2_ref[...] = _adamw(w_ref[...], g, m_ref[...], v_ref[...])

    spec = _row_spec(tr, nc)
    return pl.pallas_call(
        body, name="ada_bwd", grid=(d // tr,),
        out_shape=[jax.ShapeDtypeStruct((d, nc), F32)] * 4,
        in_specs=[pl.BlockSpec((rows, tr), lambda i: (0, i)), _const_spec((rows, nc)), spec, spec, spec],
        out_specs=[spec] * 4, compiler_params=_params(("arbitrary",)),
    )(c16, dmod16, w, m, v)


def _split3(v):
    hi = v.astype(BF16)
    r1 = v - hi.astype(F32)
    mid = r1.astype(BF16)
    lo = (r1 - mid.astype(F32)).astype(BF16)
    return hi, mid, lo


BIAS_CHUNK = QBLK * SPAN // 8


def _bias_build(table_t, idx):
    n = idx.shape[1]

    def body(t_ref, i_ref, o_ref):
        ids = i_ref[...]
        onehot = (lax.broadcasted_iota(jnp.int32, (N_BUCKETS, BIAS_CHUNK), 0) == ids).astype(BF16)
        t16 = jnp.concatenate([t_ref[...], jnp.zeros_like(t_ref[...])], axis=0)
        acc = jnp.zeros((2 * N_Q_HEADS, BIAS_CHUNK), F32)
        for part in _split3(t16):
            acc = acc + jnp.dot(part, onehot, preferred_element_type=F32)
        o_ref[...] = jnp.where(ids >= 0, acc[:N_Q_HEADS], NEG_INF)

    return pl.pallas_call(
        body, name="bias_build", grid=(n // BIAS_CHUNK,),
        out_shape=jax.ShapeDtypeStruct((N_Q_HEADS, n), F32),
        in_specs=[_const_spec((N_Q_HEADS, N_BUCKETS)), pl.BlockSpec((1, BIAS_CHUNK), lambda i: (0, i))],
        out_specs=pl.BlockSpec((N_Q_HEADS, BIAS_CHUNK), lambda i: (0, i)),
        compiler_params=_params(("arbitrary",)),
    )(table_t, idx)


def _bias_reduce(dbias, idx):
    n = idx.shape[1]

    def body(d_ref, i_ref, o_ref):
        @pl.when(pl.program_id(0) == 0)
        def _():
            o_ref[...] = jnp.zeros_like(o_ref)

        onehot = (lax.broadcasted_iota(jnp.int32, (N_BUCKETS, BIAS_CHUNK), 0) == i_ref[...]).astype(BF16)
        d16 = jnp.concatenate([d_ref[...], jnp.zeros_like(d_ref[...])], axis=0)
        acc = jnp.zeros((2 * N_Q_HEADS, N_BUCKETS), F32)
        for part in _split3(d16):
            acc = acc + lax.dot_general(part, onehot, NT_DIMS, preferred_element_type=F32)
        o_ref[...] += acc[:N_Q_HEADS]

    return pl.pallas_call(
        body, name="bias_reduce", grid=(n // BIAS_CHUNK,),
        out_shape=jax.ShapeDtypeStruct((N_Q_HEADS, N_BUCKETS), F32),
        in_specs=[pl.BlockSpec((N_Q_HEADS, BIAS_CHUNK), lambda i: (0, i)), pl.BlockSpec((1, BIAS_CHUNK), lambda i: (0, i))],
        out_specs=_const_spec((N_Q_HEADS, N_BUCKETS)),
        compiler_params=_params(("arbitrary",)),
    )(dbias, idx)


def _t5_bucket(rel):
    half = N_BUCKETS // 2
    max_exact = half // 2
    ret = jnp.where(rel > 0, half, 0)
    n = jnp.abs(rel)
    nf = jnp.maximum(n, 1).astype(jnp.float32)
    large = max_exact + (jnp.log(nf / max_exact) / math.log(MAX_DISTANCE / max_exact)
                         * (half - max_exact)).astype(jnp.int32)
    large = jnp.minimum(large, half - 1)
    return ret + jnp.where(n < max_exact, n, large)


def _fwd_in(x2, vec, w_full, comm):
    s, d = x2.shape
    pw = d // 2
    tm = min(512, s)
    nt = s // tm
    widths = (ATTN_WIDTH + 2 * KV_WIDTH, ATTN_WIDTH, pw, pw)
    dtypes = (BF16, BF16, F32, BF16)

    def body(*refs):
        ins, outs, scratch, c_in, c_out, c_sems = _split_refs(refs, 3, 5, 2, comm)
        x_ref, vec_ref, w_hbm = ins
        h_ref, qkv_ref, ga_ref, u_ref, gp_ref = outs
        w_vmem, sem = scratch
        comm.run(pl.program_id(0), nt, c_in, c_out, c_sems)

        @pl.when(pl.program_id(0) == 0)
        def _():
            cp = pltpu.make_async_copy(w_hbm, w_vmem, sem)
            cp.start()
            cp.wait()

        xf = x_ref[...]
        r = lax.rsqrt(jnp.mean(xf * xf, axis=-1, keepdims=True) + EPS)
        h = ((xf * r) * vec_ref[2:3, :]) * vec_ref[0:1, :] + vec_ref[1:2, :]
        hb = h.astype(BF16)
        h_ref[...] = hb
        off = 0
        for o_ref, wd in zip((qkv_ref, ga_ref, u_ref, gp_ref), widths):
            for c0 in range(0, wd, 512):
                c1 = min(c0 + 512, wd)
                o_ref[:, c0:c1] = jnp.dot(hb, w_vmem[:, off + c0:off + c1],
                                          preferred_element_type=F32).astype(o_ref.dtype)
            off += wd

    res = pl.pallas_call(
        body, name="fwd_in", grid=(nt,),
        out_shape=[jax.ShapeDtypeStruct((s, d), BF16)] + [jax.ShapeDtypeStruct((s, wd), dt) for wd, dt in zip(widths, dtypes)]
                  + comm.out_shapes(),
        in_specs=[_row_spec(tm, d), _const_spec((8, d)), ANY_SPEC] + comm.specs(),
        out_specs=[_row_spec(tm, d)] + [_row_spec(tm, wd) for wd in widths] + comm.specs(),
        scratch_shapes=[pltpu.VMEM(w_full.shape, BF16), pltpu.SemaphoreType.DMA] + comm.scratch(),
        compiler_params=_params(("arbitrary",)),
    )(x2, vec, w_full, *comm.arrays)
    return res[:5], res[5:]


def _merge_fwd(h, w_blk, b_merge, comm):
    s, d = h.shape
    n = w_blk.shape[0] * w_blk.shape[2]
    tm = min(512, s)
    nt = s // tm

    def body(*refs):
        ins, outs, scratch, c_in, c_out, c_sems = _split_refs(refs, 3, 1, 2, comm)
        h_ref, w_hbm, b_ref = ins
        g_ref, = outs
        w_vmem, sems = scratch
        comm.run(pl.program_id(0), nt, c_in, c_out, c_sems)

        @pl.when(pl.program_id(0) == 0)
        def _():
            _load_blocked(w_hbm, w_vmem, sems, 1)

        hb = h_ref[...]
        for c0 in range(0, n, 512):
            z = jnp.dot(hb, w_vmem[:, c0:c0 + 512], preferred_element_type=F32) + b_ref[:, c0:c0 + 512]
            g_ref[:, c0:c0 + 512] = jax.nn.sigmoid(z).astype(BF16)

    res = pl.pallas_call(
        body, name="merge_fwd", grid=(nt,),
        out_shape=[jax.ShapeDtypeStruct((s, n), BF16)] + comm.out_shapes(),
        in_specs=[_row_spec(tm, d), ANY_SPEC, _const_spec((1, n))] + comm.specs(),
        out_specs=[_row_spec(tm, n)] + comm.specs(),
        scratch_shapes=[pltpu.VMEM((d, n), BF16), pltpu.SemaphoreType.DMA((N_DEV,))] + comm.scratch(),
        compiler_params=_params(("arbitrary",)),
    )(h, w_blk, b_merge, *comm.arrays)
    return res[0], res[1:]


def _attn_probs(n, s_len, q, kv, bias_ref, sink_ref, k):
    kpos = n * QBLK - WINDOW + lax.broadcasted_iota(jnp.int32, (1, SPAN), 1)
    key_ok = (kpos >= 0) & (kpos < s_len)
    heads = [GQA_GROUP * k + g for g in range(GQA_GROUP)]
    kk = kv[:, k * HEAD_DIM:(k + 1) * HEAD_DIM]
    vk = kv[:, KV_WIDTH + k * HEAD_DIM:KV_WIDTH + (k + 1) * HEAD_DIM]
    qk = jnp.concatenate([q[:, h * HEAD_DIM:(h + 1) * HEAD_DIM] for h in heads], axis=0)
    logits = lax.dot_general(qk, kk, NT_DIMS, preferred_element_type=F32) * ATTN_SCALE
    bias = jnp.concatenate([bias_ref[h] for h in heads], axis=0)
    logits = jnp.where(key_ok, logits + bias, NEG_INF)
    sink = jnp.concatenate([jnp.full((QBLK, 1), sink_ref[h], F32) for h in heads], axis=0)
    mx = jnp.maximum(jnp.max(logits, axis=-1, keepdims=True), sink)
    e = jnp.exp(logits - mx)
    e_sink = jnp.exp(sink - mx)
    inv = 1.0 / (jnp.sum(e, axis=-1, keepdims=True) + e_sink)
    return qk, kk, vk, e * inv, e_sink * inv


def _attn_fwd(qkv, kvp, ga, bias, sink):
    s = qkv.shape[0]
    nblk = s // QBLK
    kv_specs = [pl.BlockSpec((QBLK, 2 * KV_WIDTH), (lambda n, j=j: (n + j, 0))) for j in range(3)]

    def body(sink_ref, q_ref, k0, k1, k2, ga_ref, bias_ref, a_ref, ya_ref):
        n = pl.program_id(0)
        kv = jnp.concatenate([k0[...], k1[...], k2[...]], axis=0)
        q = q_ref[...]
        for k in range(N_KV_HEADS):
            _, _, vk, p, _ = _attn_probs(n, s, q, kv, bias_ref, sink_ref, k)
            o = jnp.dot(p.astype(BF16), vk, preferred_element_type=F32)
            for g in range(GQA_GROUP):
                c0 = (GQA_GROUP * k + g) * HEAD_DIM
                og = o[g * QBLK:(g + 1) * QBLK]
                a_ref[:, c0:c0 + HEAD_DIM] = og.astype(BF16)
                ya_ref[:, c0:c0 + HEAD_DIM] = (og * _silu(ga_ref[:, c0:c0 + HEAD_DIM].astype(F32))).astype(BF16)

    return pl.pallas_call(
        body, name="attn_fwd", grid=(nblk,),
        out_shape=[jax.ShapeDtypeStruct((s, ATTN_WIDTH), BF16)] * 2,
        in_specs=[pl.BlockSpec(memory_space=pltpu.SMEM), _row_spec(QBLK, ATTN_WIDTH)] + kv_specs
                 + [_row_spec(QBLK, ATTN_WIDTH), _const_spec((N_Q_HEADS, QBLK, SPAN))],
        out_specs=[_row_spec(QBLK, ATTN_WIDTH)] * 2,
        compiler_params=_params(("arbitrary",)),
    )(sink, qkv, kvp, kvp, kvp, ga, bias)


def _roll_rows(a, k):
    return pltpu.roll(a, k % a.shape[0], axis=0)


def _window_sum(xe, w, mirrored):
    acc = xe + _roll_rows(xe, -1 if mirrored else 1)
    width = 2
    while width < w:
        half = width // 2
        acc = _roll_rows(acc, half) + _roll_rows(acc, -half)
        width *= 2
    return acc


def _halo_specs(tm, width, s):
    nh = tm // POOL_HALO
    last = s // POOL_HALO - 1
    prev = pl.BlockSpec((POOL_HALO, width), lambda i: (jnp.maximum(i * nh - 1, 0), 0))
    nxt = pl.BlockSpec((POOL_HALO, width), lambda i: (jnp.minimum((i + 1) * nh, last), 0))
    return [prev, _row_spec(tm, width), nxt]


def _with_halo(prev_ref, cur_ref, next_ref, i, n_tiles):
    prev = jnp.where(i > 0, prev_ref[...].astype(F32), 0.0)
    nxt = jnp.where(i < n_tiles - 1, next_ref[...].astype(F32), 0.0)
    return jnp.concatenate([prev, cur_ref[...].astype(F32), nxt], axis=0)


def _pool_counts(i, tm, s, w):
    pos = i * tm + lax.broadcasted_iota(jnp.int32, (tm, 1), 0)
    lo = jnp.maximum(pos - w // 2, 0)
    hi = jnp.minimum(pos + w // 2, s)
    return (hi - lo).astype(F32)


def _pool_fwd(u, gp, wp, scale):
    s, pw = u.shape
    c = pw // len(POOL_SIZES)
    tm = min(512, s)
    nt = s // tm

    def body(up, uc, un, gp_ref, wp_ref, sc_ref, yp_ref):
        i = pl.program_id(0)
        ue = _with_halo(up, uc, un, i, nt)
        for g, w in enumerate(POOL_SIZES):
            cols = slice(g * c, (g + 1) * c)
            xg = ue[:, cols]
            ws = _window_sum(xg, w, False)[POOL_HALO:POOL_HALO + tm]
            pooled = ws / _pool_counts(i, tm, s, w) - xg[POOL_HALO:POOL_HALO + tm]
            mixed = jnp.dot(pooled.astype(BF16), wp_ref[g], preferred_element_type=F32)
            yp_ref[:, cols] = (mixed * sc_ref[:, cols] * _silu(gp_ref[:, cols].astype(F32))).astype(BF16)

    return pl.pallas_call(
        body, name="pool_fwd", grid=(nt,),
        out_shape=jax.ShapeDtypeStruct((s, pw), BF16),
        in_specs=_halo_specs(tm, pw, s) + [_row_spec(tm, pw), _const_spec(wp.shape), _const_spec((1, pw))],
        out_specs=_row_spec(tm, pw),
        compiler_params=_params(("arbitrary",)),
    )(u, u, u, gp, wp, scale)


def _branch_fwd(ya, yp, g, wa_blk, wp_blk):
    s = ya.shape[0]
    d = g.shape[1] // 2
    ka, kp = ya.shape[1], yp.shape[1]
    tm = min(512, s)

    def body(ya_ref, yp_ref, g_ref, wa_hbm, wp_hbm, ba_ref, bp_ref, m_ref, wa, wp, sems):
        @pl.when(pl.program_id(0) == 0)
        def _():
            _load_blocked(wa_hbm, wa, sems.at[pl.ds(0, N_DEV)], 1)
            _load_blocked(wp_hbm, wp, sems.at[pl.ds(N_DEV, N_DEV)], 1)

        for c0 in range(0, d, 512):
            cols = slice(c0, c0 + 512)
            ba = jnp.dot(ya_ref[...], wa[:, cols], preferred_element_type=F32)
            bp = jnp.dot(yp_ref[...], wp[:, cols], preferred_element_type=F32)
            ba_ref[:, cols] = ba.astype(BF16)
            bp_ref[:, cols] = bp.astype(BF16)
            ga = g_ref[:, c0:c0 + 512].astype(F32)
            gb = g_ref[:, d + c0:d + c0 + 512].astype(F32)
            m_ref[:, cols] = (ga * ba + gb * bp).astype(BF16)

    return pl.pallas_call(
        body, name="branch_fwd", grid=(s // tm,),
        out_shape=[jax.ShapeDtypeStruct((s, d), BF16)] * 3,
        in_specs=[_row_spec(tm, ka), _row_spec(tm, kp), _row_spec(tm, 2 * d), ANY_SPEC, ANY_SPEC],
        out_specs=[_row_spec(tm, d)] * 3,
        scratch_shapes=[pltpu.VMEM((ka, d), BF16), pltpu.VMEM((kp, d), BF16), pltpu.SemaphoreType.DMA((2 * N_DEV,))],
        compiler_params=_params(("arbitrary",)),
    )(ya, yp, g, wa_blk, wp_blk)


def _out_fused(merged, x2, t2, g, ba, bp, vec, wo_blk):
    s, d = x2.shape
    tm = min(256, s)

    def body(m_ref, x_ref, t_ref, g_ref, ba_ref, bp_ref, vec_ref, wo_hbm,
             do_ref, dgm_ref, dba_ref, dbp_ref, dy_ref, acc_ref, dbm_ref, wo, sems):
        @pl.when(pl.program_id(0) == 0)
        def _():
            _load_blocked(wo_hbm, wo, sems, 0)
            acc_ref[...] = jnp.zeros_like(acc_ref)
            dbm_ref[...] = jnp.zeros_like(dbm_ref)

        gate = vec_ref[0:1, :]
        post_g = vec_ref[1:2, :]
        o = jnp.dot(m_ref[...], wo[...], preferred_element_type=F32)
        r2 = lax.rsqrt(jnp.mean(o * o, axis=-1, keepdims=True) + EPS)
        on = o * r2
        z = on * post_g
        diff = x_ref[...] + gate * z - t_ref[...]
        dy = diff * (1.0 / d)
        dy_ref[...] = dy
        dz = dy * gate
        don = dz * post_g
        do = r2 * (don - on * jnp.mean(don * on, axis=-1, keepdims=True))
        acc_ref[0:1, :] += jnp.sum(diff * diff, axis=0, keepdims=True)
        acc_ref[1:2, :] += jnp.sum(dy * z, axis=0, keepdims=True)
        acc_ref[2:3, :] += jnp.sum(dz * on, axis=0, keepdims=True)
        dob = do.astype(BF16)
        do_ref[...] = dob
        for c0 in range(0, d, 512):
            cols = slice(c0, c0 + 512)
            dm = lax.dot_general(dob, wo[c0:c0 + 512, :], NT_DIMS, preferred_element_type=F32)
            ga = g_ref[:, c0:c0 + 512].astype(F32)
            gb = g_ref[:, d + c0:d + c0 + 512].astype(F32)
            b_a = ba_ref[:, cols].astype(F32)
            b_p = bp_ref[:, cols].astype(F32)
            dba_ref[:, cols] = (dm * ga).astype(BF16)
            dbp_ref[:, cols] = (dm * gb).astype(BF16)
            dga = dm * b_a * ga * (1.0 - ga)
            dgb = dm * b_p * gb * (1.0 - gb)
            dgm_ref[:, c0:c0 + 512] = dga.astype(BF16)
            dgm_ref[:, d + c0:d + c0 + 512] = dgb.astype(BF16)
            dbm_ref[:, c0:c0 + 512] += jnp.sum(dga, axis=0, keepdims=True)
            dbm_ref[:, d + c0:d + c0 + 512] += jnp.sum(dgb, axis=0, keepdims=True)

    return pl.pallas_call(
        body, name="out_fused", grid=(s // tm,),
        out_shape=[jax.ShapeDtypeStruct((s, d), BF16), jax.ShapeDtypeStruct((s, 2 * d), BF16),
                   jax.ShapeDtypeStruct((s, d), BF16), jax.ShapeDtypeStruct((s, d), BF16),
                   jax.ShapeDtypeStruct((s, d), F32), jax.ShapeDtypeStruct((8, d), F32),
                   jax.ShapeDtypeStruct((1, 2 * d), F32)],
        in_specs=[_row_spec(tm, d), _row_spec(tm, d), _row_spec(tm, d), _row_spec(tm, 2 * d),
                  _row_spec(tm, d), _row_spec(tm, d), _const_spec((8, d)), ANY_SPEC],
        out_specs=[_row_spec(tm, d), _row_spec(tm, 2 * d), _row_spec(tm, d), _row_spec(tm, d),
                   _row_spec(tm, d), _const_spec((8, d)), _const_spec((1, 2 * d))],
        scratch_shapes=[pltpu.VMEM((d, d), BF16), pltpu.SemaphoreType.DMA((N_DEV,))],
        compiler_params=_params(("arbitrary",)),
    )(merged, x2, t2, g, ba, bp, vec, wo_blk)


def _wgrad(a, b, tn, shard_w, name, comm=NO_COMM):
    s, k = a.shape
    n = b.shape[1]
    tk = min(512, s)
    ns = s // tk
    per = tn // shard_w

    def body(*refs):
        ins, outs, scratch, c_in, c_out, c_sems = _split_refs(refs, 2, 1, 1, comm)
        a_ref, b_ref = ins
        o_ref, = outs
        acc, = scratch
        step = pl.program_id(1)
        comm.run(pl.program_id(0) * ns + step, (n // tn) * ns, c_in, c_out, c_sems)

        @pl.when(step == 0)
        def _():
            acc[...] = jnp.zeros_like(acc)

        acc[...] += lax.dot_general(a_ref[...], b_ref[...], TN_DIMS, preferred_element_type=F32)

        @pl.when(step == ns - 1)
        def _():
            for r in range(per):
                o_ref[r] = acc[:, r * shard_w:(r + 1) * shard_w].astype(BF16)

    res = pl.pallas_call(
        body, name=name, grid=(n // tn, ns),
        out_shape=[jax.ShapeDtypeStruct((n // shard_w, k, shard_w), BF16)] + comm.out_shapes(),
        in_specs=[pl.BlockSpec((tk, k), lambda j, t: (t, 0)), pl.BlockSpec((tk, tn), lambda j, t: (t, j))] + comm.specs(),
        out_specs=[pl.BlockSpec((per, k, shard_w), lambda j, t: (j, 0, 0))] + comm.specs(),
        scratch_shapes=[pltpu.VMEM((k, tn), F32)] + comm.scratch(),
        compiler_params=_params(("arbitrary", "arbitrary")),
    )(a, b, *comm.arrays)
    return res[0] if comm.n == 0 else (res[0], res[1:])


def _branch_bwd(dba, dbp, a, ga, wa_blk, wp_blk):
    s, d = dba.shape
    ka = wa_blk.shape[1]
    kp = wp_blk.shape[1]
    tm = min(512, s)

    def body(dba_ref, dbp_ref, a_ref, ga_ref, wa_hbm, wp_hbm, da_ref, dga_ref, dyp_ref, wa, wp, sems):
        @pl.when(pl.program_id(0) == 0)
        def _():
            _load_blocked(wa_hbm, wa, sems.at[pl.ds(0, N_DEV)], 1)
            _load_blocked(wp_hbm, wp, sems.at[pl.ds(N_DEV, N_DEV)], 1)

        for c0 in range(0, ka, 512):
            cols = slice(c0, c0 + 512)
            dya = lax.dot_general(dba_ref[...], wa[c0:c0 + 512, :], NT_DIMS, preferred_element_type=F32)
            gaf = ga_ref[:, cols].astype(F32)
            da_ref[:, cols] = (dya * _silu(gaf)).astype(BF16)
            dga_ref[:, cols] = (dya * a_ref[:, cols].astype(F32) * _dsilu(gaf)).astype(BF16)
        for c0 in range(0, kp, 512):
            dyp = lax.dot_general(dbp_ref[...], wp[c0:c0 + 512, :], NT_DIMS, preferred_element_type=F32)
            dyp_ref[:, c0:c0 + 512] = dyp.astype(BF16)

    return pl.pallas_call(
        body, name="branch_bwd", grid=(s // tm,),
        out_shape=[jax.ShapeDtypeStruct((s, ka), BF16), jax.ShapeDtypeStruct((s, ka), BF16),
                   jax.ShapeDtypeStruct((s, kp), BF16)],
        in_specs=[_row_spec(tm, d), _row_spec(tm, d), _row_spec(tm, ka), _row_spec(tm, ka), ANY_SPEC, ANY_SPEC],
        out_specs=[_row_spec(tm, ka), _row_spec(tm, ka), _row_spec(tm, kp)],
        scratch_shapes=[pltpu.VMEM((ka, d), BF16), pltpu.VMEM((kp, d), BF16), pltpu.SemaphoreType.DMA((2 * N_DEV,))],
        compiler_params=_params(("arbitrary",)),
    )(dba, dbp, a, ga, wa_blk, wp_blk)


def _attn_bwd(qkv, kvp, a, da, bias, sink, comm):
    s = qkv.shape[0]
    nblk = s // QBLK
    clamp = lambda n: jnp.minimum(n, nblk - 1)
    q_spec = pl.BlockSpec((QBLK, ATTN_WIDTH), lambda n: (clamp(n), 0))
    kv_specs = [pl.BlockSpec((QBLK, 2 * KV_WIDTH), (lambda n, j=j: (clamp(n) + j, 0))) for j in range(3)]

    def body(*refs):
        ins, outs, scratch, c_in, c_out, c_sems = _split_refs(refs, 8, 4, 2, comm)
        sink_ref, q_ref, k0, k1, k2, a_ref, da_ref, bias_ref = ins
        dq_ref, dkv_ref, db_ref, ds_ref = outs
        acc, sacc = scratch
        n = pl.program_id(0)
        comm.run(n, nblk + 1, c_in, c_out, c_sems)

        @pl.when(n == 0)
        def _():
            acc[...] = jnp.zeros_like(acc)
            sacc[...] = jnp.zeros_like(sacc)
            db_ref[...] = jnp.zeros_like(db_ref)

        @pl.when(n < nblk)
        def _():
            kv = jnp.concatenate([k0[...], k1[...], k2[...]], axis=0)
            q = q_ref[...]
            dks, dvs = [], []
            for k in range(N_KV_HEADS):
                heads = [GQA_GROUP * k + g for g in range(GQA_GROUP)]
                qk, kk, vk, p, p_sink = _attn_probs(n, s, q, kv, bias_ref, sink_ref, k)
                o = jnp.concatenate([a_ref[:, h * HEAD_DIM:(h + 1) * HEAD_DIM] for h in heads], axis=0)
                do = jnp.concatenate([da_ref[:, h * HEAD_DIM:(h + 1) * HEAD_DIM] for h in heads], axis=0)
                dp = lax.dot_general(do, vk, NT_DIMS, preferred_element_type=F32)
                delta = jnp.sum(do.astype(F32) * o.astype(F32), axis=-1, keepdims=True)
                dlog = p * (dp - delta)
                sacc[k] += -(p_sink * delta)
                db_ref[k] += dlog
                dlb = (dlog * ATTN_SCALE).astype(BF16)
                dq = jnp.dot(dlb, kk, preferred_element_type=F32)
                for g, h in enumerate(heads):
                    dq_ref[:, h * HEAD_DIM:(h + 1) * HEAD_DIM] = dq[g * QBLK:(g + 1) * QBLK].astype(BF16)
                dks.append(lax.dot_general(dlb, qk, TN_DIMS, preferred_element_type=F32))
                dvs.append(lax.dot_general(p.astype(BF16), do, TN_DIMS, preferred_element_type=F32))
            contrib = jnp.concatenate(dks + dvs, axis=1)
            acc[n % 3] += contrib[0:QBLK]
            acc[(n + 1) % 3] += contrib[QBLK:2 * QBLK]
            acc[(n + 2) % 3] = contrib[2 * QBLK:3 * QBLK]

        dkv_ref[...] = acc[n % 3].astype(BF16)

        @pl.when(n == nblk)
        def _():
            for k in range(N_KV_HEADS):
                for g in range(GQA_GROUP):
                    h = GQA_GROUP * k + g
                    tot = jnp.sum(sacc[k, g * QBLK:(g + 1) * QBLK, :])
                    ds_ref[h:h + 1, :] = jnp.full((1, 128), tot, F32)

    res = pl.pallas_call(
        body, name="attn_bwd", grid=(nblk + 1,),
        out_shape=[jax.ShapeDtypeStruct((s, ATTN_WIDTH), BF16), jax.ShapeDtypeStruct((s, 2 * KV_WIDTH), BF16),
                   jax.ShapeDtypeStruct((N_KV_HEADS, GQA_GROUP * QBLK, SPAN), F32),
                   jax.ShapeDtypeStruct((N_Q_HEADS, 128), F32)] + comm.out_shapes(),
        in_specs=[pl.BlockSpec(memory_space=pltpu.SMEM), q_spec] + kv_specs
                 + [q_spec, q_spec, _const_spec((N_Q_HEADS, QBLK, SPAN))] + comm.specs(),
        out_specs=[q_spec, pl.BlockSpec((QBLK, 2 * KV_WIDTH), lambda n: (jnp.maximum(n - 1, 0), 0)),
                   _const_spec((N_KV_HEADS, GQA_GROUP * QBLK, SPAN)), _const_spec((N_Q_HEADS, 128))] + comm.specs(),
        scratch_shapes=[pltpu.VMEM((3, QBLK, 2 * KV_WIDTH), F32), pltpu.VMEM((N_KV_HEADS, GQA_GROUP * QBLK, 1), F32)]
                       + comm.scratch(),
        compiler_params=_params(("arbitrary",)),
    )(sink, qkv, kvp, kvp, kvp, a, da, bias, *comm.arrays)
    return res[:4], res[4:]


def _pool_bwd_rows(u, gp, dyp, wp, scale):
    s, pw = u.shape
    c = pw // len(POOL_SIZES)
    tm = min(512, s)
    nt = s // tm

    def body(up, uc, un, gp_ref, dyp_ref, wp_ref, sc_ref, e_ref, dgp_ref, gwp_ref, dsc_ref):
        i = pl.program_id(0)

        @pl.when(i == 0)
        def _():
            gwp_ref[...] = jnp.zeros_like(gwp_ref)
            dsc_ref[...] = jnp.zeros_like(dsc_ref)

        ue = _with_halo(up, uc, un, i, nt)
        for g, w in enumerate(POOL_SIZES):
            cols = slice(g * c, (g + 1) * c)
            xg = ue[:, cols]
            cnt = _pool_counts(i, tm, s, w)
            ws = _window_sum(xg, w, False)[POOL_HALO:POOL_HALO + tm]
            pooled = (ws / cnt - xg[POOL_HALO:POOL_HALO + tm]).astype(BF16)
            mixed = jnp.dot(pooled, wp_ref[g], preferred_element_type=F32)
            sc = sc_ref[:, cols]
            gpf = gp_ref[:, cols].astype(F32)
            dyp_g = dyp_ref[:, cols].astype(F32)
            dp = dyp_g * _silu(gpf)
            dgp_ref[:, cols] = (dyp_g * (mixed * sc) * _dsilu(gpf)).astype(BF16)
            dsc_ref[:, cols] += jnp.sum(dp * mixed, axis=0, keepdims=True)
            dmixed = (dp * sc).astype(BF16)
            dpooled = lax.dot_general(dmixed, wp_ref[g], NT_DIMS, preferred_element_type=F32)
            gwp_ref[g] += lax.dot_general(pooled, dmixed, TN_DIMS, preferred_element_type=F32)
            e_ref[:, cols] = dpooled / cnt

    return pl.pallas_call(
        body, name="pool_bwd_rows", grid=(nt,),
        out_shape=[jax.ShapeDtypeStruct((s, pw), F32), jax.ShapeDtypeStruct((s, pw), BF16),
                   jax.ShapeDtypeStruct(wp.shape, F32), jax.ShapeDtypeStruct((1, pw), F32)],
        in_specs=_halo_specs(tm, pw, s) + [_row_spec(tm, pw), _row_spec(tm, pw), _const_spec(wp.shape), _const_spec((1, pw))],
        out_specs=[_row_spec(tm, pw), _row_spec(tm, pw), _const_spec(wp.shape), _const_spec((1, pw))],
        compiler_params=_params(("arbitrary",)),
    )(u, u, u, gp, dyp, wp, scale)


def _pool_bwd_window(e):
    s, pw = e.shape
    c = pw // len(POOL_SIZES)
    tm = min(512, s)
    nt = s // tm

    def body(ep, ec, en, du_ref):
        i = pl.program_id(0)
        ee = _with_halo(ep, ec, en, i, nt)
        for g, w in enumerate(POOL_SIZES):
            cols = slice(g * c, (g + 1) * c)
            eg = ee[:, cols]
            ws = _window_sum(eg, w, True)[POOL_HALO:POOL_HALO + tm]
            du_ref[:, cols] = (ws - eg[POOL_HALO:POOL_HALO + tm] * _pool_counts(i, tm, s, w)).astype(BF16)

    return pl.pallas_call(
        body, name="pool_bwd_window", grid=(nt,),
        out_shape=jax.ShapeDtypeStruct((s, pw), BF16),
        in_specs=_halo_specs(tm, pw, s), out_specs=_row_spec(tm, pw),
        compiler_params=_params(("arbitrary",)),
    )(e, e, e)


def _dh_merge(dgm, w_blk, comm):
    s, n = dgm.shape
    d = w_blk.shape[1]
    tm = min(512, s)
    nt = s // tm

    def body(*refs):
        ins, outs, scratch, c_in, c_out, c_sems = _split_refs(refs, 2, 1, 2, comm)
        dg_ref, w_hbm = ins
        o_ref, = outs
        w_vmem, sems = scratch
        comm.run(pl.program_id(0), nt, c_in, c_out, c_sems)

        @pl.when(pl.program_id(0) == 0)
        def _():
            _load_blocked(w_hbm, w_vmem, sems, 1)

        for r0 in range(0, d, 512):
            o_ref[:, r0:r0 + 512] = lax.dot_general(dg_ref[...], w_vmem[r0:r0 + 512, :], NT_DIMS,
                                                    preferred_element_type=F32)

    res = pl.pallas_call(
        body, name="dh_merge", grid=(nt,),
        out_shape=[jax.ShapeDtypeStruct((s, d), F32)] + comm.out_shapes(),
        in_specs=[_row_spec(tm, n), ANY_SPEC] + comm.specs(), out_specs=[_row_spec(tm, d)] + comm.specs(),
        scratch_shapes=[pltpu.VMEM((d, n), BF16), pltpu.SemaphoreType.DMA((N_DEV,))] + comm.scratch(),
        compiler_params=_params(("arbitrary",)),
    )(dgm, w_blk, *comm.arrays)
    return res[0], res[1:]


def _dh_final(dh1, pieces, w_full, x2, dy, vec, comm):
    s, d = x2.shape
    tm = min(256, s)
    widths = [p.shape[1] for p in pieces]
    n_p = len(pieces)

    def body(*refs):
        ins, outs, scratch, c_in, c_out, c_sems = _split_refs(refs, 5 + n_p, 2, 2, comm)
        dh1_ref = ins[0]
        p_refs = ins[1:1 + n_p]
        w_hbm, x_ref, dy_ref, vec_ref = ins[1 + n_p:]
        gx_ref, acc_ref = outs
        w_vmem, sem = scratch
        comm.run(pl.program_id(0), s // tm, c_in, c_out, c_sems)

        @pl.when(pl.program_id(0) == 0)
        def _():
            cp = pltpu.make_async_copy(w_hbm, w_vmem, sem)
            cp.start()
            cp.wait()
            acc_ref[...] = jnp.zeros_like(acc_ref)

        scale1 = vec_ref[0:1, :]
        pre_g = vec_ref[2:3, :]
        xf = x_ref[...]
        r = lax.rsqrt(jnp.mean(xf * xf, axis=-1, keepdims=True) + EPS)
        xn = xf * r
        hn = xn * pre_g
        for r0 in range(0, d, 512):
            rows = slice(r0, r0 + 512)
            dh = dh1_ref[:, rows]
            off = 0
            for p_ref, wd in zip(p_refs, widths):
                dh = dh + lax.dot_general(p_ref[...], w_vmem[r0:r0 + 512, off:off + wd], NT_DIMS,
                                          preferred_element_type=F32)
                off += wd
            acc_ref[0:1, rows] += jnp.sum(dh, axis=0, keepdims=True)
            acc_ref[1:2, rows] += jnp.sum(dh * hn[:, rows], axis=0, keepdims=True)
            dhn = dh * scale1[:, rows]
            acc_ref[2:3, rows] += jnp.sum(dhn * xn[:, rows], axis=0, keepdims=True)
            gx_ref[:, rows] = dhn * pre_g[:, rows]
        dxn = gx_ref[...]
        dx = r * (dxn - xn * jnp.mean(dxn * xn, axis=-1, keepdims=True))
        gx_ref[...] = dy_ref[...] + dx

    res = pl.pallas_call(
        body, name="dh_final", grid=(s // tm,),
        out_shape=[jax.ShapeDtypeStruct((s, d), F32), jax.ShapeDtypeStruct((8, d), F32)] + comm.out_shapes(),
        in_specs=[_row_spec(tm, d)] + [_row_spec(tm, wd) for wd in widths]
                 + [ANY_SPEC, _row_spec(tm, d), _row_spec(tm, d), _const_spec((8, d))] + comm.specs(),
        out_specs=[_row_spec(tm, d), _const_spec((8, d))] + comm.specs(),
        scratch_shapes=[pltpu.VMEM(w_full.shape, BF16), pltpu.SemaphoreType.DMA] + comm.scratch(),
        compiler_params=_params(("arbitrary",)),
    )(dh1, *pieces, w_full, x2, dy, vec, *comm.arrays)
    return res[:2], res[2:]


def _reduce_adamw(parts, w, m, v, name):
    r, c = w.shape
    n_parts = len(parts)
    rp = r // n_parts
    tr = min(256, rp)
    per = rp // tr

    def body(*refs):
        p_refs = refs[:n_parts]
        w_ref, m_ref, v_ref, g_ref, dl_ref, m2_ref, v2_ref = refs[n_parts:]
        i = pl.program_id(0)
        for k, p_ref in enumerate(p_refs):
            @pl.when((i >= k * per) & (i < (k + 1) * per))
            def _(p_ref=p_ref):
                g = p_ref[0].astype(F32)
                for dev in range(1, N_DEV):
                    g = g + p_ref[dev].astype(F32)
                g_ref[...] = g
                dl_ref[...], m2_ref[...], v2_ref[...] = _adamw(w_ref[...], g, m_ref[...], v_ref[...])

    spec = _row_spec(tr, c)
    part_specs = [pl.BlockSpec((N_DEV, tr, c), (lambda i, k=k: (0, jnp.clip(i - k * per, 0, per - 1), 0)))
                  for k in range(n_parts)]
    return pl.pallas_call(
        body, name=name, grid=(r // tr,),
        out_shape=[jax.ShapeDtypeStruct((r, c), F32)] * 4,
        in_specs=part_specs + [spec, spec, spec],
        out_specs=[spec] * 4, compiler_params=_params(("arbitrary",)),
    )(*parts, w, m, v)


def _small_reduce_adamw(gathered, w, m, v):
    rows, c = w.shape

    def body(p_ref, w_ref, m_ref, v_ref, g_ref, dl_ref, m2_ref, v2_ref):
        g = p_ref[0:rows, :]
        for dev in range(1, N_DEV):
            g = g + p_ref[dev * rows:(dev + 1) * rows, :]
        g_ref[...] = g
        dl_ref[...], m2_ref[...], v2_ref[...] = _adamw(w_ref[...], g, m_ref[...], v_ref[...])

    vm = pl.BlockSpec(memory_space=pltpu.VMEM)
    return pl.pallas_call(
        body, name="small_reduce_adamw", out_shape=[jax.ShapeDtypeStruct((rows, c), F32)] * 4,
        in_specs=[vm] * 4, out_specs=[vm] * 4, compiler_params=_params(),
    )(gathered, w, m, v)


def _pack_small(parts, total):
    flat = jnp.concatenate([p.reshape(-1).astype(F32) for p in parts])
    flat = jnp.pad(flat, (0, total - flat.shape[0]))
    return flat.reshape(8, total // 8)


def _unpack_small(packed, shapes):
    flat = packed.reshape(-1)
    out, off = [], 0
    for shp in shapes:
        size = math.prod(shp)
        out.append(flat[off:off + size].reshape(shp))
        off += size
    return out


def kernel(x, c, rel_bias_table, w_ada, b_ada, pre_norm_g, post_norm_g, w_in, attn_sink, w_pool_group, pool_scale, w_branch_attn, w_branch_pool, w_merge, b_merge, w_out, loss_target, m_rel_bias_table, m_w_ada, m_b_ada, m_pre_norm_g, m_post_norm_g, m_w_in, m_attn_sink, m_w_pool_group, m_pool_scale, m_w_branch_attn, m_w_branch_pool, m_w_merge, m_b_merge, m_w_out, v_rel_bias_table, v_w_ada, v_b_ada, v_pre_norm_g, v_post_norm_g, v_w_in, v_attn_sink, v_w_pool_group, v_pool_scale, v_w_branch_attn, v_w_branch_pool, v_w_merge, v_b_merge, v_w_out):
    xi, yi, ci = lax.axis_index("x"), lax.axis_index("y"), lax.axis_index("c")
    me = 4 * xi + 2 * yi + ci
    x2, t2 = x[0], loss_target[0]
    s, d = x2.shape
    pw = d // 2
    n_groups = len(POOL_SIZES)
    cg = pw // n_groups
    in_w = w_in.shape[2] * N_DEV
    ada_cols = w_ada.shape[2]

    c_all = _allgather_small(jnp.broadcast_to(c, (8, d)), "gather_c").reshape(N_DEV, 8, d)[:, 0, :]
    c16 = jnp.concatenate([c_all, jnp.zeros_like(c_all)], axis=0)
    b_cols = lax.dynamic_slice_in_dim(b_ada, me * ada_cols, ada_cols, axis=1)
    mod_part = _ada_fwd(c16, w_ada[0], b_cols)[:8]
    mod_all = _allgather_small(mod_part, "gather_mod").reshape(N_DEV, 8, ada_cols)
    mod = lax.dynamic_index_in_dim(mod_all, me, axis=1, keepdims=False).reshape(3 * d)
    shift, scale, gate = mod[:d], mod[d:2 * d], mod[2 * d:]
    zrow = jnp.zeros((d,), F32)
    vec_in = jnp.stack([1.0 + scale, shift, pre_norm_g[0], zrow, zrow, zrow, zrow, zrow])
    vec_out = jnp.stack([gate, post_norm_g[0], zrow, zrow, zrow, zrow, zrow, zrow])

    rows_per = w_pool_group.shape[2]
    late_shards = [w.astype(BF16) for w in (w_branch_attn[0], w_branch_pool[0], w_out[0],
                                            w_pool_group[0].reshape(n_groups * rows_per, cg))]
    w_in_blk, = _allgather_hbm([w_in[0].astype(BF16)], "gather_w_in")
    w_in_full = w_in_blk.transpose(1, 0, 2).reshape(d, in_w)

    rel = jnp.arange(SPAN)[None, :] - WINDOW - jnp.arange(QBLK)[:, None]
    idx = jnp.where(jnp.abs(rel) <= WINDOW, _t5_bucket(rel), -1).astype(jnp.int32).reshape(1, QBLK * SPAN)
    bias = _bias_build(rel_bias_table.T, idx).reshape(N_Q_HEADS, QBLK, SPAN)
    sink = attn_sink[0]

    (h, qkv, ga, u, gp), (w_merge_blk,) = _fwd_in(x2, vec_in, w_in_full, _Carried("gather", [w_merge[0].astype(BF16)]))
    g, (w_ba_blk, w_bp_blk, w_out_blk, w_pool_blk) = _merge_fwd(h, w_merge_blk, b_merge, _Carried("gather", late_shards))
    w_pool_full = w_pool_blk.reshape(N_DEV, n_groups, rows_per, cg).transpose(1, 0, 2, 3).reshape(n_groups, cg, cg)
    kvp = jnp.pad(qkv[:, ATTN_WIDTH:], ((WINDOW, WINDOW), (0, 0)))
    a, ya = _attn_fwd(qkv, kvp, ga, bias, sink)
    yp = _pool_fwd(u, gp, w_pool_full, pool_scale)
    ba, bp, merged = _branch_fwd(ya, yp, g, w_ba_blk, w_bp_blk)
    do, dgm, dba, dbp, dy, acc_out, db_merge = _out_fused(merged, x2, t2, g, ba, bp, vec_out, w_out_blk)
    loss = lax.psum(0.5 * jnp.sum(acc_out[0]) / d, MESH_AXES)

    gw_out = _wgrad(merged, do, d, d, "wgrad_out")[0].reshape(N_DEV, d // N_DEV, d)
    gw_ba = _wgrad(ya, dba, min(d, 1024), d // N_DEV, "wgrad_branch_attn")
    gw_bp = _wgrad(yp, dbp, min(d, 1024), d // N_DEV, "wgrad_branch_pool")
    gw_merge, (r_out, r_ba, r_bp) = _wgrad(h, dgm, min(2 * d, 1024), 2 * d // N_DEV, "wgrad_merge",
                                           _Carried("exchange", [gw_out, gw_ba, gw_bp]))
    da, dga, dyp = _branch_bwd(dba, dbp, a, ga, w_ba_blk, w_bp_blk)
    (dq, dkv, dbias, dsink), (r_merge,) = _attn_bwd(qkv, kvp, a, da, bias, sink, _Carried("exchange", [gw_merge]))
    e, dgp, gw_pool, dpool_scale = _pool_bwd_rows(u, gp, dyp, w_pool_full, pool_scale)
    du = _pool_bwd_window(e)
    pieces = [dq, dkv, dga, du, dgp]
    gw_in = jnp.concatenate([_wgrad(h, p, p.shape[1], p.shape[1], "wgrad_in_%d" % i)[0]
                             for i, p in enumerate(pieces)], axis=1)
    gw_in_top, gw_in_bot = [half.reshape(d // 2, N_DEV, in_w // N_DEV).transpose(1, 0, 2)
                            for half in (gw_in[:d // 2], gw_in[d // 2:])]
    gw_pool_blk = gw_pool.astype(BF16).reshape(n_groups, N_DEV, rows_per, cg).transpose(1, 0, 2, 3).reshape(
        N_DEV, n_groups * rows_per, cg)
    dh1, (r_in_top,) = _dh_merge(dgm, w_merge_blk, _Carried("exchange", [gw_in_top]))
    (gx, acc_in), (r_in_bot, r_pool) = _dh_final(dh1, pieces, w_in_full, x2, dy, vec_in,
                                                 _Carried("exchange", [gw_in_bot, gw_pool_blk]))
    d_table = _bias_reduce(dbias.reshape(N_Q_HEADS, QBLK * SPAN), idx).T

    recv = [[r_in_top, r_in_bot], [r_merge], [r_ba], [r_bp], [r_out], [r_pool]]
    names = ["w_in", "w_merge", "w_branch_attn", "w_branch_pool", "w_out", "w_pool_group"]
    big_w = [w_in, w_merge, w_branch_attn, w_branch_pool, w_out, w_pool_group]
    big_m = [m_w_in, m_w_merge, m_w_branch_attn, m_w_branch_pool, m_w_out, m_w_pool_group]
    big_v = [v_w_in, v_w_merge, v_w_branch_attn, v_w_branch_pool, v_w_out, v_w_pool_group]
    big = {}
    for nm, rc, w_, m_, v_ in zip(names, recv, big_w, big_m, big_v):
        shp2 = (len(rc) * rc[0].shape[1], rc[0].shape[2])
        outs = _reduce_adamw(rc, w_.reshape(shp2), m_.reshape(shp2), v_.reshape(shp2), "adamw_" + nm)
        big[nm] = [o.reshape(w_.shape) for o in outs]

    dmod = jnp.concatenate([acc_in[0], acc_in[1], acc_out[1]])
    small_names = ["b_ada", "pre_norm_g", "post_norm_g", "pool_scale", "b_merge", "rel_bias_table", "attn_sink"]
    small_g = [dmod, acc_in[2], acc_out[2], dpool_scale, db_merge, d_table, dsink[:, 0]]
    small_w = [b_ada, pre_norm_g, post_norm_g, pool_scale, b_merge, rel_bias_table, attn_sink]
    small_m = [m_b_ada, m_pre_norm_g, m_post_norm_g, m_pool_scale, m_b_merge, m_rel_bias_table, m_attn_sink]
    small_v = [v_b_ada, v_pre_norm_g, v_post_norm_g, v_pool_scale, v_b_merge, v_rel_bias_table, v_attn_sink]
    n_small = sum(math.prod(w_.shape) for w_ in small_w)
    total = -(-n_small // 1024) * 1024
    gathered = _allgather_small(_pack_small(small_g, total), "gather_small_grads")
    sm = _small_reduce_adamw(gathered, _pack_small(small_w, total), _pack_small(small_m, total),
                             _pack_small(small_v, total))
    shapes = [w_.shape for w_ in small_w]
    small = {nm: vals for nm, vals in zip(small_names, zip(*[_unpack_small(o, shapes) for o in sm]))}

    dmod_all = gathered.reshape(N_DEV, -1)[:, :3 * d]
    dmod_mine = lax.dynamic_slice_in_dim(dmod_all, me * ada_cols, ada_cols, axis=1)
    dmod16 = jnp.concatenate([dmod_mine, jnp.zeros_like(dmod_mine)], axis=0)
    ada = [o[None] for o in _ada_bwd(c16, dmod16, w_ada[0], m_w_ada[0], v_w_ada[0])]

    order = ["rel_bias_table", "w_ada", "b_ada", "pre_norm_g", "post_norm_g", "w_in", "attn_sink", "w_pool_group",
             "pool_scale", "w_branch_attn", "w_branch_pool", "w_merge", "b_merge", "w_out"]
    results = dict(big)
    results.update({nm: list(vals) for nm, vals in small.items()})
    results["w_ada"] = ada
    outs = [loss, gx[None]]
    for k in range(4):
        outs += [results[nm][k] for nm in order]
    return tuple(outs)
```

```python
import functools
import math

import jax
import jax.numpy as jnp
from jax import lax
from jax.experimental import pallas as pl
from jax.experimental.pallas import tpu as pltpu

F32 = jnp.float32
BF16 = jnp.bfloat16

HEAD_DIM = 128
N_Q_HEADS = 8
N_KV_HEADS = 2
GQA_GROUP = N_Q_HEADS // N_KV_HEADS
ATTN_WIDTH = N_Q_HEADS * HEAD_DIM
KV_WIDTH = N_KV_HEADS * HEAD_DIM
WINDOW = 128
QBLK = 128
SPAN = QBLK + 2 * WINDOW
N_BUCKETS = 32
MAX_DISTANCE = 128
POOL_SIZES = (2, 4, 8, 16)
EPS = 1e-6
NEG_INF = -1e30
ATTN_SCALE = HEAD_DIM ** -0.5

ADAM_LR = 0.001
ADAM_B1 = 0.9
ADAM_B2 = 0.999
ADAM_EPS = 1e-08
ADAM_WD = 0.01
ADAM_STEP = 10

N_DEV = 8
MESH_AXES = ("x", "y", "c")
MESH = pl.DeviceIdType.MESH
POOL_HALO = 16
V7X_VMEM_LIMIT = 56 * 1024 * 1024

NT_DIMS = (((1,), (1,)), ((), ()))
TN_DIMS = (((0,), (0,)), ((), ()))


def _params(sem=None, vmem=V7X_VMEM_LIMIT):
    return pltpu.CompilerParams(dimension_semantics=sem, vmem_limit_bytes=vmem)


def _silu(v):
    return v * jax.nn.sigmoid(v)


def _dsilu(v):
    s = jax.nn.sigmoid(v)
    return s * (1.0 + v * (1.0 - s))


def _row_spec(tm, width):
    return pl.BlockSpec((tm, width), lambda i: (i, 0))


def _const_spec(shape):
    return pl.BlockSpec(shape, lambda *_: (0,) * len(shape))


ANY_SPEC = pl.BlockSpec(memory_space=pl.ANY)


def _load_blocked(w_blk, w_vmem, sems, axis):
    n = w_blk.shape[0]
    width = w_blk.shape[1 + axis]
    copies = []
    for d in range(n):
        dst = w_vmem.at[pl.ds(d * width, width), :] if axis == 0 else w_vmem.at[:, pl.ds(d * width, width)]
        copies.append(pltpu.make_async_copy(w_blk.at[d], dst, sems.at[d]))
    for cp in copies:
        cp.start()
    for cp in copies:
        cp.wait()


def _mesh_pos():
    return lax.axis_index("x"), lax.axis_index("y"), lax.axis_index("c")


def _allgather_small(blk, name):
    m_per, n = blk.shape

    def body(x_ref, out_ref, send_sems, recv_sems, local_sem):
        x, y, c = _mesh_pos()
        me, sibling = (x, y, c), (x, y, 1 - c)
        chips = [(1 - x, y), (x, 1 - y), (1 - x, 1 - y)]

        def rows(px, py, pc):
            return out_ref.at[pl.ds((4 * px + 2 * py + pc) * m_per, m_per), :]

        def copy(k, block, to, src=None):
            return pltpu.make_async_remote_copy(
                src_ref=rows(*block) if src is None else src, dst_ref=rows(*block),
                send_sem=send_sems.at[k], recv_sem=recv_sems.at[k], device_id=to, device_id_type=MESH)

        mine = pltpu.make_async_copy(x_ref, rows(*me), local_sem)
        mine.start()
        first = [copy(0, me, sibling, src=x_ref)]
        first += [copy(1 + j, me, (*chip, c), src=x_ref) for j, chip in enumerate(chips)]
        for cp in first:
            cp.start()
        passed = [copy(4 + j, (*chip, c), sibling) for j, chip in enumerate(chips)]
        for j, chip in enumerate(chips):
            copy(1 + j, (*chip, c), me).wait_recv()
            passed[j].start()
        copy(0, sibling, me).wait_recv()
        for j, chip in enumerate(chips):
            copy(4 + j, (*chip, 1 - c), me).wait_recv()
        for cp in first + passed:
            cp.wait_send()
        mine.wait()

    return pl.pallas_call(
        body, name=name,
        out_shape=jax.ShapeDtypeStruct((N_DEV * m_per, n), blk.dtype),
        in_specs=[pl.BlockSpec(memory_space=pltpu.VMEM)],
        out_specs=pl.BlockSpec(memory_space=pltpu.VMEM),
        scratch_shapes=[pltpu.SemaphoreType.DMA((7,)), pltpu.SemaphoreType.DMA((7,)), pltpu.SemaphoreType.DMA],
    )(blk)


def _gather_copy(outs, send_sems, recv_sems, i, k, block, to, src=None):
    slot = outs[i].at[4 * block[0] + 2 * block[1] + block[2]]
    return pltpu.make_async_remote_copy(
        src_ref=slot if src is None else src, dst_ref=slot,
        send_sem=send_sems.at[7 * i + k], recv_sem=recv_sems.at[7 * i + k], device_id=to, device_id_type=MESH)


def _gather_places():
    x, y, c = _mesh_pos()
    return (x, y, c), (x, y, 1 - c), [(1 - x, y), (x, 1 - y), (1 - x, 1 - y)]


def _gather_start(ins, outs, send_sems, recv_sems, local_sems):
    me, sibling, chips = _gather_places()
    for i, src in enumerate(ins):
        pltpu.make_async_copy(src, outs[i].at[4 * me[0] + 2 * me[1] + me[2]], local_sems.at[i]).start()
        _gather_copy(outs, send_sems, recv_sems, i, 0, me, sibling, src=src).start()
        for j, chip in enumerate(chips):
            _gather_copy(outs, send_sems, recv_sems, i, 1 + j, me, (*chip, me[2]), src=src).start()


def _gather_forward(ins, outs, send_sems, recv_sems):
    me, sibling, chips = _gather_places()
    for i in range(len(ins)):
        for j, chip in enumerate(chips):
            _gather_copy(outs, send_sems, recv_sems, i, 1 + j, (*chip, me[2]), me).wait_recv()
            _gather_copy(outs, send_sems, recv_sems, i, 4 + j, (*chip, me[2]), sibling).start()


def _gather_finish(ins, outs, send_sems, recv_sems, local_sems):
    me, sibling, chips = _gather_places()
    for i, src in enumerate(ins):
        _gather_copy(outs, send_sems, recv_sems, i, 0, sibling, me).wait_recv()
        for j, chip in enumerate(chips):
            _gather_copy(outs, send_sems, recv_sems, i, 4 + j, (*chip, 1 - me[2]), me).wait_recv()
        _gather_copy(outs, send_sems, recv_sems, i, 0, me, sibling, src=src).wait_send()
        for j, chip in enumerate(chips):
            _gather_copy(outs, send_sems, recv_sems, i, 1 + j, me, (*chip, me[2]), src=src).wait_send()
            _gather_copy(outs, send_sems, recv_sems, i, 4 + j, (*chip, me[2]), sibling).wait_send()
        pltpu.make_async_copy(src, outs[i].at[4 * me[0] + 2 * me[1] + me[2]], local_sems.at[i]).wait()


EXCHANGE_FLIPS = [(fx, fy, fc) for fx in (0, 1) for fy in (0, 1) for fc in (0, 1)][1:]


def _exchange_copy(ins, outs, send_sems, recv_sems, i, k, arriving):
    x, y, c = _mesh_pos()
    f = EXCHANGE_FLIPS[k]
    p = (1 - x if f[0] else x, 1 - y if f[1] else y, 1 - c if f[2] else c)
    p_idx = 4 * p[0] + 2 * p[1] + p[2]
    return pltpu.make_async_remote_copy(
        src_ref=ins[i].at[p_idx], dst_ref=outs[i].at[p_idx if arriving else 4 * x + 2 * y + c],
        send_sem=send_sems.at[7 * i + k], recv_sem=recv_sems.at[7 * i + k], device_id=p, device_id_type=MESH)


def _exchange_start(ins, outs, send_sems, recv_sems, local_sems):
    x, y, c = _mesh_pos()
    me_idx = 4 * x + 2 * y + c
    for i in range(len(ins)):
        pltpu.make_async_copy(ins[i].at[me_idx], outs[i].at[me_idx], local_sems.at[i]).start()
        for k in range(7):
            _exchange_copy(ins, outs, send_sems, recv_sems, i, k, False).start()


def _exchange_finish(ins, outs, send_sems, recv_sems, local_sems):
    x, y, c = _mesh_pos()
    me_idx = 4 * x + 2 * y + c
    for i in range(len(ins)):
        for k in range(7):
            _exchange_copy(ins, outs, send_sems, recv_sems, i, k, True).wait_recv()
            _exchange_copy(ins, outs, send_sems, recv_sems, i, k, False).wait_send()
        pltpu.make_async_copy(ins[i].at[me_idx], outs[i].at[me_idx], local_sems.at[i]).wait()


class _Carried:
    def __init__(self, kind, arrays):
        self.kind = kind
        self.arrays = list(arrays)
        self.n = len(self.arrays)

    def specs(self):
        return [ANY_SPEC] * self.n

    def out_shapes(self):
        lead = (N_DEV,) if self.kind == "gather" else ()
        return [jax.ShapeDtypeStruct(lead + a.shape, a.dtype) for a in self.arrays]

    def scratch(self):
        return [pltpu.SemaphoreType.DMA((7 * self.n,)), pltpu.SemaphoreType.DMA((7 * self.n,)),
                pltpu.SemaphoreType.DMA((self.n,))]

    def run(self, step, n_steps, ins, outs, sems):
        send_sems, recv_sems, local_sems = sems
        if self.n == 0:
            return

        @pl.when(step == 0)
        def _():
            if self.kind == "gather":
                _gather_start(ins, outs, send_sems, recv_sems, local_sems)
            else:
                _exchange_start(ins, outs, send_sems, recv_sems, local_sems)

        if self.kind == "gather":
            @pl.when(step == n_steps // 2)
            def _():
                _gather_forward(ins, outs, send_sems, recv_sems)

        @pl.when(step == n_steps - 1)
        def _():
            if self.kind == "gather":
                _gather_finish(ins, outs, send_sems, recv_sems, local_sems)
            else:
                _exchange_finish(ins, outs, send_sems, recv_sems, local_sems)


NO_COMM = _Carried("exchange", [])


def _split_refs(refs, n_in, n_out, n_scratch, comm):
    a = n_in
    b = a + comm.n
    c = b + n_out
    d = c + comm.n
    e = d + n_scratch
    return refs[:a], refs[b:c], refs[d:e], refs[a:b], refs[c:d], refs[e:]


def _allgather_hbm(shards, name):
    n_arr = len(shards)

    def body(*refs):
        ins, outs = refs[:n_arr], refs[n_arr:2 * n_arr]
        send_sems, recv_sems, local_sems = refs[2 * n_arr:]
        _gather_start(ins, outs, send_sems, recv_sems, local_sems)
        _gather_forward(ins, outs, send_sems, recv_sems)
        _gather_finish(ins, outs, send_sems, recv_sems, local_sems)

    return pl.pallas_call(
        body, name=name,
        out_shape=[jax.ShapeDtypeStruct((N_DEV,) + s.shape, s.dtype) for s in shards],
        in_specs=[ANY_SPEC] * n_arr,
        out_specs=[ANY_SPEC] * n_arr,
        scratch_shapes=[pltpu.SemaphoreType.DMA((7 * n_arr,)), pltpu.SemaphoreType.DMA((7 * n_arr,)),
                        pltpu.SemaphoreType.DMA((n_arr,))],
    )(*shards)


def _ada_fwd(c16, w_shard, b_cols):
    rows, d = c16.shape
    nc = w_shard.shape[1]

    def body(c_ref, w_ref, b_ref, o_ref):
        sc = _silu(c_ref[...]).astype(BF16)
        o_ref[...] = jnp.dot(sc, w_ref[...].astype(BF16), preferred_element_type=F32) + b_ref[...]

    return pl.pallas_call(
        body, name="ada_fwd", out_shape=jax.ShapeDtypeStruct((rows, nc), F32),
        in_specs=[pl.BlockSpec(memory_space=pltpu.VMEM)] * 3, out_specs=pl.BlockSpec(memory_space=pltpu.VMEM),
        compiler_params=_params(),
    )(c16, w_shard, b_cols)


def _adamw(w, g, m, v):
    m2 = ADAM_B1 * m + (1.0 - ADAM_B1) * g
    v2 = ADAM_B2 * v + (1.0 - ADAM_B2) * (g * g)
    m_hat = m2 * (1.0 / (1.0 - ADAM_B1 ** ADAM_STEP))
    v_hat = v2 * (1.0 / (1.0 - ADAM_B2 ** ADAM_STEP))
    delta = -ADAM_LR * (m_hat / (jnp.sqrt(v_hat) + ADAM_EPS) + ADAM_WD * w)
    return delta, m2, v2


def _ada_bwd(c16, dmod16, w, m, v):
    rows, d = c16.shape
    nc = w.shape[1]
    tr = min(512, d)

    def body(c_ref, dm_ref, w_ref, m_ref, v_ref, g_ref, dl_ref, m2_ref, v2_ref):
        sc = _silu(c_ref[...]).astype(BF16)
        g = lax.dot_general(sc, dm_ref[...].astype(BF16), TN_DIMS, preferred_element_type=F32)
        g_ref[...] = g
        dl_ref[...], m2_ref[...], v2_ref[...] = _adamw(w_ref[...], g, m_ref[...], v_ref[...])

    spec = _row_spec(tr, nc)
    return pl.pallas_call(
        body, name="ada_bwd", grid=(d // tr,),
        out_shape=[jax.ShapeDtypeStruct((d, nc), F32)] * 4,
        in_specs=[pl.BlockSpec((rows, tr), lambda i: (0, i)), _const_spec((rows, nc)), spec, spec, spec],
        out_specs=[spec] * 4, compiler_params=_params(("arbitrary",)),
    )(c16, dmod16, w, m, v)


def _split3(v):
    hi = v.astype(BF16)
    r1 = v - hi.astype(F32)
    mid = r1.astype(BF16)
    lo = (r1 - mid.astype(F32)).astype(BF16)
    return hi, mid, lo


BIAS_CHUNK = QBLK * SPAN // 8


def _bias_build(table_t, idx):
    n = idx.shape[1]

    def body(t_ref, i_ref, o_ref):
        ids = i_ref[...]
        onehot = (lax.broadcasted_iota(jnp.int32, (N_BUCKETS, BIAS_CHUNK), 0) == ids).astype(BF16)
        t16 = jnp.concatenate([t_ref[...], jnp.zeros_like(t_ref[...])], axis=0)
        acc = jnp.zeros((2 * N_Q_HEADS, BIAS_CHUNK), F32)
        for part in _split3(t16):
            acc = acc + jnp.dot(part, onehot, preferred_element_type=F32)
        o_ref[...] = jnp.where(ids >= 0, acc[:N_Q_HEADS], NEG_INF)

    return pl.pallas_call(
        body, name="bias_build", grid=(n // BIAS_CHUNK,),
        out_shape=jax.ShapeDtypeStruct((N_Q_HEADS, n), F32),
        in_specs=[_const_spec((N_Q_HEADS, N_BUCKETS)), pl.BlockSpec((1, BIAS_CHUNK), lambda i: (0, i))],
        out_specs=pl.BlockSpec((N_Q_HEADS, BIAS_CHUNK), lambda i: (0, i)),
        compiler_params=_params(("arbitrary",)),
    )(table_t, idx)


def _bias_reduce(dbias, idx):
    n = idx.shape[1]

    def body(d_ref, i_ref, o_ref):
        @pl.when(pl.program_id(0) == 0)
        def _():
            o_ref[...] = jnp.zeros_like(o_ref)

        onehot = (lax.broadcasted_iota(jnp.int32, (N_BUCKETS, BIAS_CHUNK), 0) == i_ref[...]).astype(BF16)
        d16 = jnp.concatenate([d_ref[...], jnp.zeros_like(d_ref[...])], axis=0)
        acc = jnp.zeros((2 * N_Q_HEADS, N_BUCKETS), F32)
        for part in _split3(d16):
            acc = acc + lax.dot_general(part, onehot, NT_DIMS, preferred_element_type=F32)
        o_ref[...] += acc[:N_Q_HEADS]

    return pl.pallas_call(
        body, name="bias_reduce", grid=(n // BIAS_CHUNK,),
        out_shape=jax.ShapeDtypeStruct((N_Q_HEADS, N_BUCKETS), F32),
        in_specs=[pl.BlockSpec((N_Q_HEADS, BIAS_CHUNK), lambda i: (0, i)), pl.BlockSpec((1, BIAS_CHUNK), lambda i: (0, i))],
        out_specs=_const_spec((N_Q_HEADS, N_BUCKETS)),
        compiler_params=_params(("arbitrary",)),
    )(dbias, idx)


def _t5_bucket(rel):
    half = N_BUCKETS // 2
    max_exact = half // 2
    ret = jnp.where(rel > 0, half, 0)
    n = jnp.abs(rel)
    nf = jnp.maximum(n, 1).astype(jnp.float32)
    large = max_exact + (jnp.log(nf / max_exact) / math.log(MAX_DISTANCE / max_exact)
                         * (half - max_exact)).astype(jnp.int32)
    large = jnp.minimum(large, half - 1)
    return ret + jnp.where(n < max_exact, n, large)


def _fwd_in(x2, vec, w_full, comm):
    s, d = x2.shape
    pw = d // 2
    tm = min(512, s)
    nt = s // tm
    widths = (ATTN_WIDTH + 2 * KV_WIDTH, ATTN_WIDTH, pw, pw)
    dtypes = (BF16, BF16, F32, BF16)

    def body(*refs):
        ins, outs, scratch, c_in, c_out, c_sems = _split_refs(refs, 3, 5, 2, comm)
        x_ref, vec_ref, w_hbm = ins
        h_ref, qkv_ref, ga_ref, u_ref, gp_ref = outs
        w_vmem, sem = scratch
        comm.run(pl.program_id(0), nt, c_in, c_out, c_sems)

        @pl.when(pl.program_id(0) == 0)
        def _():
            cp = pltpu.make_async_copy(w_hbm, w_vmem, sem)
            cp.start()
            cp.wait()

        xf = x_ref[...]
        r = lax.rsqrt(jnp.mean(xf * xf, axis=-1, keepdims=True) + EPS)
        h = ((xf * r) * vec_ref[2:3, :]) * vec_ref[0:1, :] + vec_ref[1:2, :]
        hb = h.astype(BF16)
        h_ref[...] = hb
        off = 0
        for o_ref, wd in zip((qkv_ref, ga_ref, u_ref, gp_ref), widths):
            for c0 in range(0, wd, 512):
                c1 = min(c0 + 512, wd)
                o_ref[:, c0:c1] = jnp.dot(hb, w_vmem[:, off + c0:off + c1],
                                          preferred_element_type=F32).astype(o_ref.dtype)
            off += wd

    res = pl.pallas_call(
        body, name="fwd_in", grid=(nt,),
        out_shape=[jax.ShapeDtypeStruct((s, d), BF16)] + [jax.ShapeDtypeStruct((s, wd), dt) for wd, dt in zip(widths, dtypes)]
                  + comm.out_shapes(),
        in_specs=[_row_spec(tm, d), _const_spec((8, d)), ANY_SPEC] + comm.specs(),
        out_specs=[_row_spec(tm, d)] + [_row_spec(tm, wd) for wd in widths] + comm.specs(),
        scratch_shapes=[pltpu.VMEM(w_full.shape, BF16), pltpu.SemaphoreType.DMA] + comm.scratch(),
        compiler_params=_params(("arbitrary",)),
    )(x2, vec, w_full, *comm.arrays)
    return res[:5], res[5:]


def _merge_fwd(h, w_blk, b_merge, comm):
    s, d = h.shape
    n = w_blk.shape[0] * w_blk.shape[2]
    tm = min(512, s)
    nt = s // tm

    def body(*refs):
        ins, outs, scratch, c_in, c_out, c_sems = _split_refs(refs, 3, 1, 2, comm)
        h_ref, w_hbm, b_ref = ins
        g_ref, = outs
        w_vmem, sems = scratch
        comm.run(pl.program_id(0), nt, c_in, c_out, c_sems)

        @pl.when(pl.program_id(0) == 0)
        def _():
            _load_blocked(w_hbm, w_vmem, sems, 1)

        hb = h_ref[...]
        for c0 in range(0, n, 512):
            z = jnp.dot(hb, w_vmem[:, c0:c0 + 512], preferred_element_type=F32) + b_ref[:, c0:c0 + 512]
            g_ref[:, c0:c0 + 512] = jax.nn.sigmoid(z).astype(BF16)

    res = pl.pallas_call(
        body, name="merge_fwd", grid=(nt,),
        out_shape=[jax.ShapeDtypeStruct((s, n), BF16)] + comm.out_shapes(),
        in_specs=[_row_spec(tm, d), ANY_SPEC, _const_spec((1, n))] + comm.specs(),
        out_specs=[_row_spec(tm, n)] + comm.specs(),
        scratch_shapes=[pltpu.VMEM((d, n), BF16), pltpu.SemaphoreType.DMA((N_DEV,))] + comm.scratch(),
        compiler_params=_params(("arbitrary",)),
    )(h, w_blk, b_merge, *comm.arrays)
    return res[0], res[1:]


def _bias_variants(bias):
    t = lax.broadcasted_iota(jnp.int32, (1, 1, SPAN), 2)
    return jnp.stack([jnp.where(t < WINDOW, NEG_INF, bias), bias, jnp.where(t >= WINDOW + QBLK, NEG_INF, bias)])


def _bias_spec(nblk):
    def index(n):
        return (jnp.where(n == 0, 0, jnp.where(n >= nblk - 1, 2, 1)), 0, 0, 0)
    return pl.BlockSpec((1, N_Q_HEADS, QBLK, SPAN), index)


def _attn_head(q_ref, kv, bias_ref, sink_ref, h):
    k = h // GQA_GROUP
    qh = q_ref[:, h * HEAD_DIM:(h + 1) * HEAD_DIM]
    kk = kv[:, k * HEAD_DIM:(k + 1) * HEAD_DIM]
    logits = lax.dot_general(qh, kk, NT_DIMS, preferred_element_type=F32) * ATTN_SCALE + bias_ref[0, h]
    sink = sink_ref[h]
    mx = jnp.maximum(jnp.max(logits, axis=-1, keepdims=True), sink)
    e = jnp.exp(logits - mx)
    e_sink = jnp.exp(sink - mx)
    inv = 1.0 / (jnp.sum(e, axis=-1, keepdims=True) + e_sink)
    return qh, e, inv, e_sink


def _attn_fwd(qkv, kvp, ga, bias3, sink):
    s = qkv.shape[0]
    nblk = s // QBLK
    kv_specs = [pl.BlockSpec((QBLK, 2 * KV_WIDTH), (lambda n, j=j: (n + j, 0))) for j in range(3)]

    def body(sink_ref, q_ref, k0, k1, k2, ga_ref, bias_ref, a_ref, ya_ref):
        kv = jnp.concatenate([k0[...], k1[...], k2[...]], axis=0)
        for h in range(N_Q_HEADS):
            k = h // GQA_GROUP
            cols = slice(h * HEAD_DIM, (h + 1) * HEAD_DIM)
            vk = kv[:, KV_WIDTH + k * HEAD_DIM:KV_WIDTH + (k + 1) * HEAD_DIM]
            _, e, inv, _ = _attn_head(q_ref, kv, bias_ref, sink_ref, h)
            o = jnp.dot(e.astype(BF16), vk, preferred_element_type=F32) * inv
            a_ref[:, cols] = o.astype(BF16)
            ya_ref[:, cols] = (o * _silu(ga_ref[:, cols].astype(F32))).astype(BF16)

    return pl.pallas_call(
        body, name="attn_fwd", grid=(nblk,),
        out_shape=[jax.ShapeDtypeStruct((s, ATTN_WIDTH), BF16)] * 2,
        in_specs=[pl.BlockSpec(memory_space=pltpu.SMEM), _row_spec(QBLK, ATTN_WIDTH)] + kv_specs
                 + [_row_spec(QBLK, ATTN_WIDTH), _bias_spec(nblk)],
        out_specs=[_row_spec(QBLK, ATTN_WIDTH)] * 2,
        compiler_params=_params(("arbitrary",)),
    )(sink, qkv, kvp, kvp, kvp, ga, bias3)


def _roll_rows(a, k):
    return pltpu.roll(a, k % a.shape[0], axis=0)


def _window_sum(xe, w, mirrored):
    acc = xe + _roll_rows(xe, -1 if mirrored else 1)
    width = 2
    while width < w:
        half = width // 2
        acc = _roll_rows(acc, half) + _roll_rows(acc, -half)
        width *= 2
    return acc


def _halo_specs(tm, width, s):
    nh = tm // POOL_HALO
    last = s // POOL_HALO - 1
    prev = pl.BlockSpec((POOL_HALO, width), lambda i: (jnp.maximum(i * nh - 1, 0), 0))
    nxt = pl.BlockSpec((POOL_HALO, width), lambda i: (jnp.minimum((i + 1) * nh, last), 0))
    return [prev, _row_spec(tm, width), nxt]


def _with_halo(prev_ref, cur_ref, next_ref, i, n_tiles):
    prev = jnp.where(i > 0, prev_ref[...].astype(F32), 0.0)
    nxt = jnp.where(i < n_tiles - 1, next_ref[...].astype(F32), 0.0)
    return jnp.concatenate([prev, cur_ref[...].astype(F32), nxt], axis=0)


def _pool_counts(i, tm, s, w):
    pos = i * tm + lax.broadcasted_iota(jnp.int32, (tm, 1), 0)
    lo = jnp.maximum(pos - w // 2, 0)
    hi = jnp.minimum(pos + w // 2, s)
    return (hi - lo).astype(F32)


def _pool_fwd(u, gp, wp, scale):
    s, pw = u.shape
    c = pw // len(POOL_SIZES)
    tm = min(512, s)
    nt = s // tm

    def body(up, uc, un, gp_ref, wp_ref, sc_ref, yp_ref):
        i = pl.program_id(0)
        ue = _with_halo(up, uc, un, i, nt)
        for g, w in enumerate(POOL_SIZES):
            cols = slice(g * c, (g + 1) * c)
            xg = ue[:, cols]
            ws = _window_sum(xg, w, False)[POOL_HALO:POOL_HALO + tm]
            pooled = ws * (1.0 / _pool_counts(i, tm, s, w)) - xg[POOL_HALO:POOL_HALO + tm]
            mixed = jnp.dot(pooled.astype(BF16), wp_ref[g], preferred_element_type=F32)
            yp_ref[:, cols] = (mixed * sc_ref[:, cols] * _silu(gp_ref[:, cols].astype(F32))).astype(BF16)

    return pl.pallas_call(
        body, name="pool_fwd", grid=(nt,),
        out_shape=jax.ShapeDtypeStruct((s, pw), BF16),
        in_specs=_halo_specs(tm, pw, s) + [_row_spec(tm, pw), _const_spec(wp.shape), _const_spec((1, pw))],
        out_specs=_row_spec(tm, pw),
        compiler_params=_params(("arbitrary",)),
    )(u, u, u, gp, wp, scale)


def _branch_fwd(ya, yp, g, wa_blk, wp_blk):
    s = ya.shape[0]
    d = g.shape[1] // 2
    ka, kp = ya.shape[1], yp.shape[1]
    tm = min(512, s)

    def body(ya_ref, yp_ref, g_ref, wa_hbm, wp_hbm, ba_ref, bp_ref, m_ref, wa, wp, sems):
        @pl.when(pl.program_id(0) == 0)
        def _():
            _load_blocked(wa_hbm, wa, sems.at[pl.ds(0, N_DEV)], 1)
            _load_blocked(wp_hbm, wp, sems.at[pl.ds(N_DEV, N_DEV)], 1)

        for c0 in range(0, d, 512):
            cols = slice(c0, c0 + 512)
            ba = jnp.dot(ya_ref[...], wa[:, cols], preferred_element_type=F32)
            bp = jnp.dot(yp_ref[...], wp[:, cols], preferred_element_type=F32)
            ba_ref[:, cols] = ba.astype(BF16)
            bp_ref[:, cols] = bp.astype(BF16)
            ga = g_ref[:, c0:c0 + 512].astype(F32)
            gb = g_ref[:, d + c0:d + c0 + 512].astype(F32)
            m_ref[:, cols] = (ga * ba + gb * bp).astype(BF16)

    return pl.pallas_call(
        body, name="branch_fwd", grid=(s // tm,),
        out_shape=[jax.ShapeDtypeStruct((s, d), BF16)] * 3,
        in_specs=[_row_spec(tm, ka), _row_spec(tm, kp), _row_spec(tm, 2 * d), ANY_SPEC, ANY_SPEC],
        out_specs=[_row_spec(tm, d)] * 3,
        scratch_shapes=[pltpu.VMEM((ka, d), BF16), pltpu.VMEM((kp, d), BF16), pltpu.SemaphoreType.DMA((2 * N_DEV,))],
        compiler_params=_params(("arbitrary",)),
    )(ya, yp, g, wa_blk, wp_blk)


def _out_fused(merged, x2, t2, g, ba, bp, vec, wo_blk):
    s, d = x2.shape
    tm = min(256, s)

    def body(m_ref, x_ref, t_ref, g_ref, ba_ref, bp_ref, vec_ref, wo_hbm,
             do_ref, dgm_ref, dba_ref, dbp_ref, dy_ref, acc_ref, dbm_ref, wo, sems):
        @pl.when(pl.program_id(0) == 0)
        def _():
            _load_blocked(wo_hbm, wo, sems, 0)
            acc_ref[...] = jnp.zeros_like(acc_ref)
            dbm_ref[...] = jnp.zeros_like(dbm_ref)

        gate = vec_ref[0:1, :]
        post_g = vec_ref[1:2, :]
        o = jnp.dot(m_ref[...], wo[...], preferred_element_type=F32)
        r2 = lax.rsqrt(jnp.mean(o * o, axis=-1, keepdims=True) + EPS)
        on = o * r2
        z = on * post_g
        diff = x_ref[...] + gate * z - t_ref[...]
        dy = diff * (1.0 / d)
        dy_ref[...] = dy
        dz = dy * gate
        don = dz * post_g
        do = r2 * (don - on * jnp.mean(don * on, axis=-1, keepdims=True))
        acc_ref[0:1, :] += jnp.sum(diff * diff, axis=0, keepdims=True)
        acc_ref[1:2, :] += jnp.sum(dy * z, axis=0, keepdims=True)
        acc_ref[2:3, :] += jnp.sum(dz * on, axis=0, keepdims=True)
        dob = do.astype(BF16)
        do_ref[...] = dob
        for c0 in range(0, d, 512):
            cols = slice(c0, c0 + 512)
            dm = lax.dot_general(dob, wo[c0:c0 + 512, :], NT_DIMS, preferred_element_type=F32)
            ga = g_ref[:, c0:c0 + 512].astype(F32)
            gb = g_ref[:, d + c0:d + c0 + 512].astype(F32)
            b_a = ba_ref[:, cols].astype(F32)
            b_p = bp_ref[:, cols].astype(F32)
            dba_ref[:, cols] = (dm * ga).astype(BF16)
            dbp_ref[:, cols] = (dm * gb).astype(BF16)
            dga = dm * b_a * ga * (1.0 - ga)
            dgb = dm * b_p * gb * (1.0 - gb)
            dgm_ref[:, c0:c0 + 512] = dga.astype(BF16)
            dgm_ref[:, d + c0:d + c0 + 512] = dgb.astype(BF16)
            dbm_ref[:, c0:c0 + 512] += jnp.sum(dga, axis=0, keepdims=True)
            dbm_ref[:, d + c0:d + c0 + 512] += jnp.sum(dgb, axis=0, keepdims=True)

    return pl.pallas_call(
        body, name="out_fused", grid=(s // tm,),
        out_shape=[jax.ShapeDtypeStruct((s, d), BF16), jax.ShapeDtypeStruct((s, 2 * d), BF16),
                   jax.ShapeDtypeStruct((s, d), BF16), jax.ShapeDtypeStruct((s, d), BF16),
                   jax.ShapeDtypeStruct((s, d), F32), jax.ShapeDtypeStruct((8, d), F32),
                   jax.ShapeDtypeStruct((1, 2 * d), F32)],
        in_specs=[_row_spec(tm, d), _row_spec(tm, d), _row_spec(tm, d), _row_spec(tm, 2 * d),
                  _row_spec(tm, d), _row_spec(tm, d), _const_spec((8, d)), ANY_SPEC],
        out_specs=[_row_spec(tm, d), _row_spec(tm, 2 * d), _row_spec(tm, d), _row_spec(tm, d),
                   _row_spec(tm, d), _const_spec((8, d)), _const_spec((1, 2 * d))],
        scratch_shapes=[pltpu.VMEM((d, d), BF16), pltpu.SemaphoreType.DMA((N_DEV,))],
        compiler_params=_params(("arbitrary",)),
    )(merged, x2, t2, g, ba, bp, vec, wo_blk)


def _wgrad(a, b, tn, shard_w, name, comm=NO_COMM):
    s, k = a.shape
    n = b.shape[1]
    tk = min(512, s)
    ns = s // tk
    per = tn // shard_w

    def body(*refs):
        ins, outs, scratch, c_in, c_out, c_sems = _split_refs(refs, 2, 1, 1, comm)
        a_ref, b_ref = ins
        o_ref, = outs
        acc, = scratch
        step = pl.program_id(1)
        comm.run(pl.program_id(0) * ns + step, (n // tn) * ns, c_in, c_out, c_sems)

        @pl.when(step == 0)
        def _():
            acc[...] = jnp.zeros_like(acc)

        acc[...] += lax.dot_general(a_ref[...], b_ref[...], TN_DIMS, preferred_element_type=F32)

        @pl.when(step == ns - 1)
        def _():
            for r in range(per):
                o_ref[r] = acc[:, r * shard_w:(r + 1) * shard_w].astype(BF16)

    res = pl.pallas_call(
        body, name=name, grid=(n // tn, ns),
        out_shape=[jax.ShapeDtypeStruct((n // shard_w, k, shard_w), BF16)] + comm.out_shapes(),
        in_specs=[pl.BlockSpec((tk, k), lambda j, t: (t, 0)), pl.BlockSpec((tk, tn), lambda j, t: (t, j))] + comm.specs(),
        out_specs=[pl.BlockSpec((per, k, shard_w), lambda j, t: (j, 0, 0))] + comm.specs(),
        scratch_shapes=[pltpu.VMEM((k, tn), F32)] + comm.scratch(),
        compiler_params=_params(("arbitrary", "arbitrary")),
    )(a, b, *comm.arrays)
    return res[0] if comm.n == 0 else (res[0], res[1:])


def _branch_bwd(dba, dbp, a, ga, wa_blk, wp_blk):
    s, d = dba.shape
    ka = wa_blk.shape[1]
    kp = wp_blk.shape[1]
    tm = min(512, s)

    def body(dba_ref, dbp_ref, a_ref, ga_ref, wa_hbm, wp_hbm, da_ref, dga_ref, dyp_ref, wa, wp, sems):
        @pl.when(pl.program_id(0) == 0)
        def _():
            _load_blocked(wa_hbm, wa, sems.at[pl.ds(0, N_DEV)], 1)
            _load_blocked(wp_hbm, wp, sems.at[pl.ds(N_DEV, N_DEV)], 1)

        for c0 in range(0, ka, 512):
            cols = slice(c0, c0 + 512)
            dya = lax.dot_general(dba_ref[...], wa[c0:c0 + 512, :], NT_DIMS, preferred_element_type=F32)
            gaf = ga_ref[:, cols].astype(F32)
            da_ref[:, cols] = (dya * _silu(gaf)).astype(BF16)
            dga_ref[:, cols] = (dya * a_ref[:, cols].astype(F32) * _dsilu(gaf)).astype(BF16)
        for c0 in range(0, kp, 512):
            dyp = lax.dot_general(dbp_ref[...], wp[c0:c0 + 512, :], NT_DIMS, preferred_element_type=F32)
            dyp_ref[:, c0:c0 + 512] = dyp.astype(BF16)

    return pl.pallas_call(
        body, name="branch_bwd", grid=(s // tm,),
        out_shape=[jax.ShapeDtypeStruct((s, ka), BF16), jax.ShapeDtypeStruct((s, ka), BF16),
                   jax.ShapeDtypeStruct((s, kp), BF16)],
        in_specs=[_row_spec(tm, d), _row_spec(tm, d), _row_spec(tm, ka), _row_spec(tm, ka), ANY_SPEC, ANY_SPEC],
        out_specs=[_row_spec(tm, ka), _row_spec(tm, ka), _row_spec(tm, kp)],
        scratch_shapes=[pltpu.VMEM((ka, d), BF16), pltpu.VMEM((kp, d), BF16), pltpu.SemaphoreType.DMA((2 * N_DEV,))],
        compiler_params=_params(("arbitrary",)),
    )(dba, dbp, a, ga, wa_blk, wp_blk)


def _attn_bwd(qkv, kvp, a, da, bias3, sink, comm):
    s = qkv.shape[0]
    nblk = s // QBLK
    clamp = lambda n: jnp.minimum(n, nblk - 1)
    q_spec = pl.BlockSpec((QBLK, ATTN_WIDTH), lambda n: (clamp(n), 0))
    kv_specs = [pl.BlockSpec((QBLK, 2 * KV_WIDTH), (lambda n, j=j: (clamp(n) + j, 0))) for j in range(3)]

    def body(*refs):
        ins, outs, scratch, c_in, c_out, c_sems = _split_refs(refs, 8, 4, 2, comm)
        sink_ref, q_ref, k0, k1, k2, a_ref, da_ref, bias_ref = ins
        dq_ref, dkv_ref, db_ref, ds_ref = outs
        acc, sacc = scratch
        n = pl.program_id(0)
        comm.run(n, nblk + 1, c_in, c_out, c_sems)

        @pl.when(n == 0)
        def _():
            acc[...] = jnp.zeros_like(acc)
            sacc[...] = jnp.zeros_like(sacc)
            db_ref[...] = jnp.zeros_like(db_ref)

        @pl.when(n < nblk)
        def _():
            kv = jnp.concatenate([k0[...], k1[...], k2[...]], axis=0)
            dks, dvs = [], []
            for k in range(N_KV_HEADS):
                kk = kv[:, k * HEAD_DIM:(k + 1) * HEAD_DIM]
                vk = kv[:, KV_WIDTH + k * HEAD_DIM:KV_WIDTH + (k + 1) * HEAD_DIM]
                heads = list(range(GQA_GROUP * k, GQA_GROUP * (k + 1)))
                stack = lambda ref: jnp.concatenate([ref[:, h * HEAD_DIM:(h + 1) * HEAD_DIM] for h in heads], axis=0)
                qk, do, o = stack(q_ref), stack(da_ref), stack(a_ref)
                bias = jnp.concatenate([bias_ref[0, h] for h in heads], axis=0)
                sink = jnp.concatenate([jnp.full((QBLK, 1), sink_ref[h], F32) for h in heads], axis=0)
                logits = lax.dot_general(qk, kk, NT_DIMS, preferred_element_type=F32) * ATTN_SCALE + bias
                mx = jnp.maximum(jnp.max(logits, axis=-1, keepdims=True), sink)
                e = jnp.exp(logits - mx)
                e_sink = jnp.exp(sink - mx)
                inv = 1.0 / (jnp.sum(e, axis=-1, keepdims=True) + e_sink)
                p = e * inv
                dp = lax.dot_general(do, vk, NT_DIMS, preferred_element_type=F32)
                delta = jnp.sum(do.astype(F32) * o.astype(F32), axis=-1, keepdims=True)
                dlog = p * (dp - delta)
                d_sink = -(e_sink * inv * delta)
                dlb = dlog.astype(BF16)
                dq = jnp.dot(dlb, kk, preferred_element_type=F32) * ATTN_SCALE
                for g, h in enumerate(heads):
                    rows = slice(g * QBLK, (g + 1) * QBLK)
                    sacc[h] += d_sink[rows]
                    db_ref[h] += dlog[rows]
                    dq_ref[:, h * HEAD_DIM:(h + 1) * HEAD_DIM] = dq[rows].astype(BF16)
                dks.append(lax.dot_general(dlb, qk, TN_DIMS, preferred_element_type=F32))
                dvs.append(lax.dot_general(p.astype(BF16), do, TN_DIMS, preferred_element_type=F32))
            contrib = jnp.concatenate(dks + dvs, axis=1)
            acc[n % 3] += contrib[0:QBLK]
            acc[(n + 1) % 3] += contrib[QBLK:2 * QBLK]
            acc[(n + 2) % 3] = contrib[2 * QBLK:3 * QBLK]

        col = lax.broadcasted_iota(jnp.int32, (1, 2 * KV_WIDTH), 1)
        dkv_ref[...] = (acc[n % 3] * jnp.where(col < KV_WIDTH, ATTN_SCALE, 1.0)).astype(BF16)

        @pl.when(n == nblk)
        def _():
            for h in range(N_Q_HEADS):
                ds_ref[h:h + 1, :] = jnp.full((1, 128), jnp.sum(sacc[h]), F32)

    res = pl.pallas_call(
        body, name="attn_bwd", grid=(nblk + 1,),
        out_shape=[jax.ShapeDtypeStruct((s, ATTN_WIDTH), BF16), jax.ShapeDtypeStruct((s, 2 * KV_WIDTH), BF16),
                   jax.ShapeDtypeStruct((N_Q_HEADS, QBLK, SPAN), F32),
                   jax.ShapeDtypeStruct((N_Q_HEADS, 128), F32)] + comm.out_shapes(),
        in_specs=[pl.BlockSpec(memory_space=pltpu.SMEM), q_spec] + kv_specs
                 + [q_spec, q_spec, _bias_spec(nblk)] + comm.specs(),
        out_specs=[q_spec, pl.BlockSpec((QBLK, 2 * KV_WIDTH), lambda n: (jnp.maximum(n - 1, 0), 0)),
                   _const_spec((N_Q_HEADS, QBLK, SPAN)), _const_spec((N_Q_HEADS, 128))] + comm.specs(),
        scratch_shapes=[pltpu.VMEM((3, QBLK, 2 * KV_WIDTH), F32), pltpu.VMEM((N_Q_HEADS, QBLK, 1), F32)]
                       + comm.scratch(),
        compiler_params=_params(("arbitrary",)),
    )(sink, qkv, kvp, kvp, kvp, a, da, bias3, *comm.arrays)
    return res[:4], res[4:]


def _pool_bwd_rows(u, gp, dyp, wp, scale):
    s, pw = u.shape
    c = pw // len(POOL_SIZES)
    tm = min(512, s)
    nt = s // tm

    def body(up, uc, un, gp_ref, dyp_ref, wp_ref, sc_ref, e_ref, dgp_ref, gwp_ref, dsc_ref):
        i = pl.program_id(0)

        @pl.when(i == 0)
        def _():
            gwp_ref[...] = jnp.zeros_like(gwp_ref)
            dsc_ref[...] = jnp.zeros_like(dsc_ref)

        ue = _with_halo(up, uc, un, i, nt)
        for g, w in enumerate(POOL_SIZES):
            cols = slice(g * c, (g + 1) * c)
            xg = ue[:, cols]
            inv_cnt = 1.0 / _pool_counts(i, tm, s, w)
            ws = _window_sum(xg, w, False)[POOL_HALO:POOL_HALO + tm]
            pooled = (ws * inv_cnt - xg[POOL_HALO:POOL_HALO + tm]).astype(BF16)
            mixed = jnp.dot(pooled, wp_ref[g], preferred_element_type=F32)
            sc = sc_ref[:, cols]
            gpf = gp_ref[:, cols].astype(F32)
            dyp_g = dyp_ref[:, cols].astype(F32)
            dp = dyp_g * _silu(gpf)
            dgp_ref[:, cols] = (dyp_g * (mixed * sc) * _dsilu(gpf)).astype(BF16)
            dsc_ref[:, cols] += jnp.sum(dp * mixed, axis=0, keepdims=True)
            dmixed = (dp * sc).astype(BF16)
            dpooled = lax.dot_general(dmixed, wp_ref[g], NT_DIMS, preferred_element_type=F32)
            gwp_ref[g] += lax.dot_general(pooled, dmixed, TN_DIMS, preferred_element_type=F32)
            e_ref[:, cols] = dpooled * inv_cnt

    return pl.pallas_call(
        body, name="pool_bwd_rows", grid=(nt,),
        out_shape=[jax.ShapeDtypeStruct((s, pw), F32), jax.ShapeDtypeStruct((s, pw), BF16),
                   jax.ShapeDtypeStruct(wp.shape, F32), jax.ShapeDtypeStruct((1, pw), F32)],
        in_specs=_halo_specs(tm, pw, s) + [_row_spec(tm, pw), _row_spec(tm, pw), _const_spec(wp.shape), _const_spec((1, pw))],
        out_specs=[_row_spec(tm, pw), _row_spec(tm, pw), _const_spec(wp.shape), _const_spec((1, pw))],
        compiler_params=_params(("arbitrary",)),
    )(u, u, u, gp, dyp, wp, scale)


def _pool_bwd_window(e):
    s, pw = e.shape
    c = pw // len(POOL_SIZES)
    tm = min(512, s)
    nt = s // tm

    def body(ep, ec, en, du_ref):
        i = pl.program_id(0)
        ee = _with_halo(ep, ec, en, i, nt)
        for g, w in enumerate(POOL_SIZES):
            cols = slice(g * c, (g + 1) * c)
            eg = ee[:, cols]
            ws = _window_sum(eg, w, True)[POOL_HALO:POOL_HALO + tm]
            du_ref[:, cols] = (ws - eg[POOL_HALO:POOL_HALO + tm] * _pool_counts(i, tm, s, w)).astype(BF16)

    return pl.pallas_call(
        body, name="pool_bwd_window", grid=(nt,),
        out_shape=jax.ShapeDtypeStruct((s, pw), BF16),
        in_specs=_halo_specs(tm, pw, s), out_specs=_row_spec(tm, pw),
        compiler_params=_params(("arbitrary",)),
    )(e, e, e)


def _dh_merge(dgm, pieces, w_blk, w_full, comm):
    s, n = dgm.shape
    d = w_blk.shape[1]
    tm = min(512, s)
    nt = s // tm
    widths = [p.shape[1] for p in pieces]
    n_p = len(pieces)
    wcols = sum(widths)

    def body(*refs):
        ins, outs, scratch, c_in, c_out, c_sems = _split_refs(refs, 3 + n_p, 1, 4, comm)
        dg_ref = ins[0]
        p_refs = ins[1:1 + n_p]
        w_hbm, wi_hbm = ins[1 + n_p:]
        o_ref, = outs
        w_vmem, wi_vmem, sems, sem = scratch
        comm.run(pl.program_id(0), nt, c_in, c_out, c_sems)

        @pl.when(pl.program_id(0) == 0)
        def _():
            cp = pltpu.make_async_copy(wi_hbm.at[:, pl.ds(0, wcols)], wi_vmem, sem)
            cp.start()
            _load_blocked(w_hbm, w_vmem, sems, 1)
            cp.wait()

        for r0 in range(0, d, 512):
            dh = lax.dot_general(dg_ref[...], w_vmem[r0:r0 + 512, :], NT_DIMS, preferred_element_type=F32)
            off = 0
            for p_ref, wd in zip(p_refs, widths):
                dh = dh + lax.dot_general(p_ref[...], wi_vmem[r0:r0 + 512, off:off + wd], NT_DIMS,
                                          preferred_element_type=F32)
                off += wd
            o_ref[:, r0:r0 + 512] = dh

    res = pl.pallas_call(
        body, name="dh_merge", grid=(nt,),
        out_shape=[jax.ShapeDtypeStruct((s, d), F32)] + comm.out_shapes(),
        in_specs=[_row_spec(tm, n)] + [_row_spec(tm, wd) for wd in widths] + [ANY_SPEC, ANY_SPEC] + comm.specs(),
        out_specs=[_row_spec(tm, d)] + comm.specs(),
        scratch_shapes=[pltpu.VMEM((d, n), BF16), pltpu.VMEM((d, wcols), BF16), pltpu.SemaphoreType.DMA((N_DEV,)),
                        pltpu.SemaphoreType.DMA] + comm.scratch(),
        compiler_params=_params(("arbitrary",)),
    )(dgm, *pieces, w_blk, w_full, *comm.arrays)
    return res[0], res[1:]


def _dh_final(dh1, pieces, w_full, x2, dy, vec):
    s, d = x2.shape
    tm = min(256, s)
    widths = [p.shape[1] for p in pieces]
    n_p = len(pieces)
    wcols = sum(widths)
    comm = NO_COMM

    def body(*refs):
        ins, outs, scratch, _, _, _ = _split_refs(refs, 5 + n_p, 2, 2, comm)
        dh1_ref = ins[0]
        p_refs = ins[1:1 + n_p]
        w_hbm, x_ref, dy_ref, vec_ref = ins[1 + n_p:]
        gx_ref, acc_ref = outs
        w_vmem, sem = scratch

        @pl.when(pl.program_id(0) == 0)
        def _():
            cp = pltpu.make_async_copy(w_hbm.at[:, pl.ds(w_full.shape[1] - wcols, wcols)], w_vmem, sem)
            cp.start()
            cp.wait()
            acc_ref[...] = jnp.zeros_like(acc_ref)

        scale1 = vec_ref[0:1, :]
        pre_g = vec_ref[2:3, :]
        xf = x_ref[...]
        r = lax.rsqrt(jnp.mean(xf * xf, axis=-1, keepdims=True) + EPS)
        xn = xf * r
        hn = xn * pre_g
        for r0 in range(0, d, 512):
            rows = slice(r0, r0 + 512)
            dh = dh1_ref[:, rows]
            off = 0
            for p_ref, wd in zip(p_refs, widths):
                dh = dh + lax.dot_general(p_ref[...], w_vmem[r0:r0 + 512, off:off + wd], NT_DIMS,
                                          preferred_element_type=F32)
                off += wd
            acc_ref[0:1, rows] += jnp.sum(dh, axis=0, keepdims=True)
            acc_ref[1:2, rows] += jnp.sum(dh * hn[:, rows], axis=0, keepdims=True)
            dhn = dh * scale1[:, rows]
            acc_ref[2:3, rows] += jnp.sum(dhn * xn[:, rows], axis=0, keepdims=True)
            gx_ref[:, rows] = dhn * pre_g[:, rows]
        dxn = gx_ref[...]
        dx = r * (dxn - xn * jnp.mean(dxn * xn, axis=-1, keepdims=True))
        gx_ref[...] = dy_ref[...] + dx

    return pl.pallas_call(
        body, name="dh_final", grid=(s // tm,),
        out_shape=[jax.ShapeDtypeStruct((s, d), F32), jax.ShapeDtypeStruct((8, d), F32)],
        in_specs=[_row_spec(tm, d)] + [_row_spec(tm, wd) for wd in widths]
                 + [ANY_SPEC, _row_spec(tm, d), _row_spec(tm, d), _const_spec((8, d))],
        out_specs=[_row_spec(tm, d), _const_spec((8, d))],
        scratch_shapes=[pltpu.VMEM((d, wcols), BF16), pltpu.SemaphoreType.DMA],
        compiler_params=_params(("arbitrary",)),
    )(dh1, *pieces, w_full, x2, dy, vec)


def _reduce_adamw(parts, w, m, v, name):
    r, c = w.shape
    n_parts = len(parts)
    rp = r // n_parts
    tr = min(256, rp)
    per = rp // tr

    def body(*refs):
        p_refs = refs[:n_parts]
        w_ref, m_ref, v_ref, g_ref, dl_ref, m2_ref, v2_ref = refs[n_parts:]
        i = pl.program_id(0)
        for k, p_ref in enumerate(p_refs):
            @pl.when((i >= k * per) & (i < (k + 1) * per))
            def _(p_ref=p_ref):
                g = p_ref[0].astype(F32)
                for dev in range(1, N_DEV):
                    g = g + p_ref[dev].astype(F32)
                g_ref[...] = g
                dl_ref[...], m2_ref[...], v2_ref[...] = _adamw(w_ref[...], g, m_ref[...], v_ref[...])

    spec = _row_spec(tr, c)
    part_specs = [pl.BlockSpec((N_DEV, tr, c), (lambda i, k=k: (0, jnp.clip(i - k * per, 0, per - 1), 0)))
                  for k in range(n_parts)]
    return pl.pallas_call(
        body, name=name, grid=(r // tr,),
        out_shape=[jax.ShapeDtypeStruct((r, c), F32)] * 4,
        in_specs=part_specs + [spec, spec, spec],
        out_specs=[spec] * 4, compiler_params=_params(("arbitrary",)),
    )(*parts, w, m, v)


def _small_reduce_adamw(gathered, w, m, v):
    rows, c = w.shape

    def body(p_ref, w_ref, m_ref, v_ref, g_ref, dl_ref, m2_ref, v2_ref):
        g = p_ref[0:rows, :]
        for dev in range(1, N_DEV):
            g = g + p_ref[dev * rows:(dev + 1) * rows, :]
        g_ref[...] = g
        dl_ref[...], m2_ref[...], v2_ref[...] = _adamw(w_ref[...], g, m_ref[...], v_ref[...])

    vm = pl.BlockSpec(memory_space=pltpu.VMEM)
    return pl.pallas_call(
        body, name="small_reduce_adamw", out_shape=[jax.ShapeDtypeStruct((rows, c), F32)] * 4,
        in_specs=[vm] * 4, out_specs=[vm] * 4, compiler_params=_params(),
    )(gathered, w, m, v)


def _pack_small(parts, total):
    flat = jnp.concatenate([p.reshape(-1).astype(F32) for p in parts])
    flat = jnp.pad(flat, (0, total - flat.shape[0]))
    return flat.reshape(8, total // 8)


def _unpack_small(packed, shapes):
    flat = packed.reshape(-1)
    out, off = [], 0
    for shp in shapes:
        size = math.prod(shp)
        out.append(flat[off:off + size].reshape(shp))
        off += size
    return out


def kernel(x, c, rel_bias_table, w_ada, b_ada, pre_norm_g, post_norm_g, w_in, attn_sink, w_pool_group, pool_scale, w_branch_attn, w_branch_pool, w_merge, b_merge, w_out, loss_target, m_rel_bias_table, m_w_ada, m_b_ada, m_pre_norm_g, m_post_norm_g, m_w_in, m_attn_sink, m_w_pool_group, m_pool_scale, m_w_branch_attn, m_w_branch_pool, m_w_merge, m_b_merge, m_w_out, v_rel_bias_table, v_w_ada, v_b_ada, v_pre_norm_g, v_post_norm_g, v_w_in, v_attn_sink, v_w_pool_group, v_pool_scale, v_w_branch_attn, v_w_branch_pool, v_w_merge, v_b_merge, v_w_out):
    xi, yi, ci = lax.axis_index("x"), lax.axis_index("y"), lax.axis_index("c")
    me = 4 * xi + 2 * yi + ci
    x2, t2 = x[0], loss_target[0]
    s, d = x2.shape
    pw = d // 2
    n_groups = len(POOL_SIZES)
    cg = pw // n_groups
    in_w = w_in.shape[2] * N_DEV
    ada_cols = w_ada.shape[2]

    c_all = _allgather_small(jnp.broadcast_to(c, (8, d)), "gather_c").reshape(N_DEV, 8, d)[:, 0, :]
    c16 = jnp.concatenate([c_all, jnp.zeros_like(c_all)], axis=0)
    b_cols = lax.dynamic_slice_in_dim(b_ada, me * ada_cols, ada_cols, axis=1)
    mod_part = _ada_fwd(c16, w_ada[0], b_cols)[:8]
    mod_all = _allgather_small(mod_part, "gather_mod").reshape(N_DEV, 8, ada_cols)
    mod = lax.dynamic_index_in_dim(mod_all, me, axis=1, keepdims=False).reshape(3 * d)
    shift, scale, gate = mod[:d], mod[d:2 * d], mod[2 * d:]
    zrow = jnp.zeros((d,), F32)
    vec_in = jnp.stack([1.0 + scale, shift, pre_norm_g[0], zrow, zrow, zrow, zrow, zrow])
    vec_out = jnp.stack([gate, post_norm_g[0], zrow, zrow, zrow, zrow, zrow, zrow])

    rows_per = w_pool_group.shape[2]
    late_shards = [w.astype(BF16) for w in (w_branch_attn[0], w_branch_pool[0], w_out[0],
                                            w_pool_group[0].reshape(n_groups * rows_per, cg))]
    w_in_blk, = _allgather_hbm([w_in[0].astype(BF16)], "gather_w_in")
    w_in_full = w_in_blk.transpose(1, 0, 2).reshape(d, in_w)

    rel = jnp.arange(SPAN)[None, :] - WINDOW - jnp.arange(QBLK)[:, None]
    idx = jnp.where(jnp.abs(rel) <= WINDOW, _t5_bucket(rel), -1).astype(jnp.int32).reshape(1, QBLK * SPAN)
    bias = _bias_variants(_bias_build(rel_bias_table.T, idx).reshape(N_Q_HEADS, QBLK, SPAN))
    sink = attn_sink[0]

    (h, qkv, ga, u, gp), (w_merge_blk,) = _fwd_in(x2, vec_in, w_in_full, _Carried("gather", [w_merge[0].astype(BF16)]))
    g, (w_ba_blk, w_bp_blk, w_out_blk, w_pool_blk) = _merge_fwd(h, w_merge_blk, b_merge, _Carried("gather", late_shards))
    w_pool_full = w_pool_blk.reshape(N_DEV, n_groups, rows_per, cg).transpose(1, 0, 2, 3).reshape(n_groups, cg, cg)
    kvp = jnp.pad(qkv[:, ATTN_WIDTH:], ((WINDOW, WINDOW), (0, 0)))
    a, ya = _attn_fwd(qkv, kvp, ga, bias, sink)
    yp = _pool_fwd(u, gp, w_pool_full, pool_scale)
    ba, bp, merged = _branch_fwd(ya, yp, g, w_ba_blk, w_bp_blk)
    do, dgm, dba, dbp, dy, acc_out, db_merge = _out_fused(merged, x2, t2, g, ba, bp, vec_out, w_out_blk)
    loss = lax.psum(0.5 * jnp.sum(acc_out[0]) / d, MESH_AXES)

    gw_out = _wgrad(merged, do, d, d, "wgrad_out")[0].reshape(N_DEV, d // N_DEV, d)
    gw_ba = _wgrad(ya, dba, min(d, 1024), d // N_DEV, "wgrad_branch_attn")
    gw_bp = _wgrad(yp, dbp, min(d, 1024), d // N_DEV, "wgrad_branch_pool")
    gw_merge, (r_out, r_ba, r_bp) = _wgrad(h, dgm, min(2 * d, 1024), 2 * d // N_DEV, "wgrad_merge",
                                           _Carried("exchange", [gw_out, gw_ba, gw_bp]))
    da, dga, dyp = _branch_bwd(dba, dbp, a, ga, w_ba_blk, w_bp_blk)
    (dq, dkv, dbias, dsink), (r_merge,) = _attn_bwd(qkv, kvp, a, da, bias, sink, _Carried("exchange", [gw_merge]))
    e, dgp, gw_pool, dpool_scale = _pool_bwd_rows(u, gp, dyp, w_pool_full, pool_scale)
    du = _pool_bwd_window(e)
    pieces = [dq, dkv, dga, du, dgp]
    gw_in = jnp.concatenate([_wgrad(h, p, p.shape[1], p.shape[1], "wgrad_in_%d" % i)[0]
                             for i, p in enumerate(pieces)], axis=1)
    gw_in_blk = gw_in.reshape(d, N_DEV, in_w // N_DEV).transpose(1, 0, 2)
    gw_pool_blk = gw_pool.astype(BF16).reshape(n_groups, N_DEV, rows_per, cg).transpose(1, 0, 2, 3).reshape(
        N_DEV, n_groups * rows_per, cg)
    dh1, (r_in, r_pool) = _dh_merge(dgm, pieces[:3], w_merge_blk, w_in_full, _Carried("exchange", [gw_in_blk, gw_pool_blk]))
    gx, acc_in = _dh_final(dh1, pieces[3:], w_in_full, x2, dy, vec_in)
    d_table = _bias_reduce(dbias.reshape(N_Q_HEADS, QBLK * SPAN), idx).T

    recv = [[r_in], [r_merge], [r_ba], [r_bp], [r_out], [r_pool]]
    names = ["w_in", "w_merge", "w_branch_attn", "w_branch_pool", "w_out", "w_pool_group"]
    big_w = [w_in, w_merge, w_branch_attn, w_branch_pool, w_out, w_pool_group]
    big_m = [m_w_in, m_w_merge, m_w_branch_attn, m_w_branch_pool, m_w_out, m_w_pool_group]
    big_v = [v_w_in, v_w_merge, v_w_branch_attn, v_w_branch_pool, v_w_out, v_w_pool_group]
    big = {}
    for nm, rc, w_, m_, v_ in zip(names, recv, big_w, big_m, big_v):
        shp2 = (len(rc) * rc[0].shape[1], rc[0].shape[2])
        outs = _reduce_adamw(rc, w_.reshape(shp2), m_.reshape(shp2), v_.reshape(shp2), "adamw_" + nm)
        big[nm] = [o.reshape(w_.shape) for o in outs]

    dmod = jnp.concatenate([acc_in[0], acc_in[1], acc_out[1]])
    small_names = ["b_ada", "pre_norm_g", "post_norm_g", "pool_scale", "b_merge", "rel_bias_table", "attn_sink"]
    small_g = [dmod, acc_in[2], acc_out[2], dpool_scale, db_merge, d_table, dsink[:, 0]]
    small_w = [b_ada, pre_norm_g, post_norm_g, pool_scale, b_merge, rel_bias_table, attn_sink]
    small_m = [m_b_ada, m_pre_norm_g, m_post_norm_g, m_pool_scale, m_b_merge, m_rel_bias_table, m_attn_sink]
    small_v = [v_b_ada, v_pre_norm_g, v_post_norm_g, v_pool_scale, v_b_merge, v_rel_bias_table, v_attn_sink]
    n_small = sum(math.prod(w_.shape) for w_ in small_w)
    total = -(-n_small // 1024) * 1024
    gathered = _allgather_small(_pack_small(small_g, total), "gather_small_grads")
    sm = _small_reduce_adamw(gathered, _pack_small(small_w, total), _pack_small(small_m, total),
                             _pack_small(small_v, total))
    shapes = [w_.shape for w_ in small_w]
    small = {nm: vals for nm, vals in zip(small_names, zip(*[_unpack_small(o, shapes) for o in sm]))}

    dmod_all = gathered.reshape(N_DEV, -1)[:, :3 * d]
    dmod_mine = lax.dynamic_slice_in_dim(dmod_all, me * ada_cols, ada_cols, axis=1)
    dmod16 = jnp.concatenate([dmod_mine, jnp.zeros_like(dmod_mine)], axis=0)
    ada = [o[None] for o in _ada_bwd(c16, dmod16, w_ada[0], m_w_ada[0], v_w_ada[0])]

    order = ["rel_bias_table", "w_ada", "b_ada", "pre_norm_g", "post_norm_g", "w_in", "attn_sink", "w_pool_group",
             "pool_scale", "w_branch_attn", "w_branch_pool", "w_merge", "b_merge", "w_out"]
    results = dict(big)
    results.update({nm: list(vals) for nm, vals in small.items()})
    results["w_ada"] = ada
    outs = [loss, gx[None]]
    for k in range(4):
        outs += [results[nm][k] for nm in order]
    return tuple(outs)
```

```python
import functools
import math

import jax
import jax.numpy as jnp
from jax import lax
from jax.experimental import pallas as pl
from jax.experimental.pallas import tpu as pltpu

F32 = jnp.float32
BF16 = jnp.bfloat16

HEAD_DIM = 128
N_Q_HEADS = 8
N_KV_HEADS = 2
GQA_GROUP = N_Q_HEADS // N_KV_HEADS
ATTN_WIDTH = N_Q_HEADS * HEAD_DIM
KV_WIDTH = N_KV_HEADS * HEAD_DIM
WINDOW = 128
QBLK = 128
SPAN = QBLK + 2 * WINDOW
N_BUCKETS = 32
MAX_DISTANCE = 128
POOL_SIZES = (2, 4, 8, 16)
EPS = 1e-6
NEG_INF = -1e30
ATTN_SCALE = HEAD_DIM ** -0.5

ADAM_LR = 0.001
ADAM_B1 = 0.9
ADAM_B2 = 0.999
ADAM_EPS = 1e-08
ADAM_WD = 0.01
ADAM_STEP = 10

N_DEV = 8
MESH_AXES = ("x", "y", "c")
MESH = pl.DeviceIdType.MESH
POOL_HALO = 16
V7X_VMEM_LIMIT = 56 * 1024 * 1024

NT_DIMS = (((1,), (1,)), ((), ()))
TN_DIMS = (((0,), (0,)), ((), ()))


def _params(sem=None, vmem=V7X_VMEM_LIMIT):
    return pltpu.CompilerParams(dimension_semantics=sem, vmem_limit_bytes=vmem)


def _silu(v):
    return v * jax.nn.sigmoid(v)


def _dsilu(v):
    s = jax.nn.sigmoid(v)
    return s * (1.0 + v * (1.0 - s))


def _row_spec(tm, width):
    return pl.BlockSpec((tm, width), lambda i: (i, 0))


def _const_spec(shape):
    return pl.BlockSpec(shape, lambda *_: (0,) * len(shape))


ANY_SPEC = pl.BlockSpec(memory_space=pl.ANY)


def _load_blocked(w_blk, w_vmem, sems, axis):
    n = w_blk.shape[0]
    width = w_blk.shape[1 + axis]
    copies = []
    for d in range(n):
        dst = w_vmem.at[pl.ds(d * width, width), :] if axis == 0 else w_vmem.at[:, pl.ds(d * width, width)]
        copies.append(pltpu.make_async_copy(w_blk.at[d], dst, sems.at[d]))
    for cp in copies:
        cp.start()
    for cp in copies:
        cp.wait()


def _mesh_pos():
    return lax.axis_index("x"), lax.axis_index("y"), lax.axis_index("c")


def _allgather_small(blk, name):
    m_per, n = blk.shape

    def body(x_ref, out_ref, send_sems, recv_sems, local_sem):
        x, y, c = _mesh_pos()
        me, sibling = (x, y, c), (x, y, 1 - c)
        chips = [(1 - x, y), (x, 1 - y), (1 - x, 1 - y)]

        def rows(px, py, pc):
            return out_ref.at[pl.ds((4 * px + 2 * py + pc) * m_per, m_per), :]

        def copy(k, block, to, src=None):
            return pltpu.make_async_remote_copy(
                src_ref=rows(*block) if src is None else src, dst_ref=rows(*block),
                send_sem=send_sems.at[k], recv_sem=recv_sems.at[k], device_id=to, device_id_type=MESH)

        mine = pltpu.make_async_copy(x_ref, rows(*me), local_sem)
        mine.start()
        first = [copy(0, me, sibling, src=x_ref)]
        first += [copy(1 + j, me, (*chip, c), src=x_ref) for j, chip in enumerate(chips)]
        for cp in first:
            cp.start()
        passed = [copy(4 + j, (*chip, c), sibling) for j, chip in enumerate(chips)]
        for j, chip in enumerate(chips):
            copy(1 + j, (*chip, c), me).wait_recv()
            passed[j].start()
        copy(0, sibling, me).wait_recv()
        for j, chip in enumerate(chips):
            copy(4 + j, (*chip, 1 - c), me).wait_recv()
        for cp in first + passed:
            cp.wait_send()
        mine.wait()

    return pl.pallas_call(
        body, name=name,
        out_shape=jax.ShapeDtypeStruct((N_DEV * m_per, n), blk.dtype),
        in_specs=[pl.BlockSpec(memory_space=pltpu.VMEM)],
        out_specs=pl.BlockSpec(memory_space=pltpu.VMEM),
        scratch_shapes=[pltpu.SemaphoreType.DMA((7,)), pltpu.SemaphoreType.DMA((7,)), pltpu.SemaphoreType.DMA],
    )(blk)


def _gather_copy(outs, send_sems, recv_sems, i, k, block, to, src=None):
    slot = outs[i].at[4 * block[0] + 2 * block[1] + block[2]]
    return pltpu.make_async_remote_copy(
        src_ref=slot if src is None else src, dst_ref=slot,
        send_sem=send_sems.at[7 * i + k], recv_sem=recv_sems.at[7 * i + k], device_id=to, device_id_type=MESH)


def _gather_places():
    x, y, c = _mesh_pos()
    return (x, y, c), (x, y, 1 - c), [(1 - x, y), (x, 1 - y), (1 - x, 1 - y)]


def _gather_start(ins, outs, send_sems, recv_sems, local_sems):
    me, sibling, chips = _gather_places()
    for i, src in enumerate(ins):
        pltpu.make_async_copy(src, outs[i].at[4 * me[0] + 2 * me[1] + me[2]], local_sems.at[i]).start()
        _gather_copy(outs, send_sems, recv_sems, i, 0, me, sibling, src=src).start()
        for j, chip in enumerate(chips):
            _gather_copy(outs, send_sems, recv_sems, i, 1 + j, me, (*chip, me[2]), src=src).start()


def _gather_forward(ins, outs, send_sems, recv_sems):
    me, sibling, chips = _gather_places()
    for i in range(len(ins)):
        for j, chip in enumerate(chips):
            _gather_copy(outs, send_sems, recv_sems, i, 1 + j, (*chip, me[2]), me).wait_recv()
            _gather_copy(outs, send_sems, recv_sems, i, 4 + j, (*chip, me[2]), sibling).start()


def _gather_finish(ins, outs, send_sems, recv_sems, local_sems):
    me, sibling, chips = _gather_places()
    for i, src in enumerate(ins):
        _gather_copy(outs, send_sems, recv_sems, i, 0, sibling, me).wait_recv()
        for j, chip in enumerate(chips):
            _gather_copy(outs, send_sems, recv_sems, i, 4 + j, (*chip, 1 - me[2]), me).wait_recv()
        _gather_copy(outs, send_sems, recv_sems, i, 0, me, sibling, src=src).wait_send()
        for j, chip in enumerate(chips):
            _gather_copy(outs, send_sems, recv_sems, i, 1 + j, me, (*chip, me[2]), src=src).wait_send()
            _gather_copy(outs, send_sems, recv_sems, i, 4 + j, (*chip, me[2]), sibling).wait_send()
        pltpu.make_async_copy(src, outs[i].at[4 * me[0] + 2 * me[1] + me[2]], local_sems.at[i]).wait()


EXCHANGE_FLIPS = [(fx, fy, fc) for fx in (0, 1) for fy in (0, 1) for fc in (0, 1)][1:]


def _exchange_copy(ins, outs, send_sems, recv_sems, i, k, arriving):
    x, y, c = _mesh_pos()
    f = EXCHANGE_FLIPS[k]
    p = (1 - x if f[0] else x, 1 - y if f[1] else y, 1 - c if f[2] else c)
    p_idx = 4 * p[0] + 2 * p[1] + p[2]
    return pltpu.make_async_remote_copy(
        src_ref=ins[i].at[p_idx], dst_ref=outs[i].at[p_idx if arriving else 4 * x + 2 * y + c],
        send_sem=send_sems.at[7 * i + k], recv_sem=recv_sems.at[7 * i + k], device_id=p, device_id_type=MESH)


def _exchange_start(ins, outs, send_sems, recv_sems, local_sems):
    x, y, c = _mesh_pos()
    me_idx = 4 * x + 2 * y + c
    for i in range(len(ins)):
        pltpu.make_async_copy(ins[i].at[me_idx], outs[i].at[me_idx], local_sems.at[i]).start()
        for k in range(7):
            _exchange_copy(ins, outs, send_sems, recv_sems, i, k, False).start()


def _exchange_finish(ins, outs, send_sems, recv_sems, local_sems):
    x, y, c = _mesh_pos()
    me_idx = 4 * x + 2 * y + c
    for i in range(len(ins)):
        for k in range(7):
            _exchange_copy(ins, outs, send_sems, recv_sems, i, k, True).wait_recv()
            _exchange_copy(ins, outs, send_sems, recv_sems, i, k, False).wait_send()
        pltpu.make_async_copy(ins[i].at[me_idx], outs[i].at[me_idx], local_sems.at[i]).wait()


class _Carried:
    def __init__(self, kind, arrays):
        self.kind = kind
        self.arrays = list(arrays)
        self.n = len(self.arrays)

    def specs(self):
        return [ANY_SPEC] * self.n

    def out_shapes(self):
        lead = (N_DEV,) if self.kind == "gather" else ()
        return [jax.ShapeDtypeStruct(lead + a.shape, a.dtype) for a in self.arrays]

    def scratch(self):
        return [pltpu.SemaphoreType.DMA((7 * self.n,)), pltpu.SemaphoreType.DMA((7 * self.n,)),
                pltpu.SemaphoreType.DMA((self.n,))]

    def run(self, step, n_steps, ins, outs, sems):
        send_sems, recv_sems, local_sems = sems
        if self.n == 0:
            return

        @pl.when(step == 0)
        def _():
            if self.kind == "gather":
                _gather_start(ins, outs, send_sems, recv_sems, local_sems)
            else:
                _exchange_start(ins, outs, send_sems, recv_sems, local_sems)

        if self.kind == "gather":
            @pl.when(step == n_steps // 2)
            def _():
                _gather_forward(ins, outs, send_sems, recv_sems)

        @pl.when(step == n_steps - 1)
        def _():
            if self.kind == "gather":
                _gather_finish(ins, outs, send_sems, recv_sems, local_sems)
            else:
                _exchange_finish(ins, outs, send_sems, recv_sems, local_sems)


NO_COMM = _Carried("exchange", [])


def _split_refs(refs, n_in, n_out, n_scratch, comm):
    a = n_in
    b = a + comm.n
    c = b + n_out
    d = c + comm.n
    e = d + n_scratch
    return refs[:a], refs[b:c], refs[d:e], refs[a:b], refs[c:d], refs[e:]


def _allgather_hbm(shards, name):
    n_arr = len(shards)

    def body(*refs):
        ins, outs = refs[:n_arr], refs[n_arr:2 * n_arr]
        send_sems, recv_sems, local_sems = refs[2 * n_arr:]
        _gather_start(ins, outs, send_sems, recv_sems, local_sems)
        _gather_forward(ins, outs, send_sems, recv_sems)
        _gather_finish(ins, outs, send_sems, recv_sems, local_sems)

    return pl.pallas_call(
        body, name=name,
        out_shape=[jax.ShapeDtypeStruct((N_DEV,) + s.shape, s.dtype) for s in shards],
        in_specs=[ANY_SPEC] * n_arr,
        out_specs=[ANY_SPEC] * n_arr,
        scratch_shapes=[pltpu.SemaphoreType.DMA((7 * n_arr,)), pltpu.SemaphoreType.DMA((7 * n_arr,)),
                        pltpu.SemaphoreType.DMA((n_arr,))],
    )(*shards)


def _ada_fwd(c16, w_shard, b_cols):
    rows, d = c16.shape
    nc = w_shard.shape[1]

    def body(c_ref, w_ref, b_ref, o_ref):
        sc = _silu(c_ref[...]).astype(BF16)
        o_ref[...] = jnp.dot(sc, w_ref[...].astype(BF16), preferred_element_type=F32) + b_ref[...]

    return pl.pallas_call(
        body, name="ada_fwd", out_shape=jax.ShapeDtypeStruct((rows, nc), F32),
        in_specs=[pl.BlockSpec(memory_space=pltpu.VMEM)] * 3, out_specs=pl.BlockSpec(memory_space=pltpu.VMEM),
        compiler_params=_params(),
    )(c16, w_shard, b_cols)


def _adamw(w, g, m, v):
    m2 = ADAM_B1 * m + (1.0 - ADAM_B1) * g
    v2 = ADAM_B2 * v + (1.0 - ADAM_B2) * (g * g)
    m_hat = m2 * (1.0 / (1.0 - ADAM_B1 ** ADAM_STEP))
    v_hat = v2 * (1.0 / (1.0 - ADAM_B2 ** ADAM_STEP))
    delta = -ADAM_LR * (m_hat / (jnp.sqrt(v_hat) + ADAM_EPS) + ADAM_WD * w)
    return delta, m2, v2


def _ada_bwd(c16, dmod16, w, m, v):
    rows, d = c16.shape
    nc = w.shape[1]
    tr = min(512, d)

    def body(c_ref, dm_ref, w_ref, m_ref, v_ref, g_ref, dl_ref, m2_ref, v2_ref):
        sc = _silu(c_ref[...]).astype(BF16)
        g = lax.dot_general(sc, dm_ref[...].astype(BF16), TN_DIMS, preferred_element_type=F32)
        g_ref[...] = g
        dl_ref[...], m2_ref[...], v2_ref[...] = _adamw(w_ref[...], g, m_ref[...], v_ref[...])

    spec = _row_spec(tr, nc)
    return pl.pallas_call(
        body, name="ada_bwd", grid=(d // tr,),
        out_shape=[jax.ShapeDtypeStruct((d, nc), F32)] * 4,
        in_specs=[pl.BlockSpec((rows, tr), lambda i: (0, i)), _const_spec((rows, nc)), spec, spec, spec],
        out_specs=[spec] * 4, compiler_params=_params(("arbitrary",)),
    )(c16, dmod16, w, m, v)


def _split3(v):
    hi = v.astype(BF16)
    r1 = v - hi.astype(F32)
    mid = r1.astype(BF16)
    lo = (r1 - mid.astype(F32)).astype(BF16)
    return hi, mid, lo


BIAS_CHUNK = QBLK * SPAN // 8


def _bias_build(table_t, idx):
    n = idx.shape[1]

    def body(t_ref, i_ref, o_ref):
        ids = i_ref[...]
        onehot = (lax.broadcasted_iota(jnp.int32, (N_BUCKETS, BIAS_CHUNK), 0) == ids).astype(BF16)
        t16 = jnp.concatenate([t_ref[...], jnp.zeros_like(t_ref[...])], axis=0)
        acc = jnp.zeros((2 * N_Q_HEADS, BIAS_CHUNK), F32)
        for part in _split3(t16):
            acc = acc + jnp.dot(part, onehot, preferred_element_type=F32)
        o_ref[...] = jnp.where(ids >= 0, acc[:N_Q_HEADS], NEG_INF)

    return pl.pallas_call(
        body, name="bias_build", grid=(n // BIAS_CHUNK,),
        out_shape=jax.ShapeDtypeStruct((N_Q_HEADS, n), F32),
        in_specs=[_const_spec((N_Q_HEADS, N_BUCKETS)), pl.BlockSpec((1, BIAS_CHUNK), lambda i: (0, i))],
        out_specs=pl.BlockSpec((N_Q_HEADS, BIAS_CHUNK), lambda i: (0, i)),
        compiler_params=_params(("arbitrary",)),
    )(table_t, idx)


def _bias_reduce(dbias, idx):
    n = idx.shape[1]

    def body(d_ref, i_ref, o_ref):
        @pl.when(pl.program_id(0) == 0)
        def _():
            o_ref[...] = jnp.zeros_like(o_ref)

        onehot = (lax.broadcasted_iota(jnp.int32, (N_BUCKETS, BIAS_CHUNK), 0) == i_ref[...]).astype(BF16)
        d16 = jnp.concatenate([d_ref[...], jnp.zeros_like(d_ref[...])], axis=0)
        acc = jnp.zeros((2 * N_Q_HEADS, N_BUCKETS), F32)
        for part in _split3(d16):
            acc = acc + lax.dot_general(part, onehot, NT_DIMS, preferred_element_type=F32)
        o_ref[...] += acc[:N_Q_HEADS]

    return pl.pallas_call(
        body, name="bias_reduce", grid=(n // BIAS_CHUNK,),
        out_shape=jax.ShapeDtypeStruct((N_Q_HEADS, N_BUCKETS), F32),
        in_specs=[pl.BlockSpec((N_Q_HEADS, BIAS_CHUNK), lambda i: (0, i)), pl.BlockSpec((1, BIAS_CHUNK), lambda i: (0, i))],
        out_specs=_const_spec((N_Q_HEADS, N_BUCKETS)),
        compiler_params=_params(("arbitrary",)),
    )(dbias, idx)


def _t5_bucket(rel):
    half = N_BUCKETS // 2
    max_exact = half // 2
    ret = jnp.where(rel > 0, half, 0)
    n = jnp.abs(rel)
    nf = jnp.maximum(n, 1).astype(jnp.float32)
    large = max_exact + (jnp.log(nf / max_exact) / math.log(MAX_DISTANCE / max_exact)
                         * (half - max_exact)).astype(jnp.int32)
    large = jnp.minimum(large, half - 1)
    return ret + jnp.where(n < max_exact, n, large)


def _fwd_in(x2, vec, w_full, comm):
    s, d = x2.shape
    pw = d // 2
    tm = min(512, s)
    nt = s // tm
    widths = (ATTN_WIDTH + 2 * KV_WIDTH, ATTN_WIDTH, pw, pw)
    dtypes = (BF16, BF16, F32, BF16)

    def body(*refs):
        ins, outs, scratch, c_in, c_out, c_sems = _split_refs(refs, 3, 5, 2, comm)
        x_ref, vec_ref, w_hbm = ins
        h_ref, qkv_ref, ga_ref, u_ref, gp_ref = outs
        w_vmem, sem = scratch
        comm.run(pl.program_id(0), nt, c_in, c_out, c_sems)

        @pl.when(pl.program_id(0) == 0)
        def _():
            cp = pltpu.make_async_copy(w_hbm, w_vmem, sem)
            cp.start()
            cp.wait()

        xf = x_ref[...]
        r = lax.rsqrt(jnp.mean(xf * xf, axis=-1, keepdims=True) + EPS)
        h = ((xf * r) * vec_ref[2:3, :]) * vec_ref[0:1, :] + vec_ref[1:2, :]
        hb = h.astype(BF16)
        h_ref[...] = hb
        off = 0
        for o_ref, wd in zip((qkv_ref, ga_ref, u_ref, gp_ref), widths):
            for c0 in range(0, wd, 512):
                c1 = min(c0 + 512, wd)
                o_ref[:, c0:c1] = jnp.dot(hb, w_vmem[:, off + c0:off + c1],
                                          preferred_element_type=F32).astype(o_ref.dtype)
            off += wd

    res = pl.pallas_call(
        body, name="fwd_in", grid=(nt,),
        out_shape=[jax.ShapeDtypeStruct((s, d), BF16)] + [jax.ShapeDtypeStruct((s, wd), dt) for wd, dt in zip(widths, dtypes)]
                  + comm.out_shapes(),
        in_specs=[_row_spec(tm, d), _const_spec((8, d)), ANY_SPEC] + comm.specs(),
        out_specs=[_row_spec(tm, d)] + [_row_spec(tm, wd) for wd in widths] + comm.specs(),
        scratch_shapes=[pltpu.VMEM(w_full.shape, BF16), pltpu.SemaphoreType.DMA] + comm.scratch(),
        compiler_params=_params(("arbitrary",)),
    )(x2, vec, w_full, *comm.arrays)
    return res[:5], res[5:]


def _merge_fwd(h, w_blk, b_merge, comm):
    s, d = h.shape
    n = w_blk.shape[0] * w_blk.shape[2]
    tm = min(512, s)
    nt = s // tm

    def body(*refs):
        ins, outs, scratch, c_in, c_out, c_sems = _split_refs(refs, 3, 1, 2, comm)
        h_ref, w_hbm, b_ref = ins
        g_ref, = outs
        w_vmem, sems = scratch
        comm.run(pl.program_id(0), nt, c_in, c_out, c_sems)

        @pl.when(pl.program_id(0) == 0)
        def _():
            _load_blocked(w_hbm, w_vmem, sems, 1)

        hb = h_ref[...]
        for c0 in range(0, n, 512):
            z = jnp.dot(hb, w_vmem[:, c0:c0 + 512], preferred_element_type=F32) + b_ref[:, c0:c0 + 512]
            g_ref[:, c0:c0 + 512] = jax.nn.sigmoid(z).astype(BF16)

    res = pl.pallas_call(
        body, name="merge_fwd", grid=(nt,),
        out_shape=[jax.ShapeDtypeStruct((s, n), BF16)] + comm.out_shapes(),
        in_specs=[_row_spec(tm, d), ANY_SPEC, _const_spec((1, n))] + comm.specs(),
        out_specs=[_row_spec(tm, n)] + comm.specs(),
        scratch_shapes=[pltpu.VMEM((d, n), BF16), pltpu.SemaphoreType.DMA((N_DEV,))] + comm.scratch(),
        compiler_params=_params(("arbitrary",)),
    )(h, w_blk, b_merge, *comm.arrays)
    return res[0], res[1:]


def _bias_variants(bias):
    t = lax.broadcasted_iota(jnp.int32, (1, 1, SPAN), 2)
    return jnp.stack([jnp.where(t < WINDOW, NEG_INF, bias), bias, jnp.where(t >= WINDOW + QBLK, NEG_INF, bias)])


def _bias_spec(nblk):
    def index(n):
        return (jnp.where(n == 0, 0, jnp.where(n >= nblk - 1, 2, 1)), 0, 0, 0)
    return pl.BlockSpec((1, N_Q_HEADS, QBLK, SPAN), index)


def _attn_head(q_ref, kv, bias_ref, sink_ref, h):
    k = h // GQA_GROUP
    qh = q_ref[:, h * HEAD_DIM:(h + 1) * HEAD_DIM]
    kk = kv[:, k * HEAD_DIM:(k + 1) * HEAD_DIM]
    logits = lax.dot_general(qh, kk, NT_DIMS, preferred_element_type=F32) * ATTN_SCALE + bias_ref[0, h]
    sink = sink_ref[h]
    mx = jnp.maximum(jnp.max(logits, axis=-1, keepdims=True), sink)
    e = jnp.exp(logits - mx)
    e_sink = jnp.exp(sink - mx)
    inv = 1.0 / (jnp.sum(e, axis=-1, keepdims=True) + e_sink)
    return qh, e, inv, e_sink


def _attn_fwd(qkv, kvp, ga, bias3, sink):
    s = qkv.shape[0]
    nblk = s // QBLK
    kv_specs = [pl.BlockSpec((QBLK, 2 * KV_WIDTH), (lambda n, j=j: (n + j, 0))) for j in range(3)]

    def body(sink_ref, q_ref, k0, k1, k2, ga_ref, bias_ref, a_ref, ya_ref):
        kv = jnp.concatenate([k0[...], k1[...], k2[...]], axis=0)
        for h in range(N_Q_HEADS):
            k = h // GQA_GROUP
            cols = slice(h * HEAD_DIM, (h + 1) * HEAD_DIM)
            vk = kv[:, KV_WIDTH + k * HEAD_DIM:KV_WIDTH + (k + 1) * HEAD_DIM]
            _, e, inv, _ = _attn_head(q_ref, kv, bias_ref, sink_ref, h)
            o = jnp.dot(e.astype(BF16), vk, preferred_element_type=F32) * inv
            a_ref[:, cols] = o.astype(BF16)
            ya_ref[:, cols] = (o * _silu(ga_ref[:, cols].astype(F32))).astype(BF16)

    return pl.pallas_call(
        body, name="attn_fwd", grid=(nblk,),
        out_shape=[jax.ShapeDtypeStruct((s, ATTN_WIDTH), BF16)] * 2,
        in_specs=[pl.BlockSpec(memory_space=pltpu.SMEM), _row_spec(QBLK, ATTN_WIDTH)] + kv_specs
                 + [_row_spec(QBLK, ATTN_WIDTH), _bias_spec(nblk)],
        out_specs=[_row_spec(QBLK, ATTN_WIDTH)] * 2,
        compiler_params=_params(("arbitrary",)),
    )(sink, qkv, kvp, kvp, kvp, ga, bias3)


def _roll_rows(a, k):
    return pltpu.roll(a, k % a.shape[0], axis=0)


def _window_sum(xe, w, mirrored):
    acc = xe + _roll_rows(xe, -1 if mirrored else 1)
    width = 2
    while width < w:
        half = width // 2
        acc = _roll_rows(acc, half) + _roll_rows(acc, -half)
        width *= 2
    return acc


def _halo_specs(tm, width, s):
    nh = tm // POOL_HALO
    last = s // POOL_HALO - 1
    prev = pl.BlockSpec((POOL_HALO, width), lambda i: (jnp.maximum(i * nh - 1, 0), 0))
    nxt = pl.BlockSpec((POOL_HALO, width), lambda i: (jnp.minimum((i + 1) * nh, last), 0))
    return [prev, _row_spec(tm, width), nxt]


def _with_halo(prev_ref, cur_ref, next_ref, i, n_tiles):
    prev = jnp.where(i > 0, prev_ref[...].astype(F32), 0.0)
    nxt = jnp.where(i < n_tiles - 1, next_ref[...].astype(F32), 0.0)
    return jnp.concatenate([prev, cur_ref[...].astype(F32), nxt], axis=0)


def _pool_counts(i, tm, s, w):
    pos = i * tm + lax.broadcasted_iota(jnp.int32, (tm, 1), 0)
    lo = jnp.maximum(pos - w // 2, 0)
    hi = jnp.minimum(pos + w // 2, s)
    return (hi - lo).astype(F32)


def _pool_fwd(u, gp, wp, scale):
    s, pw = u.shape
    c = pw // len(POOL_SIZES)
    tm = min(512, s)
    nt = s // tm

    def body(up, uc, un, gp_ref, wp_ref, sc_ref, yp_ref):
        i = pl.program_id(0)
        ue = _with_halo(up, uc, un, i, nt)
        for g, w in enumerate(POOL_SIZES):
            cols = slice(g * c, (g + 1) * c)
            xg = ue[:, cols]
            ws = _window_sum(xg, w, False)[POOL_HALO:POOL_HALO + tm]
            pooled = ws * (1.0 / _pool_counts(i, tm, s, w)) - xg[POOL_HALO:POOL_HALO + tm]
            mixed = jnp.dot(pooled.astype(BF16), wp_ref[g], preferred_element_type=F32)
            yp_ref[:, cols] = (mixed * sc_ref[:, cols] * _silu(gp_ref[:, cols].astype(F32))).astype(BF16)

    return pl.pallas_call(
        body, name="pool_fwd", grid=(nt,),
        out_shape=jax.ShapeDtypeStruct((s, pw), BF16),
        in_specs=_halo_specs(tm, pw, s) + [_row_spec(tm, pw), _const_spec(wp.shape), _const_spec((1, pw))],
        out_specs=_row_spec(tm, pw),
        compiler_params=_params(("arbitrary",)),
    )(u, u, u, gp, wp, scale)


def _branch_fwd(ya, yp, g, wa_blk, wp_blk):
    s = ya.shape[0]
    d = g.shape[1] // 2
    ka, kp = ya.shape[1], yp.shape[1]
    tm = min(512, s)

    def body(ya_ref, yp_ref, g_ref, wa_hbm, wp_hbm, ba_ref, bp_ref, m_ref, wa, wp, sems):
        @pl.when(pl.program_id(0) == 0)
        def _():
            _load_blocked(wa_hbm, wa, sems.at[pl.ds(0, N_DEV)], 1)
            _load_blocked(wp_hbm, wp, sems.at[pl.ds(N_DEV, N_DEV)], 1)

        for c0 in range(0, d, 512):
            cols = slice(c0, c0 + 512)
            ba = jnp.dot(ya_ref[...], wa[:, cols], preferred_element_type=F32)
            bp = jnp.dot(yp_ref[...], wp[:, cols], preferred_element_type=F32)
            ba_ref[:, cols] = ba.astype(BF16)
            bp_ref[:, cols] = bp.astype(BF16)
            ga = g_ref[:, c0:c0 + 512].astype(F32)
            gb = g_ref[:, d + c0:d + c0 + 512].astype(F32)
            m_ref[:, cols] = (ga * ba + gb * bp).astype(BF16)

    return pl.pallas_call(
        body, name="branch_fwd", grid=(s // tm,),
        out_shape=[jax.ShapeDtypeStruct((s, d), BF16)] * 3,
        in_specs=[_row_spec(tm, ka), _row_spec(tm, kp), _row_spec(tm, 2 * d), ANY_SPEC, ANY_SPEC],
        out_specs=[_row_spec(tm, d)] * 3,
        scratch_shapes=[pltpu.VMEM((ka, d), BF16), pltpu.VMEM((kp, d), BF16), pltpu.SemaphoreType.DMA((2 * N_DEV,))],
        compiler_params=_params(("arbitrary",)),
    )(ya, yp, g, wa_blk, wp_blk)


def _out_fused(merged, x2, t2, g, ba, bp, vec, wo_blk):
    s, d = x2.shape
    tm = min(256, s)

    def body(m_ref, x_ref, t_ref, g_ref, ba_ref, bp_ref, vec_ref, wo_hbm,
             do_ref, dgm_ref, dba_ref, dbp_ref, dy_ref, acc_ref, dbm_ref, wo, sems):
        @pl.when(pl.program_id(0) == 0)
        def _():
            _load_blocked(wo_hbm, wo, sems, 0)
            acc_ref[...] = jnp.zeros_like(acc_ref)
            dbm_ref[...] = jnp.zeros_like(dbm_ref)

        gate = vec_ref[0:1, :]
        post_g = vec_ref[1:2, :]
        o = jnp.dot(m_ref[...], wo[...], preferred_element_type=F32)
        r2 = lax.rsqrt(jnp.mean(o * o, axis=-1, keepdims=True) + EPS)
        on = o * r2
        z = on * post_g
        diff = x_ref[...] + gate * z - t_ref[...]
        dy = diff * (1.0 / d)
        dy_ref[...] = dy
        dz = dy * gate
        don = dz * post_g
        do = r2 * (don - on * jnp.mean(don * on, axis=-1, keepdims=True))
        acc_ref[0:1, :] += jnp.sum(diff * diff, axis=0, keepdims=True)
        acc_ref[1:2, :] += jnp.sum(dy * z, axis=0, keepdims=True)
        acc_ref[2:3, :] += jnp.sum(dz * on, axis=0, keepdims=True)
        dob = do.astype(BF16)
        do_ref[...] = dob
        for c0 in range(0, d, 512):
            cols = slice(c0, c0 + 512)
            dm = lax.dot_general(dob, wo[c0:c0 + 512, :], NT_DIMS, preferred_element_type=F32)
            ga = g_ref[:, c0:c0 + 512].astype(F32)
            gb = g_ref[:, d + c0:d + c0 + 512].astype(F32)
            b_a = ba_ref[:, cols].astype(F32)
            b_p = bp_ref[:, cols].astype(F32)
            dba_ref[:, cols] = (dm * ga).astype(BF16)
            dbp_ref[:, cols] = (dm * gb).astype(BF16)
            dga = dm * b_a * ga * (1.0 - ga)
            dgb = dm * b_p * gb * (1.0 - gb)
            dgm_ref[:, c0:c0 + 512] = dga.astype(BF16)
            dgm_ref[:, d + c0:d + c0 + 512] = dgb.astype(BF16)
            dbm_ref[:, c0:c0 + 512] += jnp.sum(dga, axis=0, keepdims=True)
            dbm_ref[:, d + c0:d + c0 + 512] += jnp.sum(dgb, axis=0, keepdims=True)

    return pl.pallas_call(
        body, name="out_fused", grid=(s // tm,),
        out_shape=[jax.ShapeDtypeStruct((s, d), BF16), jax.ShapeDtypeStruct((s, 2 * d), BF16),
                   jax.ShapeDtypeStruct((s, d), BF16), jax.ShapeDtypeStruct((s, d), BF16),
                   jax.ShapeDtypeStruct((s, d), F32), jax.ShapeDtypeStruct((8, d), F32),
                   jax.ShapeDtypeStruct((1, 2 * d), F32)],
        in_specs=[_row_spec(tm, d), _row_spec(tm, d), _row_spec(tm, d), _row_spec(tm, 2 * d),
                  _row_spec(tm, d), _row_spec(tm, d), _const_spec((8, d)), ANY_SPEC],
        out_specs=[_row_spec(tm, d), _row_spec(tm, 2 * d), _row_spec(tm, d), _row_spec(tm, d),
                   _row_spec(tm, d), _const_spec((8, d)), _const_spec((1, 2 * d))],
        scratch_shapes=[pltpu.VMEM((d, d), BF16), pltpu.SemaphoreType.DMA((N_DEV,))],
        compiler_params=_params(("arbitrary",)),
    )(merged, x2, t2, g, ba, bp, vec, wo_blk)


def _wgrad(a, b, tn, shard_w, name, comm=NO_COMM):
    s, k = a.shape
    n = b.shape[1]
    tk = min(512, s)
    ns = s // tk
    per = tn // shard_w

    def body(*refs):
        ins, outs, scratch, c_in, c_out, c_sems = _split_refs(refs, 2, 1, 1, comm)
        a_ref, b_ref = ins
        o_ref, = outs
        acc, = scratch
        step = pl.program_id(1)
        comm.run(pl.program_id(0) * ns + step, (n // tn) * ns, c_in, c_out, c_sems)

        @pl.when(step == 0)
        def _():
            acc[...] = jnp.zeros_like(acc)

        acc[...] += lax.dot_general(a_ref[...], b_ref[...], TN_DIMS, preferred_element_type=F32)

        @pl.when(step == ns - 1)
        def _():
            for r in range(per):
                o_ref[r] = acc[:, r * shard_w:(r + 1) * shard_w].astype(BF16)

    res = pl.pallas_call(
        body, name=name, grid=(n // tn, ns),
        out_shape=[jax.ShapeDtypeStruct((n // shard_w, k, shard_w), BF16)] + comm.out_shapes(),
        in_specs=[pl.BlockSpec((tk, k), lambda j, t: (t, 0)), pl.BlockSpec((tk, tn), lambda j, t: (t, j))] + comm.specs(),
        out_specs=[pl.BlockSpec((per, k, shard_w), lambda j, t: (j, 0, 0))] + comm.specs(),
        scratch_shapes=[pltpu.VMEM((k, tn), F32)] + comm.scratch(),
        compiler_params=_params(("arbitrary", "arbitrary")),
    )(a, b, *comm.arrays)
    return res[0] if comm.n == 0 else (res[0], res[1:])


def _branch_bwd(dba, dbp, a, ga, wa_blk, wp_blk):
    s, d = dba.shape
    ka = wa_blk.shape[1]
    kp = wp_blk.shape[1]
    tm = min(512, s)

    def body(dba_ref, dbp_ref, a_ref, ga_ref, wa_hbm, wp_hbm, da_ref, dga_ref, dyp_ref, wa, wp, sems):
        @pl.when(pl.program_id(0) == 0)
        def _():
            _load_blocked(wa_hbm, wa, sems.at[pl.ds(0, N_DEV)], 1)
            _load_blocked(wp_hbm, wp, sems.at[pl.ds(N_DEV, N_DEV)], 1)

        for c0 in range(0, ka, 512):
            cols = slice(c0, c0 + 512)
            dya = lax.dot_general(dba_ref[...], wa[c0:c0 + 512, :], NT_DIMS, preferred_element_type=F32)
            gaf = ga_ref[:, cols].astype(F32)
            da_ref[:, cols] = (dya * _silu(gaf)).astype(BF16)
            dga_ref[:, cols] = (dya * a_ref[:, cols].astype(F32) * _dsilu(gaf)).astype(BF16)
        for c0 in range(0, kp, 512):
            dyp = lax.dot_general(dbp_ref[...], wp[c0:c0 + 512, :], NT_DIMS, preferred_element_type=F32)
            dyp_ref[:, c0:c0 + 512] = dyp.astype(BF16)

    return pl.pallas_call(
        body, name="branch_bwd", grid=(s // tm,),
        out_shape=[jax.ShapeDtypeStruct((s, ka), BF16), jax.ShapeDtypeStruct((s, ka), BF16),
                   jax.ShapeDtypeStruct((s, kp), BF16)],
        in_specs=[_row_spec(tm, d), _row_spec(tm, d), _row_spec(tm, ka), _row_spec(tm, ka), ANY_SPEC, ANY_SPEC],
        out_specs=[_row_spec(tm, ka), _row_spec(tm, ka), _row_spec(tm, kp)],
        scratch_shapes=[pltpu.VMEM((ka, d), BF16), pltpu.VMEM((kp, d), BF16), pltpu.SemaphoreType.DMA((2 * N_DEV,))],
        compiler_params=_params(("arbitrary",)),
    )(dba, dbp, a, ga, wa_blk, wp_blk)


def _attn_bwd(qkv, kvp, a, da, bias3, sink, comm):
    s = qkv.shape[0]
    nblk = s // QBLK
    clamp = lambda n: jnp.minimum(n, nblk - 1)
    q_spec = pl.BlockSpec((QBLK, ATTN_WIDTH), lambda n: (clamp(n), 0))
    kv_specs = [pl.BlockSpec((QBLK, 2 * KV_WIDTH), (lambda n, j=j: (clamp(n) + j, 0))) for j in range(3)]

    def body(*refs):
        ins, outs, scratch, c_in, c_out, c_sems = _split_refs(refs, 8, 4, 2, comm)
        sink_ref, q_ref, k0, k1, k2, a_ref, da_ref, bias_ref = ins
        dq_ref, dkv_ref, db_ref, ds_ref = outs
        acc, sacc = scratch
        n = pl.program_id(0)
        comm.run(n, nblk + 1, c_in, c_out, c_sems)

        @pl.when(n == 0)
        def _():
            acc[...] = jnp.zeros_like(acc)
            sacc[...] = jnp.zeros_like(sacc)
            db_ref[...] = jnp.zeros_like(db_ref)

        @pl.when(n < nblk)
        def _():
            kv = jnp.concatenate([k0[...], k1[...], k2[...]], axis=0)
            dks, dvs = [], []
            for k in range(N_KV_HEADS):
                kk = kv[:, k * HEAD_DIM:(k + 1) * HEAD_DIM]
                vk = kv[:, KV_WIDTH + k * HEAD_DIM:KV_WIDTH + (k + 1) * HEAD_DIM]
                heads = list(range(GQA_GROUP * k, GQA_GROUP * (k + 1)))
                stack = lambda ref: jnp.concatenate([ref[:, h * HEAD_DIM:(h + 1) * HEAD_DIM] for h in heads], axis=0)
                qk, do, o = stack(q_ref), stack(da_ref), stack(a_ref)
                bias = jnp.concatenate([bias_ref[0, h] for h in heads], axis=0)
                sink = jnp.concatenate([jnp.full((QBLK, 1), sink_ref[h], F32) for h in heads], axis=0)
                logits = lax.dot_general(qk, kk, NT_DIMS, preferred_element_type=F32) * ATTN_SCALE + bias
                mx = jnp.maximum(jnp.max(logits, axis=-1, keepdims=True), sink)
                e = jnp.exp(logits - mx)
                e_sink = jnp.exp(sink - mx)
                inv = 1.0 / (jnp.sum(e, axis=-1, keepdims=True) + e_sink)
                p = e * inv
                dp = lax.dot_general(do, vk, NT_DIMS, preferred_element_type=F32)
                delta = jnp.sum(do.astype(F32) * o.astype(F32), axis=-1, keepdims=True)
                dlog = p * (dp - delta)
                d_sink = -(e_sink * inv * delta)
                dlb = dlog.astype(BF16)
                dq = jnp.dot(dlb, kk, preferred_element_type=F32) * ATTN_SCALE
                for g, h in enumerate(heads):
                    rows = slice(g * QBLK, (g + 1) * QBLK)
                    sacc[h] += d_sink[rows]
                    db_ref[h] += dlog[rows]
                    dq_ref[:, h * HEAD_DIM:(h + 1) * HEAD_DIM] = dq[rows].astype(BF16)
                dks.append(lax.dot_general(dlb, qk, TN_DIMS, preferred_element_type=F32))
                dvs.append(lax.dot_general(p.astype(BF16), do, TN_DIMS, preferred_element_type=F32))
            contrib = jnp.concatenate(dks + dvs, axis=1)
            acc[n % 3] += contrib[0:QBLK]
            acc[(n + 1) % 3] += contrib[QBLK:2 * QBLK]
            acc[(n + 2) % 3] = contrib[2 * QBLK:3 * QBLK]

        col = lax.broadcasted_iota(jnp.int32, (1, 2 * KV_WIDTH), 1)
        dkv_ref[...] = (acc[n % 3] * jnp.where(col < KV_WIDTH, ATTN_SCALE, 1.0)).astype(BF16)

        @pl.when(n == nblk)
        def _():
            for h in range(N_Q_HEADS):
                ds_ref[h:h + 1, :] = jnp.full((1, 128), jnp.sum(sacc[h]), F32)

    res = pl.pallas_call(
        body, name="attn_bwd", grid=(nblk + 1,),
        out_shape=[jax.ShapeDtypeStruct((s, ATTN_WIDTH), BF16), jax.ShapeDtypeStruct((s, 2 * KV_WIDTH), BF16),
                   jax.ShapeDtypeStruct((N_Q_HEADS, QBLK, SPAN), F32),
                   jax.ShapeDtypeStruct((N_Q_HEADS, 128), F32)] + comm.out_shapes(),
        in_specs=[pl.BlockSpec(memory_space=pltpu.SMEM), q_spec] + kv_specs
                 + [q_spec, q_spec, _bias_spec(nblk)] + comm.specs(),
        out_specs=[q_spec, pl.BlockSpec((QBLK, 2 * KV_WIDTH), lambda n: (jnp.maximum(n - 1, 0), 0)),
                   _const_spec((N_Q_HEADS, QBLK, SPAN)), _const_spec((N_Q_HEADS, 128))] + comm.specs(),
        scratch_shapes=[pltpu.VMEM((3, QBLK, 2 * KV_WIDTH), F32), pltpu.VMEM((N_Q_HEADS, QBLK, 1), F32)]
                       + comm.scratch(),
        compiler_params=_params(("arbitrary",)),
    )(sink, qkv, kvp, kvp, kvp, a, da, bias3, *comm.arrays)
    return res[:4], res[4:]


def _pool_bwd_rows(u, gp, dyp, wp, scale):
    s, pw = u.shape
    c = pw // len(POOL_SIZES)
    tm = min(512, s)
    nt = s // tm

    def body(up, uc, un, gp_ref, dyp_ref, wp_ref, sc_ref, e_ref, dgp_ref, gwp_ref, dsc_ref):
        i = pl.program_id(0)

        @pl.when(i == 0)
        def _():
            gwp_ref[...] = jnp.zeros_like(gwp_ref)
            dsc_ref[...] = jnp.zeros_like(dsc_ref)

        ue = _with_halo(up, uc, un, i, nt)
        for g, w in enumerate(POOL_SIZES):
            cols = slice(g * c, (g + 1) * c)
            xg = ue[:, cols]
            inv_cnt = 1.0 / _pool_counts(i, tm, s, w)
            ws = _window_sum(xg, w, False)[POOL_HALO:POOL_HALO + tm]
            pooled = (ws * inv_cnt - xg[POOL_HALO:POOL_HALO + tm]).astype(BF16)
            mixed = jnp.dot(pooled, wp_ref[g], preferred_element_type=F32)
            sc = sc_ref[:, cols]
            gpf = gp_ref[:, cols].astype(F32)
            dyp_g = dyp_ref[:, cols].astype(F32)
            dp = dyp_g * _silu(gpf)
            dgp_ref[:, cols] = (dyp_g * (mixed * sc) * _dsilu(gpf)).astype(BF16)
            dsc_ref[:, cols] += jnp.sum(dp * mixed, axis=0, keepdims=True)
            dmixed = (dp * sc).astype(BF16)
            dpooled = lax.dot_general(dmixed, wp_ref[g], NT_DIMS, preferred_element_type=F32)
            gwp_ref[g] += lax.dot_general(pooled, dmixed, TN_DIMS, preferred_element_type=F32)
            e_ref[:, cols] = dpooled * inv_cnt

    return pl.pallas_call(
        body, name="pool_bwd_rows", grid=(nt,),
        out_shape=[jax.ShapeDtypeStruct((s, pw), F32), jax.ShapeDtypeStruct((s, pw), BF16),
                   jax.ShapeDtypeStruct(wp.shape, F32), jax.ShapeDtypeStruct((1, pw), F32)],
        in_specs=_halo_specs(tm, pw, s) + [_row_spec(tm, pw), _row_spec(tm, pw), _const_spec(wp.shape), _const_spec((1, pw))],
        out_specs=[_row_spec(tm, pw), _row_spec(tm, pw), _const_spec(wp.shape), _const_spec((1, pw))],
        compiler_params=_params(("arbitrary",)),
    )(u, u, u, gp, dyp, wp, scale)


def _pool_bwd_window(e):
    s, pw = e.shape
    c = pw // len(POOL_SIZES)
    tm = min(512, s)
    nt = s // tm

    def body(ep, ec, en, du_ref):
        i = pl.program_id(0)
        ee = _with_halo(ep, ec, en, i, nt)
        for g, w in enumerate(POOL_SIZES):
            cols = slice(g * c, (g + 1) * c)
            eg = ee[:, cols]
            ws = _window_sum(eg, w, True)[POOL_HALO:POOL_HALO + tm]
            du_ref[:, cols] = (ws - eg[POOL_HALO:POOL_HALO + tm] * _pool_counts(i, tm, s, w)).astype(BF16)

    return pl.pallas_call(
        body, name="pool_bwd_window", grid=(nt,),
        out_shape=jax.ShapeDtypeStruct((s, pw), BF16),
        in_specs=_halo_specs(tm, pw, s), out_specs=_row_spec(tm, pw),
        compiler_params=_params(("arbitrary",)),
    )(e, e, e)


def _dh_merge(dgm, pieces, w_blk, w_full, comm):
    s, n = dgm.shape
    d = w_blk.shape[1]
    tm = min(512, s)
    nt = s // tm
    widths = [p.shape[1] for p in pieces]
    n_p = len(pieces)
    wcols = sum(widths)

    def body(*refs):
        ins, outs, scratch, c_in, c_out, c_sems = _split_refs(refs, 3 + n_p, 1, 4, comm)
        dg_ref = ins[0]
        p_refs = ins[1:1 + n_p]
        w_hbm, wi_hbm = ins[1 + n_p:]
        o_ref, = outs
        w_vmem, wi_vmem, sems, sem = scratch
        comm.run(pl.program_id(0), nt, c_in, c_out, c_sems)

        @pl.when(pl.program_id(0) == 0)
        def _():
            cp = pltpu.make_async_copy(wi_hbm.at[:, pl.ds(0, wcols)], wi_vmem, sem)
            cp.start()
            _load_blocked(w_hbm, w_vmem, sems, 1)
            cp.wait()

        for r0 in range(0, d, 512):
            dh = lax.dot_general(dg_ref[...], w_vmem[r0:r0 + 512, :], NT_DIMS, preferred_element_type=F32)
            off = 0
            for p_ref, wd in zip(p_refs, widths):
                dh = dh + lax.dot_general(p_ref[...], wi_vmem[r0:r0 + 512, off:off + wd], NT_DIMS,
                                          preferred_element_type=F32)
                off += wd
            o_ref[:, r0:r0 + 512] = dh

    res = pl.pallas_call(
        body, name="dh_merge", grid=(nt,),
        out_shape=[jax.ShapeDtypeStruct((s, d), F32)] + comm.out_shapes(),
        in_specs=[_row_spec(tm, n)] + [_row_spec(tm, wd) for wd in widths] + [ANY_SPEC, ANY_SPEC] + comm.specs(),
        out_specs=[_row_spec(tm, d)] + comm.specs(),
        scratch_shapes=[pltpu.VMEM((d, n), BF16), pltpu.VMEM((d, wcols), BF16), pltpu.SemaphoreType.DMA((N_DEV,)),
                        pltpu.SemaphoreType.DMA] + comm.scratch(),
        compiler_params=_params(("arbitrary",)),
    )(dgm, *pieces, w_blk, w_full, *comm.arrays)
    return res[0], res[1:]


def _dh_final(dh1, pieces, w_full, x2, dy, vec):
    s, d = x2.shape
    tm = min(256, s)
    widths = [p.shape[1] for p in pieces]
    n_p = len(pieces)
    wcols = sum(widths)
    comm = NO_COMM

    def body(*refs):
        ins, outs, scratch, _, _, _ = _split_refs(refs, 5 + n_p, 2, 2, comm)
        dh1_ref = ins[0]
        p_refs = ins[1:1 + n_p]
        w_hbm, x_ref, dy_ref, vec_ref = ins[1 + n_p:]
        gx_ref, acc_ref = outs
        w_vmem, sem = scratch

        @pl.when(pl.program_id(0) == 0)
        def _():
            cp = pltpu.make_async_copy(w_hbm.at[:, pl.ds(w_full.shape[1] - wcols, wcols)], w_vmem, sem)
            cp.start()
            cp.wait()
            acc_ref[...] = jnp.zeros_like(acc_ref)

        scale1 = vec_ref[0:1, :]
        pre_g = vec_ref[2:3, :]
        xf = x_ref[...]
        r = lax.rsqrt(jnp.mean(xf * xf, axis=-1, keepdims=True) + EPS)
        xn = xf * r
        hn = xn * pre_g
        for r0 in range(0, d, 512):
            rows = slice(r0, r0 + 512)
            dh = dh1_ref[:, rows]
            off = 0
            for p_ref, wd in zip(p_refs, widths):
                dh = dh + lax.dot_general(p_ref[...], w_vmem[r0:r0 + 512, off:off + wd], NT_DIMS,
                                          preferred_element_type=F32)
                off += wd
            acc_ref[0:1, rows] += jnp.sum(dh, axis=0, keepdims=True)
            acc_ref[1:2, rows] += jnp.sum(dh * hn[:, rows], axis=0, keepdims=True)
            dhn = dh * scale1[:, rows]
            acc_ref[2:3, rows] += jnp.sum(dhn * xn[:, rows], axis=0, keepdims=True)
            gx_ref[:, rows] = dhn * pre_g[:, rows]
        dxn = gx_ref[...]
        dx = r * (dxn - xn * jnp.mean(dxn * xn, axis=-1, keepdims=True))
        gx_ref[...] = dy_ref[...] + dx

    return pl.pallas_call(
        body, name="dh_final", grid=(s // tm,),
        out_shape=[jax.ShapeDtypeStruct((s, d), F32), jax.ShapeDtypeStruct((8, d), F32)],
        in_specs=[_row_spec(tm, d)] + [_row_spec(tm, wd) for wd in widths]
                 + [ANY_SPEC, _row_spec(tm, d), _row_spec(tm, d), _const_spec((8, d))],
        out_specs=[_row_spec(tm, d), _const_spec((8, d))],
        scratch_shapes=[pltpu.VMEM((d, wcols), BF16), pltpu.SemaphoreType.DMA],
        compiler_params=_params(("arbitrary",)),
    )(dh1, *pieces, w_full, x2, dy, vec)


def _reduce_adamw(parts, w, m, v, name):
    r, c = w.shape
    n_parts = len(parts)
    rp = r // n_parts
    tr = min(256, rp)
    per = rp // tr

    def body(*refs):
        p_refs = refs[:n_parts]
        w_ref, m_ref, v_ref, g_ref, dl_ref, m2_ref, v2_ref = refs[n_parts:]
        i = pl.program_id(0)
        for k, p_ref in enumerate(p_refs):
            @pl.when((i >= k * per) & (i < (k + 1) * per))
            def _(p_ref=p_ref):
                g = p_ref[0].astype(F32)
                for dev in range(1, N_DEV):
                    g = g + p_ref[dev].astype(F32)
                g_ref[...] = g
                dl_ref[...], m2_ref[...], v2_ref[...] = _adamw(w_ref[...], g, m_ref[...], v_ref[...])

    spec = _row_spec(tr, c)
    part_specs = [pl.BlockSpec((N_DEV, tr, c), (lambda i, k=k: (0, jnp.clip(i - k * per, 0, per - 1), 0)))
                  for k in range(n_parts)]
    return pl.pallas_call(
        body, name=name, grid=(r // tr,),
        out_shape=[jax.ShapeDtypeStruct((r, c), F32)] * 4,
        in_specs=part_specs + [spec, spec, spec],
        out_specs=[spec] * 4, compiler_params=_params(("arbitrary",)),
    )(*parts, w, m, v)


def _small_reduce_adamw(gathered, w, m, v):
    rows, c = w.shape

    def body(p_ref, w_ref, m_ref, v_ref, g_ref, dl_ref, m2_ref, v2_ref):
        g = p_ref[0:rows, :]
        for dev in range(1, N_DEV):
            g = g + p_ref[dev * rows:(dev + 1) * rows, :]
        g_ref[...] = g
        dl_ref[...], m2_ref[...], v2_ref[...] = _adamw(w_ref[...], g, m_ref[...], v_ref[...])

    vm = pl.BlockSpec(memory_space=pltpu.VMEM)
    return pl.pallas_call(
        body, name="small_reduce_adamw", out_shape=[jax.ShapeDtypeStruct((rows, c), F32)] * 4,
        in_specs=[vm] * 4, out_specs=[vm] * 4, compiler_params=_params(),
    )(gathered, w, m, v)


def _pack_small(parts, total):
    flat = jnp.concatenate([p.reshape(-1).astype(F32) for p in parts])
    flat = jnp.pad(flat, (0, total - flat.shape[0]))
    return flat.reshape(8, total // 8)


def _unpack_small(packed, shapes):
    flat = packed.reshape(-1)
    out, off = [], 0
    for shp in shapes:
        size = math.prod(shp)
        out.append(flat[off:off + size].reshape(shp))
        off += size
    return out


def kernel(x, c, rel_bias_table, w_ada, b_ada, pre_norm_g, post_norm_g, w_in, attn_sink, w_pool_group, pool_scale, w_branch_attn, w_branch_pool, w_merge, b_merge, w_out, loss_target, m_rel_bias_table, m_w_ada, m_b_ada, m_pre_norm_g, m_post_norm_g, m_w_in, m_attn_sink, m_w_pool_group, m_pool_scale, m_w_branch_attn, m_w_branch_pool, m_w_merge, m_b_merge, m_w_out, v_rel_bias_table, v_w_ada, v_b_ada, v_pre_norm_g, v_post_norm_g, v_w_in, v_attn_sink, v_w_pool_group, v_pool_scale, v_w_branch_attn, v_w_branch_pool, v_w_merge, v_b_merge, v_w_out):
    xi, yi, ci = lax.axis_index("x"), lax.axis_index("y"), lax.axis_index("c")
    me = 4 * xi + 2 * yi + ci
    x2, t2 = x[0], loss_target[0]
    s, d = x2.shape
    pw = d // 2
    n_groups = len(POOL_SIZES)
    cg = pw // n_groups
    in_w = w_in.shape[2] * N_DEV
    ada_cols = w_ada.shape[2]

    c_all = _allgather_small(jnp.broadcast_to(c, (8, d)), "gather_c").reshape(N_DEV, 8, d)[:, 0, :]
    c16 = jnp.concatenate([c_all, jnp.zeros_like(c_all)], axis=0)
    b_cols = lax.dynamic_slice_in_dim(b_ada, me * ada_cols, ada_cols, axis=1)
    mod_part = _ada_fwd(c16, w_ada[0], b_cols)[:8]
    mod_all = _allgather_small(mod_part, "gather_mod").reshape(N_DEV, 8, ada_cols)
    mod = lax.dynamic_index_in_dim(mod_all, me, axis=1, keepdims=False).reshape(3 * d)
    shift, scale, gate = mod[:d], mod[d:2 * d], mod[2 * d:]
    zrow = jnp.zeros((d,), F32)
    vec_in = jnp.stack([1.0 + scale, shift, pre_norm_g[0], zrow, zrow, zrow, zrow, zrow])
    vec_out = jnp.stack([gate, post_norm_g[0], zrow, zrow, zrow, zrow, zrow, zrow])

    rows_per = w_pool_group.shape[2]
    late_shards = [w.astype(BF16) for w in (w_branch_attn[0], w_branch_pool[0], w_out[0],
                                            w_pool_group[0].reshape(n_groups * rows_per, cg))]
    w_in_blk, = _allgather_hbm([w_in[0].astype(BF16)], "gather_w_in")
    w_in_full = w_in_blk.transpose(1, 0, 2).reshape(d, in_w)

    rel = jnp.arange(SPAN)[None, :] - WINDOW - jnp.arange(QBLK)[:, None]
    idx = jnp.where(jnp.abs(rel) <= WINDOW, _t5_bucket(rel), -1).astype(jnp.int32).reshape(1, QBLK * SPAN)
    bias = _bias_variants(_bias_build(rel_bias_table.T, idx).reshape(N_Q_HEADS, QBLK, SPAN))
    sink = attn_sink[0]

    (h, qkv, ga, u, gp), (w_merge_blk,) = _fwd_in(x2, vec_in, w_in_full, _Carried("gather", [w_merge[0].astype(BF16)]))
    g, (w_ba_blk, w_bp_blk, w_out_blk, w_pool_blk) = _merge_fwd(h, w_merge_blk, b_merge, _Carried("gather", late_shards))
    w_pool_full = w_pool_blk.reshape(N_DEV, n_groups, rows_per, cg).transpose(1, 0, 2, 3).reshape(n_groups, cg, cg)
    kvp = jnp.pad(qkv[:, ATTN_WIDTH:], ((WINDOW, WINDOW), (0, 0)))
    a, ya = _attn_fwd(qkv, kvp, ga, bias, sink)
    yp = _pool_fwd(u, gp, w_pool_full, pool_scale)
    ba, bp, merged = _branch_fwd(ya, yp, g, w_ba_blk, w_bp_blk)
    do, dgm, dba, dbp, dy, acc_out, db_merge = _out_fused(merged, x2, t2, g, ba, bp, vec_out, w_out_blk)
    loss = lax.psum(0.5 * jnp.sum(acc_out[0]) / d, MESH_AXES)

    gw_out = _wgrad(merged, do, d, d, "wgrad_out")[0].reshape(N_DEV, d // N_DEV, d)
    gw_ba = _wgrad(ya, dba, d, d // N_DEV, "wgrad_branch_attn")
    gw_bp = _wgrad(yp, dbp, d, d // N_DEV, "wgrad_branch_pool")
    gw_merge, (r_out, r_ba, r_bp) = _wgrad(h, dgm, min(2 * d, 2048), 2 * d // N_DEV, "wgrad_merge",
                                           _Carried("exchange", [gw_out, gw_ba, gw_bp]))
    da, dga, dyp = _branch_bwd(dba, dbp, a, ga, w_ba_blk, w_bp_blk)
    (dq, dkv, dbias, dsink), (r_merge,) = _attn_bwd(qkv, kvp, a, da, bias, sink, _Carried("exchange", [gw_merge]))
    e, dgp, gw_pool, dpool_scale = _pool_bwd_rows(u, gp, dyp, w_pool_full, pool_scale)
    du = _pool_bwd_window(e)
    pieces = [dq, dkv, dga, du, dgp]
    gw_in = jnp.concatenate([_wgrad(h, p, p.shape[1], p.shape[1], "wgrad_in_%d" % i)[0]
                             for i, p in enumerate(pieces)], axis=1)
    gw_in_blk = gw_in.reshape(d, N_DEV, in_w // N_DEV).transpose(1, 0, 2)
    gw_pool_blk = gw_pool.astype(BF16).reshape(n_groups, N_DEV, rows_per, cg).transpose(1, 0, 2, 3).reshape(
        N_DEV, n_groups * rows_per, cg)
    dh1, (r_in, r_pool) = _dh_merge(dgm, pieces[:3], w_merge_blk, w_in_full, _Carried("exchange", [gw_in_blk, gw_pool_blk]))
    gx, acc_in = _dh_final(dh1, pieces[3:], w_in_full, x2, dy, vec_in)
    d_table = _bias_reduce(dbias.reshape(N_Q_HEADS, QBLK * SPAN), idx).T

    recv = [[r_in], [r_merge], [r_ba], [r_bp], [r_out], [r_pool]]
    names = ["w_in", "w_merge", "w_branch_attn", "w_branch_pool", "w_out", "w_pool_group"]
    big_w = [w_in, w_merge, w_branch_attn, w_branch_pool, w_out, w_pool_group]
    big_m = [m_w_in, m_w_merge, m_w_branch_attn, m_w_branch_pool, m_w_out, m_w_pool_group]
    big_v = [v_w_in, v_w_merge, v_w_branch_attn, v_w_branch_pool, v_w_out, v_w_pool_group]
    big = {}
    for nm, rc, w_, m_, v_ in zip(names, recv, big_w, big_m, big_v):
        shp2 = (len(rc) * rc[0].shape[1], rc[0].shape[2])
        outs = _reduce_adamw(rc, w_.reshape(shp2), m_.reshape(shp2), v_.reshape(shp2), "adamw_" + nm)
        big[nm] = [o.reshape(w_.shape) for o in outs]

    dmod = jnp.concatenate([acc_in[0], acc_in[1], acc_out[1]])
    small_names = ["b_ada", "pre_norm_g", "post_norm_g", "pool_scale", "b_merge", "rel_bias_table", "attn_sink"]
    small_g = [dmod, acc_in[2], acc_out[2], dpool_scale, db_merge, d_table, dsink[:, 0]]
    small_w = [b_ada, pre_norm_g, post_norm_g, pool_scale, b_merge, rel_bias_table, attn_sink]
    small_m = [m_b_ada, m_pre_norm_g, m_post_norm_g, m_pool_scale, m_b_merge, m_rel_bias_table, m_attn_sink]
    small_v = [v_b_ada, v_pre_norm_g, v_post_norm_g, v_pool_scale, v_b_merge, v_rel_bias_table, v_attn_sink]
    n_small = sum(math.prod(w_.shape) for w_ in small_w)
    total = -(-n_small // 1024) * 1024
    gathered = _allgather_small(_pack_small(small_g, total), "gather_small_grads")
    sm = _small_reduce_adamw(gathered, _pack_small(small_w, total), _pack_small(small_m, total),
                             _pack_small(small_v, total))
    shapes = [w_.shape for w_ in small_w]
    small = {nm: vals for nm, vals in zip(small_names, zip(*[_unpack_small(o, shapes) for o in sm]))}

    dmod_all = gathered.reshape(N_DEV, -1)[:, :3 * d]
    dmod_mine = lax.dynamic_slice_in_dim(dmod_all, me * ada_cols, ada_cols, axis=1)
    dmod16 = jnp.concatenate([dmod_mine, jnp.zeros_like(dmod_mine)], axis=0)
    ada = [o[None] for o in _ada_bwd(c16, dmod16, w_ada[0], m_w_ada[0], v_w_ada[0])]

    order = ["rel_bias_table", "w_ada", "b_ada", "pre_norm_g", "post_norm_g", "w_in", "attn_sink", "w_pool_group",
             "pool_scale", "w_branch_attn", "w_branch_pool", "w_merge", "b_merge", "w_out"]
    results = dict(big)
    results.update({nm: list(vals) for nm, vals in small.items()})
    results["w_ada"] = ada
    outs = [loss, gx[None]]
    for k in range(4):
        outs += [results[nm][k] for nm in order]
    return tuple(outs)
```

```python
import functools
import math

import jax
import jax.numpy as jnp
from jax import lax
from jax.experimental import pallas as pl
from jax.experimental.pallas import tpu as pltpu

F32 = jnp.float32
BF16 = jnp.bfloat16

HEAD_DIM = 128
N_Q_HEADS = 8
N_KV_HEADS = 2
GQA_GROUP = N_Q_HEADS // N_KV_HEADS
ATTN_WIDTH = N_Q_HEADS * HEAD_DIM
KV_WIDTH = N_KV_HEADS * HEAD_DIM
WINDOW = 128
QBLK = 128
SPAN = QBLK + 2 * WINDOW
N_BUCKETS = 32
MAX_DISTANCE = 128
POOL_SIZES = (2, 4, 8, 16)
EPS = 1e-6
NEG_INF = -1e30
ATTN_SCALE = HEAD_DIM ** -0.5

ADAM_LR = 0.001
ADAM_B1 = 0.9
ADAM_B2 = 0.999
ADAM_EPS = 1e-08
ADAM_WD = 0.01
ADAM_STEP = 10

N_DEV = 8
MESH_AXES = ("x", "y", "c")
MESH = pl.DeviceIdType.MESH
POOL_HALO = 16
V7X_VMEM_LIMIT = 56 * 1024 * 1024

NT_DIMS = (((1,), (1,)), ((), ()))
TN_DIMS = (((0,), (0,)), ((), ()))


def _params(sem=None, vmem=V7X_VMEM_LIMIT):
    return pltpu.CompilerParams(dimension_semantics=sem, vmem_limit_bytes=vmem)


def _silu(v):
    return v * jax.nn.sigmoid(v)


def _dsilu(v):
    s = jax.nn.sigmoid(v)
    return s * (1.0 + v * (1.0 - s))


def _row_spec(tm, width):
    return pl.BlockSpec((tm, width), lambda i: (i, 0))


def _const_spec(shape):
    return pl.BlockSpec(shape, lambda *_: (0,) * len(shape))


ANY_SPEC = pl.BlockSpec(memory_space=pl.ANY)


def _load_blocked(w_blk, w_vmem, sems, axis):
    n = w_blk.shape[0]
    width = w_blk.shape[1 + axis]
    copies = []
    for d in range(n):
        dst = w_vmem.at[pl.ds(d * width, width), :] if axis == 0 else w_vmem.at[:, pl.ds(d * width, width)]
        copies.append(pltpu.make_async_copy(w_blk.at[d], dst, sems.at[d]))
    for cp in copies:
        cp.start()
    for cp in copies:
        cp.wait()


def _mesh_pos():
    return lax.axis_index("x"), lax.axis_index("y"), lax.axis_index("c")


def _allgather_small(blk, name):
    m_per, n = blk.shape

    def body(x_ref, out_ref, send_sems, recv_sems, local_sem):
        x, y, c = _mesh_pos()
        me, sibling = (x, y, c), (x, y, 1 - c)
        chips = [(1 - x, y), (x, 1 - y), (1 - x, 1 - y)]

        def rows(px, py, pc):
            return out_ref.at[pl.ds((4 * px + 2 * py + pc) * m_per, m_per), :]

        def copy(k, block, to, src=None):
            return pltpu.make_async_remote_copy(
                src_ref=rows(*block) if src is None else src, dst_ref=rows(*block),
                send_sem=send_sems.at[k], recv_sem=recv_sems.at[k], device_id=to, device_id_type=MESH)

        mine = pltpu.make_async_copy(x_ref, rows(*me), local_sem)
        mine.start()
        first = [copy(0, me, sibling, src=x_ref)]
        first += [copy(1 + j, me, (*chip, c), src=x_ref) for j, chip in enumerate(chips)]
        for cp in first:
            cp.start()
        passed = [copy(4 + j, (*chip, c), sibling) for j, chip in enumerate(chips)]
        for j, chip in enumerate(chips):
            copy(1 + j, (*chip, c), me).wait_recv()
            passed[j].start()
        copy(0, sibling, me).wait_recv()
        for j, chip in enumerate(chips):
            copy(4 + j, (*chip, 1 - c), me).wait_recv()
        for cp in first + passed:
            cp.wait_send()
        mine.wait()

    return pl.pallas_call(
        body, name=name,
        out_shape=jax.ShapeDtypeStruct((N_DEV * m_per, n), blk.dtype),
        in_specs=[pl.BlockSpec(memory_space=pltpu.VMEM)],
        out_specs=pl.BlockSpec(memory_space=pltpu.VMEM),
        scratch_shapes=[pltpu.SemaphoreType.DMA((7,)), pltpu.SemaphoreType.DMA((7,)), pltpu.SemaphoreType.DMA],
    )(blk)


def _gather_copy(outs, send_sems, recv_sems, i, k, block, to, src=None):
    slot = outs[i].at[4 * block[0] + 2 * block[1] + block[2]]
    return pltpu.make_async_remote_copy(
        src_ref=slot if src is None else src, dst_ref=slot,
        send_sem=send_sems.at[7 * i + k], recv_sem=recv_sems.at[7 * i + k], device_id=to, device_id_type=MESH)


def _gather_places():
    x, y, c = _mesh_pos()
    return (x, y, c), (x, y, 1 - c), [(1 - x, y), (x, 1 - y), (1 - x, 1 - y)]


def _gather_start(ins, outs, send_sems, recv_sems, local_sems):
    me, sibling, chips = _gather_places()
    for i, src in enumerate(ins):
        pltpu.make_async_copy(src, outs[i].at[4 * me[0] + 2 * me[1] + me[2]], local_sems.at[i]).start()
        _gather_copy(outs, send_sems, recv_sems, i, 0, me, sibling, src=src).start()
        for j, chip in enumerate(chips):
            _gather_copy(outs, send_sems, recv_sems, i, 1 + j, me, (*chip, me[2]), src=src).start()


def _gather_forward(ins, outs, send_sems, recv_sems):
    me, sibling, chips = _gather_places()
    for i in range(len(ins)):
        for j, chip in enumerate(chips):
            _gather_copy(outs, send_sems, recv_sems, i, 1 + j, (*chip, me[2]), me).wait_recv()
            _gather_copy(outs, send_sems, recv_sems, i, 4 + j, (*chip, me[2]), sibling).start()


def _gather_finish(ins, outs, send_sems, recv_sems, local_sems):
    me, sibling, chips = _gather_places()
    for i, src in enumerate(ins):
        _gather_copy(outs, send_sems, recv_sems, i, 0, sibling, me).wait_recv()
        for j, chip in enumerate(chips):
            _gather_copy(outs, send_sems, recv_sems, i, 4 + j, (*chip, 1 - me[2]), me).wait_recv()
        _gather_copy(outs, send_sems, recv_sems, i, 0, me, sibling, src=src).wait_send()
        for j, chip in enumerate(chips):
            _gather_copy(outs, send_sems, recv_sems, i, 1 + j, me, (*chip, me[2]), src=src).wait_send()
            _gather_copy(outs, send_sems, recv_sems, i, 4 + j, (*chip, me[2]), sibling).wait_send()
        pltpu.make_async_copy(src, outs[i].at[4 * me[0] + 2 * me[1] + me[2]], local_sems.at[i]).wait()


EXCHANGE_FLIPS = [(fx, fy, fc) for fx in (0, 1) for fy in (0, 1) for fc in (0, 1)][1:]


def _exchange_copy(ins, outs, send_sems, recv_sems, i, k, arriving):
    x, y, c = _mesh_pos()
    f = EXCHANGE_FLIPS[k]
    p = (1 - x if f[0] else x, 1 - y if f[1] else y, 1 - c if f[2] else c)
    p_idx = 4 * p[0] + 2 * p[1] + p[2]
    return pltpu.make_async_remote_copy(
        src_ref=ins[i].at[p_idx], dst_ref=outs[i].at[p_idx if arriving else 4 * x + 2 * y + c],
        send_sem=send_sems.at[7 * i + k], recv_sem=recv_sems.at[7 * i + k], device_id=p, device_id_type=MESH)


def _exchange_start(ins, outs, send_sems, recv_sems, local_sems):
    x, y, c = _mesh_pos()
    me_idx = 4 * x + 2 * y + c
    for i in range(len(ins)):
        pltpu.make_async_copy(ins[i].at[me_idx], outs[i].at[me_idx], local_sems.at[i]).start()
        for k in range(7):
            _exchange_copy(ins, outs, send_sems, recv_sems, i, k, False).start()


def _exchange_finish(ins, outs, send_sems, recv_sems, local_sems):
    x, y, c = _mesh_pos()
    me_idx = 4 * x + 2 * y + c
    for i in range(len(ins)):
        for k in range(7):
            _exchange_copy(ins, outs, send_sems, recv_sems, i, k, True).wait_recv()
            _exchange_copy(ins, outs, send_sems, recv_sems, i, k, False).wait_send()
        pltpu.make_async_copy(ins[i].at[me_idx], outs[i].at[me_idx], local_sems.at[i]).wait()


class _Carried:
    def __init__(self, kind, arrays):
        self.kind = kind
        self.arrays = list(arrays)
        self.n = len(self.arrays)

    def specs(self):
        return [ANY_SPEC] * self.n

    def out_shapes(self):
        lead = (N_DEV,) if self.kind == "gather" else ()
        return [jax.ShapeDtypeStruct(lead + a.shape, a.dtype) for a in self.arrays]

    def scratch(self):
        return [pltpu.SemaphoreType.DMA((7 * self.n,)), pltpu.SemaphoreType.DMA((7 * self.n,)),
                pltpu.SemaphoreType.DMA((self.n,))]

    def run(self, step, n_steps, ins, outs, sems):
        send_sems, recv_sems, local_sems = sems
        if self.n == 0:
            return

        @pl.when(step == 0)
        def _():
            if self.kind == "gather":
                _gather_start(ins, outs, send_sems, recv_sems, local_sems)
            else:
                _exchange_start(ins, outs, send_sems, recv_sems, local_sems)

        if self.kind == "gather":
            @pl.when(step == n_steps // 2)
            def _():
                _gather_forward(ins, outs, send_sems, recv_sems)

        @pl.when(step == n_steps - 1)
        def _():
            if self.kind == "gather":
                _gather_finish(ins, outs, send_sems, recv_sems, local_sems)
            else:
                _exchange_finish(ins, outs, send_sems, recv_sems, local_sems)


NO_COMM = _Carried("exchange", [])


def _split_refs(refs, n_in, n_out, n_scratch, comm):
    a = n_in
    b = a + comm.n
    c = b + n_out
    d = c + comm.n
    e = d + n_scratch
    return refs[:a], refs[b:c], refs[d:e], refs[a:b], refs[c:d], refs[e:]


def _allgather_hbm(shards, name):
    n_arr = len(shards)

    def body(*refs):
        ins, outs = refs[:n_arr], refs[n_arr:2 * n_arr]
        send_sems, recv_sems, local_sems = refs[2 * n_arr:]
        _gather_start(ins, outs, send_sems, recv_sems, local_sems)
        _gather_forward(ins, outs, send_sems, recv_sems)
        _gather_finish(ins, outs, send_sems, recv_sems, local_sems)

    return pl.pallas_call(
        body, name=name,
        out_shape=[jax.ShapeDtypeStruct((N_DEV,) + s.shape, s.dtype) for s in shards],
        in_specs=[ANY_SPEC] * n_arr,
        out_specs=[ANY_SPEC] * n_arr,
        scratch_shapes=[pltpu.SemaphoreType.DMA((7 * n_arr,)), pltpu.SemaphoreType.DMA((7 * n_arr,)),
                        pltpu.SemaphoreType.DMA((n_arr,))],
    )(*shards)


def _ada_fwd(c16, w_shard, b_cols):
    rows, d = c16.shape
    nc = w_shard.shape[1]

    def body(c_ref, w_ref, b_ref, o_ref):
        sc = _silu(c_ref[...]).astype(BF16)
        o_ref[...] = jnp.dot(sc, w_ref[...].astype(BF16), preferred_element_type=F32) + b_ref[...]

    return pl.pallas_call(
        body, name="ada_fwd", out_shape=jax.ShapeDtypeStruct((rows, nc), F32),
        in_specs=[pl.BlockSpec(memory_space=pltpu.VMEM)] * 3, out_specs=pl.BlockSpec(memory_space=pltpu.VMEM),
        compiler_params=_params(),
    )(c16, w_shard, b_cols)


def _adamw(w, g, m, v):
    m2 = ADAM_B1 * m + (1.0 - ADAM_B1) * g
    v2 = ADAM_B2 * v + (1.0 - ADAM_B2) * (g * g)
    m_hat = m2 * (1.0 / (1.0 - ADAM_B1 ** ADAM_STEP))
    v_hat = v2 * (1.0 / (1.0 - ADAM_B2 ** ADAM_STEP))
    delta = -ADAM_LR * (m_hat / (jnp.sqrt(v_hat) + ADAM_EPS) + ADAM_WD * w)
    return delta, m2, v2


def _ada_bwd(c16, dmod16, w, m, v):
    rows, d = c16.shape
    nc = w.shape[1]
    tr = min(512, d)

    def body(c_ref, dm_ref, w_ref, m_ref, v_ref, g_ref, dl_ref, m2_ref, v2_ref):
        sc = _silu(c_ref[...]).astype(BF16)
        g = lax.dot_general(sc, dm_ref[...].astype(BF16), TN_DIMS, preferred_element_type=F32)
        g_ref[...] = g
        dl_ref[...], m2_ref[...], v2_ref[...] = _adamw(w_ref[...], g, m_ref[...], v_ref[...])

    spec = _row_spec(tr, nc)
    return pl.pallas_call(
        body, name="ada_bwd", grid=(d // tr,),
        out_shape=[jax.ShapeDtypeStruct((d, nc), F32)] * 4,
        in_specs=[pl.BlockSpec((rows, tr), lambda i: (0, i)), _const_spec((rows, nc)), spec, spec, spec],
        out_specs=[spec] * 4, compiler_params=_params(("arbitrary",)),
    )(c16, dmod16, w, m, v)


def _split3(v):
    hi = v.astype(BF16)
    r1 = v - hi.astype(F32)
    mid = r1.astype(BF16)
    lo = (r1 - mid.astype(F32)).astype(BF16)
    return hi, mid, lo


BIAS_CHUNK = QBLK * SPAN // 8


def _bias_build(table_t, idx):
    n = idx.shape[1]

    def body(t_ref, i_ref, o_ref):
        ids = i_ref[...]
        onehot = (lax.broadcasted_iota(jnp.int32, (N_BUCKETS, BIAS_CHUNK), 0) == ids).astype(BF16)
        t16 = jnp.concatenate([t_ref[...], jnp.zeros_like(t_ref[...])], axis=0)
        acc = jnp.zeros((2 * N_Q_HEADS, BIAS_CHUNK), F32)
        for part in _split3(t16):
            acc = acc + jnp.dot(part, onehot, preferred_element_type=F32)
        o_ref[...] = jnp.where(ids >= 0, acc[:N_Q_HEADS], NEG_INF)

    return pl.pallas_call(
        body, name="bias_build", grid=(n // BIAS_CHUNK,),
        out_shape=jax.ShapeDtypeStruct((N_Q_HEADS, n), F32),
        in_specs=[_const_spec((N_Q_HEADS, N_BUCKETS)), pl.BlockSpec((1, BIAS_CHUNK), lambda i: (0, i))],
        out_specs=pl.BlockSpec((N_Q_HEADS, BIAS_CHUNK), lambda i: (0, i)),
        compiler_params=_params(("arbitrary",)),
    )(table_t, idx)


def _bias_reduce(dbias, idx):
    n = idx.shape[1]

    def body(d_ref, i_ref, o_ref):
        @pl.when(pl.program_id(0) == 0)
        def _():
            o_ref[...] = jnp.zeros_like(o_ref)

        onehot = (lax.broadcasted_iota(jnp.int32, (N_BUCKETS, BIAS_CHUNK), 0) == i_ref[...]).astype(BF16)
        d16 = jnp.concatenate([d_ref[...], jnp.zeros_like(d_ref[...])], axis=0)
        acc = jnp.zeros((2 * N_Q_HEADS, N_BUCKETS), F32)
        for part in _split3(d16):
            acc = acc + lax.dot_general(part, onehot, NT_DIMS, preferred_element_type=F32)
        o_ref[...] += acc[:N_Q_HEADS]

    return pl.pallas_call(
        body, name="bias_reduce", grid=(n // BIAS_CHUNK,),
        out_shape=jax.ShapeDtypeStruct((N_Q_HEADS, N_BUCKETS), F32),
        in_specs=[pl.BlockSpec((N_Q_HEADS, BIAS_CHUNK), lambda i: (0, i)), pl.BlockSpec((1, BIAS_CHUNK), lambda i: (0, i))],
        out_specs=_const_spec((N_Q_HEADS, N_BUCKETS)),
        compiler_params=_params(("arbitrary",)),
    )(dbias, idx)


def _t5_bucket(rel):
    half = N_BUCKETS // 2
    max_exact = half // 2
    ret = jnp.where(rel > 0, half, 0)
    n = jnp.abs(rel)
    nf = jnp.maximum(n, 1).astype(jnp.float32)
    large = max_exact + (jnp.log(nf / max_exact) / math.log(MAX_DISTANCE / max_exact)
                         * (half - max_exact)).astype(jnp.int32)
    large = jnp.minimum(large, half - 1)
    return ret + jnp.where(n < max_exact, n, large)


def _fwd_in(x2, vec, w_full, comm):
    s, d = x2.shape
    pw = d // 2
    tm = min(512, s)
    nt = s // tm
    widths = (ATTN_WIDTH + 2 * KV_WIDTH, ATTN_WIDTH, pw, pw)
    dtypes = (BF16, BF16, F32, BF16)

    def body(*refs):
        ins, outs, scratch, c_in, c_out, c_sems = _split_refs(refs, 3, 5, 2, comm)
        x_ref, vec_ref, w_hbm = ins
        h_ref, qkv_ref, ga_ref, u_ref, gp_ref = outs
        w_vmem, sem = scratch
        comm.run(pl.program_id(0), nt, c_in, c_out, c_sems)

        @pl.when(pl.program_id(0) == 0)
        def _():
            cp = pltpu.make_async_copy(w_hbm, w_vmem, sem)
            cp.start()
            cp.wait()

        xf = x_ref[...]
        r = lax.rsqrt(jnp.mean(xf * xf, axis=-1, keepdims=True) + EPS)
        h = ((xf * r) * vec_ref[2:3, :]) * vec_ref[0:1, :] + vec_ref[1:2, :]
        hb = h.astype(BF16)
        h_ref[...] = hb
        off = 0
        for o_ref, wd in zip((qkv_ref, ga_ref, u_ref, gp_ref), widths):
            for c0 in range(0, wd, 512):
                c1 = min(c0 + 512, wd)
                o_ref[:, c0:c1] = jnp.dot(hb, w_vmem[:, off + c0:off + c1],
                                          preferred_element_type=F32).astype(o_ref.dtype)
            off += wd

    res = pl.pallas_call(
        body, name="fwd_in", grid=(nt,),
        out_shape=[jax.ShapeDtypeStruct((s, d), BF16)] + [jax.ShapeDtypeStruct((s, wd), dt) for wd, dt in zip(widths, dtypes)]
                  + comm.out_shapes(),
        in_specs=[_row_spec(tm, d), _const_spec((8, d)), ANY_SPEC] + comm.specs(),
        out_specs=[_row_spec(tm, d)] + [_row_spec(tm, wd) for wd in widths] + comm.specs(),
        scratch_shapes=[pltpu.VMEM(w_full.shape, BF16), pltpu.SemaphoreType.DMA] + comm.scratch(),
        compiler_params=_params(("arbitrary",)),
    )(x2, vec, w_full, *comm.arrays)
    return res[:5], res[5:]


def _merge_fwd(h, w_blk, b_merge, comm):
    s, d = h.shape
    n = w_blk.shape[0] * w_blk.shape[2]
    tm = min(512, s)
    nt = s // tm

    def body(*refs):
        ins, outs, scratch, c_in, c_out, c_sems = _split_refs(refs, 3, 1, 2, comm)
        h_ref, w_hbm, b_ref = ins
        g_ref, = outs
        w_vmem, sems = scratch
        comm.run(pl.program_id(0), nt, c_in, c_out, c_sems)

        @pl.when(pl.program_id(0) == 0)
        def _():
            _load_blocked(w_hbm, w_vmem, sems, 1)

        hb = h_ref[...]
        for c0 in range(0, n, 512):
            z = jnp.dot(hb, w_vmem[:, c0:c0 + 512], preferred_element_type=F32) + b_ref[:, c0:c0 + 512]
            g_ref[:, c0:c0 + 512] = jax.nn.sigmoid(z).astype(BF16)

    res = pl.pallas_call(
        body, name="merge_fwd", grid=(nt,),
        out_shape=[jax.ShapeDtypeStruct((s, n), BF16)] + comm.out_shapes(),
        in_specs=[_row_spec(tm, d), ANY_SPEC, _const_spec((1, n))] + comm.specs(),
        out_specs=[_row_spec(tm, n)] + comm.specs(),
        scratch_shapes=[pltpu.VMEM((d, n), BF16), pltpu.SemaphoreType.DMA((N_DEV,))] + comm.scratch(),
        compiler_params=_params(("arbitrary",)),
    )(h, w_blk, b_merge, *comm.arrays)
    return res[0], res[1:]


def _bias_variants(bias):
    t = lax.broadcasted_iota(jnp.int32, (1, 1, SPAN), 2)
    return jnp.stack([jnp.where(t < WINDOW, NEG_INF, bias), bias, jnp.where(t >= WINDOW + QBLK, NEG_INF, bias)])


def _bias_spec(nblk):
    def index(n):
        return (jnp.where(n == 0, 0, jnp.where(n >= nblk - 1, 2, 1)), 0, 0, 0)
    return pl.BlockSpec((1, N_Q_HEADS, QBLK, SPAN), index)


def _attn_head(q_ref, kv, bias_ref, sink_ref, h):
    k = h // GQA_GROUP
    qh = q_ref[:, h * HEAD_DIM:(h + 1) * HEAD_DIM]
    kk = kv[:, k * HEAD_DIM:(k + 1) * HEAD_DIM]
    logits = lax.dot_general(qh, kk, NT_DIMS, preferred_element_type=F32) * ATTN_SCALE + bias_ref[0, h]
    sink = sink_ref[h]
    mx = jnp.maximum(jnp.max(logits, axis=-1, keepdims=True), sink)
    e = jnp.exp(logits - mx)
    e_sink = jnp.exp(sink - mx)
    inv = 1.0 / (jnp.sum(e, axis=-1, keepdims=True) + e_sink)
    return qh, e, inv, e_sink


def _attn_fwd(qkv, kvp, ga, bias3, sink):
    s = qkv.shape[0]
    nblk = s // QBLK
    kv_specs = [pl.BlockSpec((QBLK, 2 * KV_WIDTH), (lambda n, j=j: (n + j, 0))) for j in range(3)]

    def body(sink_ref, q_ref, k0, k1, k2, ga_ref, bias_ref, a_ref, ya_ref):
        kv = jnp.concatenate([k0[...], k1[...], k2[...]], axis=0)
        for h in range(N_Q_HEADS):
            k = h // GQA_GROUP
            cols = slice(h * HEAD_DIM, (h + 1) * HEAD_DIM)
            vk = kv[:, KV_WIDTH + k * HEAD_DIM:KV_WIDTH + (k + 1) * HEAD_DIM]
            _, e, inv, _ = _attn_head(q_ref, kv, bias_ref, sink_ref, h)
            o = jnp.dot(e.astype(BF16), vk, preferred_element_type=F32) * inv
            a_ref[:, cols] = o.astype(BF16)
            ya_ref[:, cols] = (o * _silu(ga_ref[:, cols].astype(F32))).astype(BF16)

    return pl.pallas_call(
        body, name="attn_fwd", grid=(nblk,),
        out_shape=[jax.ShapeDtypeStruct((s, ATTN_WIDTH), BF16)] * 2,
        in_specs=[pl.BlockSpec(memory_space=pltpu.SMEM), _row_spec(QBLK, ATTN_WIDTH)] + kv_specs
                 + [_row_spec(QBLK, ATTN_WIDTH), _bias_spec(nblk)],
        out_specs=[_row_spec(QBLK, ATTN_WIDTH)] * 2,
        compiler_params=_params(("arbitrary",)),
    )(sink, qkv, kvp, kvp, kvp, ga, bias3)


def _roll_rows(a, k):
    return pltpu.roll(a, k % a.shape[0], axis=0)


def _window_sum(xe, w, mirrored):
    acc = xe + _roll_rows(xe, -1 if mirrored else 1)
    width = 2
    while width < w:
        half = width // 2
        acc = _roll_rows(acc, half) + _roll_rows(acc, -half)
        width *= 2
    return acc


def _halo_specs(tm, width, s):
    nh = tm // POOL_HALO
    last = s // POOL_HALO - 1
    prev = pl.BlockSpec((POOL_HALO, width), lambda i: (jnp.maximum(i * nh - 1, 0), 0))
    nxt = pl.BlockSpec((POOL_HALO, width), lambda i: (jnp.minimum((i + 1) * nh, last), 0))
    return [prev, _row_spec(tm, width), nxt]


def _with_halo(prev_ref, cur_ref, next_ref, i, n_tiles):
    prev = jnp.where(i > 0, prev_ref[...].astype(F32), 0.0)
    nxt = jnp.where(i < n_tiles - 1, next_ref[...].astype(F32), 0.0)
    return jnp.concatenate([prev, cur_ref[...].astype(F32), nxt], axis=0)


def _pool_counts(i, tm, s, w):
    pos = i * tm + lax.broadcasted_iota(jnp.int32, (tm, 1), 0)
    lo = jnp.maximum(pos - w // 2, 0)
    hi = jnp.minimum(pos + w // 2, s)
    return (hi - lo).astype(F32)


def _pool_fwd(u, gp, wp, scale):
    s, pw = u.shape
    c = pw // len(POOL_SIZES)
    tm = min(512, s)
    nt = s // tm

    def body(up, uc, un, gp_ref, wp_ref, sc_ref, yp_ref):
        i = pl.program_id(0)
        ue = _with_halo(up, uc, un, i, nt)
        for g, w in enumerate(POOL_SIZES):
            cols = slice(g * c, (g + 1) * c)
            xg = ue[:, cols]
            ws = _window_sum(xg, w, False)[POOL_HALO:POOL_HALO + tm]
            pooled = ws * (1.0 / _pool_counts(i, tm, s, w)) - xg[POOL_HALO:POOL_HALO + tm]
            mixed = jnp.dot(pooled.astype(BF16), wp_ref[g], preferred_element_type=F32)
            yp_ref[:, cols] = (mixed * sc_ref[:, cols] * _silu(gp_ref[:, cols].astype(F32))).astype(BF16)

    return pl.pallas_call(
        body, name="pool_fwd", grid=(nt,),
        out_shape=jax.ShapeDtypeStruct((s, pw), BF16),
        in_specs=_halo_specs(tm, pw, s) + [_row_spec(tm, pw), _const_spec(wp.shape), _const_spec((1, pw))],
        out_specs=_row_spec(tm, pw),
        compiler_params=_params(("arbitrary",)),
    )(u, u, u, gp, wp, scale)


def _branch_fwd(ya, yp, g, wa_blk, wp_blk):
    s = ya.shape[0]
    d = g.shape[1] // 2
    ka, kp = ya.shape[1], yp.shape[1]
    tm = min(512, s)

    def body(ya_ref, yp_ref, g_ref, wa_hbm, wp_hbm, ba_ref, bp_ref, m_ref, wa, wp, sems):
        @pl.when(pl.program_id(0) == 0)
        def _():
            _load_blocked(wa_hbm, wa, sems.at[pl.ds(0, N_DEV)], 1)
            _load_blocked(wp_hbm, wp, sems.at[pl.ds(N_DEV, N_DEV)], 1)

        for c0 in range(0, d, 512):
            cols = slice(c0, c0 + 512)
            ba = jnp.dot(ya_ref[...], wa[:, cols], preferred_element_type=F32)
            bp = jnp.dot(yp_ref[...], wp[:, cols], preferred_element_type=F32)
            ba_ref[:, cols] = ba.astype(BF16)
            bp_ref[:, cols] = bp.astype(BF16)
            ga = g_ref[:, c0:c0 + 512].astype(F32)
            gb = g_ref[:, d + c0:d + c0 + 512].astype(F32)
            m_ref[:, cols] = (ga * ba + gb * bp).astype(BF16)

    return pl.pallas_call(
        body, name="branch_fwd", grid=(s // tm,),
        out_shape=[jax.ShapeDtypeStruct((s, d), BF16)] * 3,
        in_specs=[_row_spec(tm, ka), _row_spec(tm, kp), _row_spec(tm, 2 * d), ANY_SPEC, ANY_SPEC],
        out_specs=[_row_spec(tm, d)] * 3,
        scratch_shapes=[pltpu.VMEM((ka, d), BF16), pltpu.VMEM((kp, d), BF16), pltpu.SemaphoreType.DMA((2 * N_DEV,))],
        compiler_params=_params(("arbitrary",)),
    )(ya, yp, g, wa_blk, wp_blk)


def _out_fused(merged, x2, t2, g, ba, bp, vec, wo_blk):
    s, d = x2.shape
    tm = min(256, s)

    def body(m_ref, x_ref, t_ref, g_ref, ba_ref, bp_ref, vec_ref, wo_hbm,
             do_ref, dgm_ref, dba_ref, dbp_ref, dy_ref, acc_ref, dbm_ref, wo, sems):
        @pl.when(pl.program_id(0) == 0)
        def _():
            _load_blocked(wo_hbm, wo, sems, 0)
            acc_ref[...] = jnp.zeros_like(acc_ref)
            dbm_ref[...] = jnp.zeros_like(dbm_ref)

        gate = vec_ref[0:1, :]
        post_g = vec_ref[1:2, :]
        o = jnp.dot(m_ref[...], wo[...], preferred_element_type=F32)
        r2 = lax.rsqrt(jnp.mean(o * o, axis=-1, keepdims=True) + EPS)
        on = o * r2
        z = on * post_g
        diff = x_ref[...] + gate * z - t_ref[...]
        dy = diff * (1.0 / d)
        dy_ref[...] = dy
        dz = dy * gate
        don = dz * post_g
        do = r2 * (don - on * jnp.mean(don * on, axis=-1, keepdims=True))
        acc_ref[0:1, :] += jnp.sum(diff * diff, axis=0, keepdims=True)
        acc_ref[1:2, :] += jnp.sum(dy * z, axis=0, keepdims=True)
        acc_ref[2:3, :] += jnp.sum(dz * on, axis=0, keepdims=True)
        dob = do.astype(BF16)
        do_ref[...] = dob
        for c0 in range(0, d, 512):
            cols = slice(c0, c0 + 512)
            dm = lax.dot_general(dob, wo[c0:c0 + 512, :], NT_DIMS, preferred_element_type=F32)
            ga = g_ref[:, c0:c0 + 512].astype(F32)
            gb = g_ref[:, d + c0:d + c0 + 512].astype(F32)
            b_a = ba_ref[:, cols].astype(F32)
            b_p = bp_ref[:, cols].astype(F32)
            dba_ref[:, cols] = (dm * ga).astype(BF16)
            dbp_ref[:, cols] = (dm * gb).astype(BF16)
            dga = dm * b_a * ga * (1.0 - ga)
            dgb = dm * b_p * gb * (1.0 - gb)
            dgm_ref[:, c0:c0 + 512] = dga.astype(BF16)
            dgm_ref[:, d + c0:d + c0 + 512] = dgb.astype(BF16)
            dbm_ref[:, c0:c0 + 512] += jnp.sum(dga, axis=0, keepdims=True)
            dbm_ref[:, d + c0:d + c0 + 512] += jnp.sum(dgb, axis=0, keepdims=True)

    return pl.pallas_call(
        body, name="out_fused", grid=(s // tm,),
        out_shape=[jax.ShapeDtypeStruct((s, d), BF16), jax.ShapeDtypeStruct((s, 2 * d), BF16),
                   jax.ShapeDtypeStruct((s, d), BF16), jax.ShapeDtypeStruct((s, d), BF16),
                   jax.ShapeDtypeStruct((s, d), F32), jax.ShapeDtypeStruct((8, d), F32),
                   jax.ShapeDtypeStruct((1, 2 * d), F32)],
        in_specs=[_row_spec(tm, d), _row_spec(tm, d), _row_spec(tm, d), _row_spec(tm, 2 * d),
                  _row_spec(tm, d), _row_spec(tm, d), _const_spec((8, d)), ANY_SPEC],
        out_specs=[_row_spec(tm, d), _row_spec(tm, 2 * d), _row_spec(tm, d), _row_spec(tm, d),
                   _row_spec(tm, d), _const_spec((8, d)), _const_spec((1, 2 * d))],
        scratch_shapes=[pltpu.VMEM((d, d), BF16), pltpu.SemaphoreType.DMA((N_DEV,))],
        compiler_params=_params(("arbitrary",)),
    )(merged, x2, t2, g, ba, bp, vec, wo_blk)


def _wgrad(a, b, tn, shard_w, name, comm=NO_COMM):
    s, k = a.shape
    n = b.shape[1]
    tk = min(512, s)
    ns = s // tk
    per = tn // shard_w

    def body(*refs):
        ins, outs, scratch, c_in, c_out, c_sems = _split_refs(refs, 2, 1, 1, comm)
        a_ref, b_ref = ins
        o_ref, = outs
        acc, = scratch
        step = pl.program_id(1)
        comm.run(pl.program_id(0) * ns + step, (n // tn) * ns, c_in, c_out, c_sems)

        @pl.when(step == 0)
        def _():
            acc[...] = jnp.zeros_like(acc)

        acc[...] += lax.dot_general(a_ref[...], b_ref[...], TN_DIMS, preferred_element_type=F32)

        @pl.when(step == ns - 1)
        def _():
            for r in range(per):
                o_ref[r] = acc[:, r * shard_w:(r + 1) * shard_w].astype(BF16)

    res = pl.pallas_call(
        body, name=name, grid=(n // tn, ns),
        out_shape=[jax.ShapeDtypeStruct((n // shard_w, k, shard_w), BF16)] + comm.out_shapes(),
        in_specs=[pl.BlockSpec((tk, k), lambda j, t: (t, 0)), pl.BlockSpec((tk, tn), lambda j, t: (t, j))] + comm.specs(),
        out_specs=[pl.BlockSpec((per, k, shard_w), lambda j, t: (j, 0, 0))] + comm.specs(),
        scratch_shapes=[pltpu.VMEM((k, tn), F32)] + comm.scratch(),
        compiler_params=_params(("arbitrary", "arbitrary")),
    )(a, b, *comm.arrays)
    return res[0] if comm.n == 0 else (res[0], res[1:])


def _branch_bwd(dba, dbp, a, ga, wa_blk, wp_blk, comm):
    s, d = dba.shape
    ka = wa_blk.shape[1]
    kp = wp_blk.shape[1]
    tm = min(512, s)
    nt = s // tm

    def body(*refs):
        ins, outs, scratch, c_in, c_out, c_sems = _split_refs(refs, 6, 3, 3, comm)
        dba_ref, dbp_ref, a_ref, ga_ref, wa_hbm, wp_hbm = ins
        da_ref, dga_ref, dyp_ref = outs
        wa, wp, sems = scratch
        comm.run(pl.program_id(0), nt, c_in, c_out, c_sems)

        @pl.when(pl.program_id(0) == 0)
        def _():
            _load_blocked(wa_hbm, wa, sems.at[pl.ds(0, N_DEV)], 1)
            _load_blocked(wp_hbm, wp, sems.at[pl.ds(N_DEV, N_DEV)], 1)

        for c0 in range(0, ka, 512):
            cols = slice(c0, c0 + 512)
            dya = lax.dot_general(dba_ref[...], wa[c0:c0 + 512, :], NT_DIMS, preferred_element_type=F32)
            gaf = ga_ref[:, cols].astype(F32)
            da_ref[:, cols] = (dya * _silu(gaf)).astype(BF16)
            dga_ref[:, cols] = (dya * a_ref[:, cols].astype(F32) * _dsilu(gaf)).astype(BF16)
        for c0 in range(0, kp, 512):
            dyp = lax.dot_general(dbp_ref[...], wp[c0:c0 + 512, :], NT_DIMS, preferred_element_type=F32)
            dyp_ref[:, c0:c0 + 512] = dyp.astype(BF16)

    res = pl.pallas_call(
        body, name="branch_bwd", grid=(nt,),
        out_shape=[jax.ShapeDtypeStruct((s, ka), BF16), jax.ShapeDtypeStruct((s, ka), BF16),
                   jax.ShapeDtypeStruct((s, kp), BF16)] + comm.out_shapes(),
        in_specs=[_row_spec(tm, d), _row_spec(tm, d), _row_spec(tm, ka), _row_spec(tm, ka), ANY_SPEC, ANY_SPEC]
                 + comm.specs(),
        out_specs=[_row_spec(tm, ka), _row_spec(tm, ka), _row_spec(tm, kp)] + comm.specs(),
        scratch_shapes=[pltpu.VMEM((ka, d), BF16), pltpu.VMEM((kp, d), BF16), pltpu.SemaphoreType.DMA((2 * N_DEV,))]
                       + comm.scratch(),
        compiler_params=_params(("arbitrary",)),
    )(dba, dbp, a, ga, wa_blk, wp_blk, *comm.arrays)
    return res[:3], res[3:]


def _attn_bwd(qkv, kvp, a, da, bias3, sink, comm):
    s = qkv.shape[0]
    nblk = s // QBLK
    clamp = lambda n: jnp.minimum(n, nblk - 1)
    q_spec = pl.BlockSpec((QBLK, ATTN_WIDTH), lambda n: (clamp(n), 0))
    kv_specs = [pl.BlockSpec((QBLK, 2 * KV_WIDTH), (lambda n, j=j: (clamp(n) + j, 0))) for j in range(3)]

    def body(*refs):
        ins, outs, scratch, c_in, c_out, c_sems = _split_refs(refs, 8, 4, 2, comm)
        sink_ref, q_ref, k0, k1, k2, a_ref, da_ref, bias_ref = ins
        dq_ref, dkv_ref, db_ref, ds_ref = outs
        acc, sacc = scratch
        n = pl.program_id(0)
        comm.run(n, nblk + 1, c_in, c_out, c_sems)

        @pl.when(n == 0)
        def _():
            acc[...] = jnp.zeros_like(acc)
            sacc[...] = jnp.zeros_like(sacc)
            db_ref[...] = jnp.zeros_like(db_ref)

        @pl.when(n < nblk)
        def _():
            kv = jnp.concatenate([k0[...], k1[...], k2[...]], axis=0)
            dks, dvs = [], []
            for k in range(N_KV_HEADS):
                kk = kv[:, k * HEAD_DIM:(k + 1) * HEAD_DIM]
                vk = kv[:, KV_WIDTH + k * HEAD_DIM:KV_WIDTH + (k + 1) * HEAD_DIM]
                heads = list(range(GQA_GROUP * k, GQA_GROUP * (k + 1)))
                stack = lambda ref: jnp.concatenate([ref[:, h * HEAD_DIM:(h + 1) * HEAD_DIM] for h in heads], axis=0)
                qk, do, o = stack(q_ref), stack(da_ref), stack(a_ref)
                bias = jnp.concatenate([bias_ref[0, h] for h in heads], axis=0)
                sink = jnp.concatenate([jnp.full((QBLK, 1), sink_ref[h], F32) for h in heads], axis=0)
                logits = lax.dot_general(qk, kk, NT_DIMS, preferred_element_type=F32) * ATTN_SCALE + bias
                mx = jnp.maximum(jnp.max(logits, axis=-1, keepdims=True), sink)
                e = jnp.exp(logits - mx)
                e_sink = jnp.exp(sink - mx)
                inv = 1.0 / (jnp.sum(e, axis=-1, keepdims=True) + e_sink)
                p = e * inv
                dp = lax.dot_general(do, vk, NT_DIMS, preferred_element_type=F32)
                delta = jnp.sum(do.astype(F32) * o.astype(F32), axis=-1, keepdims=True)
                dlog = p * (dp - delta)
                d_sink = -(e_sink * inv * delta)
                dlb = dlog.astype(BF16)
                dq = jnp.dot(dlb, kk, preferred_element_type=F32) * ATTN_SCALE
                for g, h in enumerate(heads):
                    rows = slice(g * QBLK, (g + 1) * QBLK)
                    sacc[h] += d_sink[rows]
                    db_ref[h] += dlog[rows]
                    dq_ref[:, h * HEAD_DIM:(h + 1) * HEAD_DIM] = dq[rows].astype(BF16)
                dks.append(lax.dot_general(dlb, qk, TN_DIMS, preferred_element_type=F32))
                dvs.append(lax.dot_general(p.astype(BF16), do, TN_DIMS, preferred_element_type=F32))
            contrib = jnp.concatenate(dks + dvs, axis=1)
            acc[n % 3] += contrib[0:QBLK]
            acc[(n + 1) % 3] += contrib[QBLK:2 * QBLK]
            acc[(n + 2) % 3] = contrib[2 * QBLK:3 * QBLK]

        col = lax.broadcasted_iota(jnp.int32, (1, 2 * KV_WIDTH), 1)
        dkv_ref[...] = (acc[n % 3] * jnp.where(col < KV_WIDTH, ATTN_SCALE, 1.0)).astype(BF16)

        @pl.when(n == nblk)
        def _():
            for h in range(N_Q_HEADS):
                ds_ref[h:h + 1, :] = jnp.full((1, 128), jnp.sum(sacc[h]), F32)

    res = pl.pallas_call(
        body, name="attn_bwd", grid=(nblk + 1,),
        out_shape=[jax.ShapeDtypeStruct((s, ATTN_WIDTH), BF16), jax.ShapeDtypeStruct((s, 2 * KV_WIDTH), BF16),
                   jax.ShapeDtypeStruct((N_Q_HEADS, QBLK, SPAN), F32),
                   jax.ShapeDtypeStruct((N_Q_HEADS, 128), F32)] + comm.out_shapes(),
        in_specs=[pl.BlockSpec(memory_space=pltpu.SMEM), q_spec] + kv_specs
                 + [q_spec, q_spec, _bias_spec(nblk)] + comm.specs(),
        out_specs=[q_spec, pl.BlockSpec((QBLK, 2 * KV_WIDTH), lambda n: (jnp.maximum(n - 1, 0), 0)),
                   _const_spec((N_Q_HEADS, QBLK, SPAN)), _const_spec((N_Q_HEADS, 128))] + comm.specs(),
        scratch_shapes=[pltpu.VMEM((3, QBLK, 2 * KV_WIDTH), F32), pltpu.VMEM((N_Q_HEADS, QBLK, 1), F32)]
                       + comm.scratch(),
        compiler_params=_params(("arbitrary",)),
    )(sink, qkv, kvp, kvp, kvp, a, da, bias3, *comm.arrays)
    return res[:4], res[4:]


def _pool_bwd_rows(u, gp, dyp, wp, scale):
    s, pw = u.shape
    c = pw // len(POOL_SIZES)
    tm = min(512, s)
    nt = s // tm

    def body(up, uc, un, gp_ref, dyp_ref, wp_ref, sc_ref, e_ref, dgp_ref, gwp_ref, dsc_ref):
        i = pl.program_id(0)

        @pl.when(i == 0)
        def _():
            gwp_ref[...] = jnp.zeros_like(gwp_ref)
            dsc_ref[...] = jnp.zeros_like(dsc_ref)

        ue = _with_halo(up, uc, un, i, nt)
        for g, w in enumerate(POOL_SIZES):
            cols = slice(g * c, (g + 1) * c)
            xg = ue[:, cols]
            inv_cnt = 1.0 / _pool_counts(i, tm, s, w)
            ws = _window_sum(xg, w, False)[POOL_HALO:POOL_HALO + tm]
            pooled = (ws * inv_cnt - xg[POOL_HALO:POOL_HALO + tm]).astype(BF16)
            mixed = jnp.dot(pooled, wp_ref[g], preferred_element_type=F32)
            sc = sc_ref[:, cols]
            gpf = gp_ref[:, cols].astype(F32)
            dyp_g = dyp_ref[:, cols].astype(F32)
            dp = dyp_g * _silu(gpf)
            dgp_ref[:, cols] = (dyp_g * (mixed * sc) * _dsilu(gpf)).astype(BF16)
            dsc_ref[:, cols] += jnp.sum(dp * mixed, axis=0, keepdims=True)
            dmixed = (dp * sc).astype(BF16)
            dpooled = lax.dot_general(dmixed, wp_ref[g], NT_DIMS, preferred_element_type=F32)
            gwp_ref[g] += lax.dot_general(pooled, dmixed, TN_DIMS, preferred_element_type=F32)
            e_ref[:, cols] = dpooled * inv_cnt

    return pl.pallas_call(
        body, name="pool_bwd_rows", grid=(nt,),
        out_shape=[jax.ShapeDtypeStruct((s, pw), F32), jax.ShapeDtypeStruct((s, pw), BF16),
                   jax.ShapeDtypeStruct(wp.shape, F32), jax.ShapeDtypeStruct((1, pw), F32)],
        in_specs=_halo_specs(tm, pw, s) + [_row_spec(tm, pw), _row_spec(tm, pw), _const_spec(wp.shape), _const_spec((1, pw))],
        out_specs=[_row_spec(tm, pw), _row_spec(tm, pw), _const_spec(wp.shape), _const_spec((1, pw))],
        compiler_params=_params(("arbitrary",)),
    )(u, u, u, gp, dyp, wp, scale)


def _pool_bwd_window(e):
    s, pw = e.shape
    c = pw // len(POOL_SIZES)
    tm = min(512, s)
    nt = s // tm

    def body(ep, ec, en, du_ref):
        i = pl.program_id(0)
        ee = _with_halo(ep, ec, en, i, nt)
        for g, w in enumerate(POOL_SIZES):
            cols = slice(g * c, (g + 1) * c)
            eg = ee[:, cols]
            ws = _window_sum(eg, w, True)[POOL_HALO:POOL_HALO + tm]
            du_ref[:, cols] = (ws - eg[POOL_HALO:POOL_HALO + tm] * _pool_counts(i, tm, s, w)).astype(BF16)

    return pl.pallas_call(
        body, name="pool_bwd_window", grid=(nt,),
        out_shape=jax.ShapeDtypeStruct((s, pw), BF16),
        in_specs=_halo_specs(tm, pw, s), out_specs=_row_spec(tm, pw),
        compiler_params=_params(("arbitrary",)),
    )(e, e, e)


def _dh_merge(dgm, pieces, w_blk, w_full, comm):
    s, n = dgm.shape
    d = w_blk.shape[1]
    tm = min(512, s)
    nt = s // tm
    widths = [p.shape[1] for p in pieces]
    n_p = len(pieces)
    wcols = sum(widths)

    def body(*refs):
        ins, outs, scratch, c_in, c_out, c_sems = _split_refs(refs, 3 + n_p, 1, 4, comm)
        dg_ref = ins[0]
        p_refs = ins[1:1 + n_p]
        w_hbm, wi_hbm = ins[1 + n_p:]
        o_ref, = outs
        w_vmem, wi_vmem, sems, sem = scratch
        comm.run(pl.program_id(0), nt, c_in, c_out, c_sems)

        @pl.when(pl.program_id(0) == 0)
        def _():
            cp = pltpu.make_async_copy(wi_hbm.at[:, pl.ds(0, wcols)], wi_vmem, sem)
            cp.start()
            _load_blocked(w_hbm, w_vmem, sems, 1)
            cp.wait()

        for r0 in range(0, d, 512):
            dh = lax.dot_general(dg_ref[...], w_vmem[r0:r0 + 512, :], NT_DIMS, preferred_element_type=F32)
            off = 0
            for p_ref, wd in zip(p_refs, widths):
                dh = dh + lax.dot_general(p_ref[...], wi_vmem[r0:r0 + 512, off:off + wd], NT_DIMS,
                                          preferred_element_type=F32)
                off += wd
            o_ref[:, r0:r0 + 512] = dh

    res = pl.pallas_call(
        body, name="dh_merge", grid=(nt,),
        out_shape=[jax.ShapeDtypeStruct((s, d), F32)] + comm.out_shapes(),
        in_specs=[_row_spec(tm, n)] + [_row_spec(tm, wd) for wd in widths] + [ANY_SPEC, ANY_SPEC] + comm.specs(),
        out_specs=[_row_spec(tm, d)] + comm.specs(),
        scratch_shapes=[pltpu.VMEM((d, n), BF16), pltpu.VMEM((d, wcols), BF16), pltpu.SemaphoreType.DMA((N_DEV,)),
                        pltpu.SemaphoreType.DMA] + comm.scratch(),
        compiler_params=_params(("arbitrary",)),
    )(dgm, *pieces, w_blk, w_full, *comm.arrays)
    return res[0], res[1:]


def _dh_final(dh1, pieces, w_full, x2, dy, vec):
    s, d = x2.shape
    tm = min(256, s)
    widths = [p.shape[1] for p in pieces]
    n_p = len(pieces)
    wcols = sum(widths)
    comm = NO_COMM

    def body(*refs):
        ins, outs, scratch, _, _, _ = _split_refs(refs, 5 + n_p, 2, 2, comm)
        dh1_ref = ins[0]
        p_refs = ins[1:1 + n_p]
        w_hbm, x_ref, dy_ref, vec_ref = ins[1 + n_p:]
        gx_ref, acc_ref = outs
        w_vmem, sem = scratch

        @pl.when(pl.program_id(0) == 0)
        def _():
            cp = pltpu.make_async_copy(w_hbm.at[:, pl.ds(w_full.shape[1] - wcols, wcols)], w_vmem, sem)
            cp.start()
            cp.wait()
            acc_ref[...] = jnp.zeros_like(acc_ref)

        scale1 = vec_ref[0:1, :]
        pre_g = vec_ref[2:3, :]
        xf = x_ref[...]
        r = lax.rsqrt(jnp.mean(xf * xf, axis=-1, keepdims=True) + EPS)
        xn = xf * r
        hn = xn * pre_g
        for r0 in range(0, d, 512):
            rows = slice(r0, r0 + 512)
            dh = dh1_ref[:, rows]
            off = 0
            for p_ref, wd in zip(p_refs, widths):
                dh = dh + lax.dot_general(p_ref[...], w_vmem[r0:r0 + 512, off:off + wd], NT_DIMS,
                                          preferred_element_type=F32)
                off += wd
            acc_ref[0:1, rows] += jnp.sum(dh, axis=0, keepdims=True)
            acc_ref[1:2, rows] += jnp.sum(dh * hn[:, rows], axis=0, keepdims=True)
            dhn = dh * scale1[:, rows]
            acc_ref[2:3, rows] += jnp.sum(dhn * xn[:, rows], axis=0, keepdims=True)
            gx_ref[:, rows] = dhn * pre_g[:, rows]
        dxn = gx_ref[...]
        dx = r * (dxn - xn * jnp.mean(dxn * xn, axis=-1, keepdims=True))
        gx_ref[...] = dy_ref[...] + dx

    return pl.pallas_call(
        body, name="dh_final", grid=(s // tm,),
        out_shape=[jax.ShapeDtypeStruct((s, d), F32), jax.ShapeDtypeStruct((8, d), F32)],
        in_specs=[_row_spec(tm, d)] + [_row_spec(tm, wd) for wd in widths]
                 + [ANY_SPEC, _row_spec(tm, d), _row_spec(tm, d), _const_spec((8, d))],
        out_specs=[_row_spec(tm, d), _const_spec((8, d))],
        scratch_shapes=[pltpu.VMEM((d, wcols), BF16), pltpu.SemaphoreType.DMA],
        compiler_params=_params(("arbitrary",)),
    )(dh1, *pieces, w_full, x2, dy, vec)


def _reduce_adamw(parts, w, m, v, name):
    r, c = w.shape
    n_parts = len(parts)
    rp = r // n_parts
    tr = min(256, rp)
    per = rp // tr

    def body(*refs):
        p_refs = refs[:n_parts]
        w_ref, m_ref, v_ref, g_ref, dl_ref, m2_ref, v2_ref = refs[n_parts:]
        i = pl.program_id(0)
        for k, p_ref in enumerate(p_refs):
            @pl.when((i >= k * per) & (i < (k + 1) * per))
            def _(p_ref=p_ref):
                g = p_ref[0].astype(F32)
                for dev in range(1, N_DEV):
                    g = g + p_ref[dev].astype(F32)
                g_ref[...] = g
                dl_ref[...], m2_ref[...], v2_ref[...] = _adamw(w_ref[...], g, m_ref[...], v_ref[...])

    spec = _row_spec(tr, c)
    part_specs = [pl.BlockSpec((N_DEV, tr, c), (lambda i, k=k: (0, jnp.clip(i - k * per, 0, per - 1), 0)))
                  for k in range(n_parts)]
    return pl.pallas_call(
        body, name=name, grid=(r // tr,),
        out_shape=[jax.ShapeDtypeStruct((r, c), F32)] * 4,
        in_specs=part_specs + [spec, spec, spec],
        out_specs=[spec] * 4, compiler_params=_params(("arbitrary",)),
    )(*parts, w, m, v)


def _small_reduce_adamw(gathered, w, m, v):
    rows, c = w.shape

    def body(p_ref, w_ref, m_ref, v_ref, g_ref, dl_ref, m2_ref, v2_ref):
        g = p_ref[0:rows, :]
        for dev in range(1, N_DEV):
            g = g + p_ref[dev * rows:(dev + 1) * rows, :]
        g_ref[...] = g
        dl_ref[...], m2_ref[...], v2_ref[...] = _adamw(w_ref[...], g, m_ref[...], v_ref[...])

    vm = pl.BlockSpec(memory_space=pltpu.VMEM)
    return pl.pallas_call(
        body, name="small_reduce_adamw", out_shape=[jax.ShapeDtypeStruct((rows, c), F32)] * 4,
        in_specs=[vm] * 4, out_specs=[vm] * 4, compiler_params=_params(),
    )(gathered, w, m, v)


def _pack_small(parts, total):
    flat = jnp.concatenate([p.reshape(-1).astype(F32) for p in parts])
    flat = jnp.pad(flat, (0, total - flat.shape[0]))
    return flat.reshape(8, total // 8)


def _unpack_small(packed, shapes):
    flat = packed.reshape(-1)
    out, off = [], 0
    for shp in shapes:
        size = math.prod(shp)
        out.append(flat[off:off + size].reshape(shp))
        off += size
    return out


def kernel(x, c, rel_bias_table, w_ada, b_ada, pre_norm_g, post_norm_g, w_in, attn_sink, w_pool_group, pool_scale, w_branch_attn, w_branch_pool, w_merge, b_merge, w_out, loss_target, m_rel_bias_table, m_w_ada, m_b_ada, m_pre_norm_g, m_post_norm_g, m_w_in, m_attn_sink, m_w_pool_group, m_pool_scale, m_w_branch_attn, m_w_branch_pool, m_w_merge, m_b_merge, m_w_out, v_rel_bias_table, v_w_ada, v_b_ada, v_pre_norm_g, v_post_norm_g, v_w_in, v_attn_sink, v_w_pool_group, v_pool_scale, v_w_branch_attn, v_w_branch_pool, v_w_merge, v_b_merge, v_w_out):
    xi, yi, ci = lax.axis_index("x"), lax.axis_index("y"), lax.axis_index("c")
    me = 4 * xi + 2 * yi + ci
    x2, t2 = x[0], loss_target[0]
    s, d = x2.shape
    pw = d // 2
    n_groups = len(POOL_SIZES)
    cg = pw // n_groups
    in_w = w_in.shape[2] * N_DEV
    ada_cols = w_ada.shape[2]

    c_all = _allgather_small(jnp.broadcast_to(c, (8, d)), "gather_c").reshape(N_DEV, 8, d)[:, 0, :]
    c16 = jnp.concatenate([c_all, jnp.zeros_like(c_all)], axis=0)
    b_cols = lax.dynamic_slice_in_dim(b_ada, me * ada_cols, ada_cols, axis=1)
    mod_part = _ada_fwd(c16, w_ada[0], b_cols)[:8]
    mod_all = _allgather_small(mod_part, "gather_mod").reshape(N_DEV, 8, ada_cols)
    mod = lax.dynamic_index_in_dim(mod_all, me, axis=1, keepdims=False).reshape(3 * d)
    shift, scale, gate = mod[:d], mod[d:2 * d], mod[2 * d:]
    zrow = jnp.zeros((d,), F32)
    vec_in = jnp.stack([1.0 + scale, shift, pre_norm_g[0], zrow, zrow, zrow, zrow, zrow])
    vec_out = jnp.stack([gate, post_norm_g[0], zrow, zrow, zrow, zrow, zrow, zrow])

    rows_per = w_pool_group.shape[2]
    late_shards = [w.astype(BF16) for w in (w_branch_attn[0], w_branch_pool[0], w_out[0],
                                            w_pool_group[0].reshape(n_groups * rows_per, cg))]
    w_in_blk, = _allgather_hbm([w_in[0].astype(BF16)], "gather_w_in")
    w_in_full = w_in_blk.transpose(1, 0, 2).reshape(d, in_w)

    rel = jnp.arange(SPAN)[None, :] - WINDOW - jnp.arange(QBLK)[:, None]
    idx = jnp.where(jnp.abs(rel) <= WINDOW, _t5_bucket(rel), -1).astype(jnp.int32).reshape(1, QBLK * SPAN)
    bias = _bias_variants(_bias_build(rel_bias_table.T, idx).reshape(N_Q_HEADS, QBLK, SPAN))
    sink = attn_sink[0]

    (h, qkv, ga, u, gp), (w_merge_blk,) = _fwd_in(x2, vec_in, w_in_full, _Carried("gather", [w_merge[0].astype(BF16)]))
    g, (w_ba_blk, w_bp_blk, w_out_blk, w_pool_blk) = _merge_fwd(h, w_merge_blk, b_merge, _Carried("gather", late_shards))
    w_pool_full = w_pool_blk.reshape(N_DEV, n_groups, rows_per, cg).transpose(1, 0, 2, 3).reshape(n_groups, cg, cg)
    kvp = jnp.pad(qkv[:, ATTN_WIDTH:], ((WINDOW, WINDOW), (0, 0)))
    a, ya = _attn_fwd(qkv, kvp, ga, bias, sink)
    yp = _pool_fwd(u, gp, w_pool_full, pool_scale)
    ba, bp, merged = _branch_fwd(ya, yp, g, w_ba_blk, w_bp_blk)
    do, dgm, dba, dbp, dy, acc_out, db_merge = _out_fused(merged, x2, t2, g, ba, bp, vec_out, w_out_blk)
    loss = lax.psum(0.5 * jnp.sum(acc_out[0]) / d, MESH_AXES)

    gw_out = _wgrad(merged, do, d, d, "wgrad_out")[0].reshape(N_DEV, d // N_DEV, d)
    gw_ba = _wgrad(ya, dba, d, d // N_DEV, "wgrad_branch_attn")
    gw_bp = _wgrad(yp, dbp, d, d // N_DEV, "wgrad_branch_pool")
    gw_merge, (r_out,) = _wgrad(h, dgm, min(2 * d, 2048), 2 * d // N_DEV, "wgrad_merge",
                                _Carried("exchange", [gw_out]))
    (da, dga, dyp), (r_ba, r_bp) = _branch_bwd(dba, dbp, a, ga, w_ba_blk, w_bp_blk,
                                               _Carried("exchange", [gw_ba, gw_bp]))
    (dq, dkv, dbias, dsink), (r_merge,) = _attn_bwd(qkv, kvp, a, da, bias, sink, _Carried("exchange", [gw_merge]))
    e, dgp, gw_pool, dpool_scale = _pool_bwd_rows(u, gp, dyp, w_pool_full, pool_scale)
    du = _pool_bwd_window(e)
    pieces = [dq, dkv, dga, du, dgp]
    gw_in = jnp.concatenate([_wgrad(h, p, p.shape[1], p.shape[1], "wgrad_in_%d" % i)[0]
                             for i, p in enumerate(pieces)], axis=1)
    gw_in_blk = gw_in.reshape(d, N_DEV, in_w // N_DEV).transpose(1, 0, 2)
    gw_pool_blk = gw_pool.astype(BF16).reshape(n_groups, N_DEV, rows_per, cg).transpose(1, 0, 2, 3).reshape(
        N_DEV, n_groups * rows_per, cg)
    dh1, (r_in, r_pool) = _dh_merge(dgm, pieces[:3], w_merge_blk, w_in_full, _Carried("exchange", [gw_in_blk, gw_pool_blk]))
    gx, acc_in = _dh_final(dh1, pieces[3:], w_in_full, x2, dy, vec_in)
    d_table = _bias_reduce(dbias.reshape(N_Q_HEADS, QBLK * SPAN), idx).T

    recv = [[r_in], [r_merge], [r_ba], [r_bp], [r_out], [r_pool]]
    names = ["w_in", "w_merge", "w_branch_attn", "w_branch_pool", "w_out", "w_pool_group"]
    big_w = [w_in, w_merge, w_branch_attn, w_branch_pool, w_out, w_pool_group]
    big_m = [m_w_in, m_w_merge, m_w_branch_attn, m_w_branch_pool, m_w_out, m_w_pool_group]
    big_v = [v_w_in, v_w_merge, v_w_branch_attn, v_w_branch_pool, v_w_out, v_w_pool_group]
    big = {}
    for nm, rc, w_, m_, v_ in zip(names, recv, big_w, big_m, big_v):
        shp2 = (len(rc) * rc[0].shape[1], rc[0].shape[2])
        outs = _reduce_adamw(rc, w_.reshape(shp2), m_.reshape(shp2), v_.reshape(shp2), "adamw_" + nm)
        big[nm] = [o.reshape(w_.shape) for o in outs]

    dmod = jnp.concatenate([acc_in[0], acc_in[1], acc_out[1]])
    small_names = ["b_ada", "pre_norm_g", "post_norm_g", "pool_scale", "b_merge", "rel_bias_table", "attn_sink"]
    small_g = [dmod, acc_in[2], acc_out[2], dpool_scale, db_merge, d_table, dsink[:, 0]]
    small_w = [b_ada, pre_norm_g, post_norm_g, pool_scale, b_merge, rel_bias_table, attn_sink]
    small_m = [m_b_ada, m_pre_norm_g, m_post_norm_g, m_pool_scale, m_b_merge, m_rel_bias_table, m_attn_sink]
    small_v = [v_b_ada, v_pre_norm_g, v_post_norm_g, v_pool_scale, v_b_merge, v_rel_bias_table, v_attn_sink]
    n_small = sum(math.prod(w_.shape) for w_ in small_w)
    total = -(-n_small // 1024) * 1024
    gathered = _allgather_small(_pack_small(small_g, total), "gather_small_grads")
    sm = _small_reduce_adamw(gathered, _pack_small(small_w, total), _pack_small(small_m, total),
                             _pack_small(small_v, total))
    shapes = [w_.shape for w_ in small_w]
    small = {nm: vals for nm, vals in zip(small_names, zip(*[_unpack_small(o, shapes) for o in sm]))}

    dmod_all = gathered.reshape(N_DEV, -1)[:, :3 * d]
    dmod_mine = lax.dynamic_slice_in_dim(dmod_all, me * ada_cols, ada_cols, axis=1)
    dmod16 = jnp.concatenate([dmod_mine, jnp.zeros_like(dmod_mine)], axis=0)
    ada = [o[None] for o in _ada_bwd(c16, dmod16, w_ada[0], m_w_ada[0], v_w_ada[0])]

    order = ["rel_bias_table", "w_ada", "b_ada", "pre_norm_g", "post_norm_g", "w_in", "attn_sink", "w_pool_group",
             "pool_scale", "w_branch_attn", "w_branch_pool", "w_merge", "b_merge", "w_out"]
    results = dict(big)
    results.update({nm: list(vals) for nm, vals in small.items()})
    results["w_ada"] = ada
    outs = [loss, gx[None]]
    for k in range(4):
        outs += [results[nm][k] for nm in order]
    return tuple(outs)
```

```python
import functools
import math

import jax
import jax.numpy as jnp
from jax import lax
from jax.experimental import pallas as pl
from jax.experimental.pallas import tpu as pltpu

F32 = jnp.float32
BF16 = jnp.bfloat16

HEAD_DIM = 128
N_Q_HEADS = 8
N_KV_HEADS = 2
GQA_GROUP = N_Q_HEADS // N_KV_HEADS
ATTN_WIDTH = N_Q_HEADS * HEAD_DIM
KV_WIDTH = N_KV_HEADS * HEAD_DIM
WINDOW = 128
QBLK = 128
SPAN = QBLK + 2 * WINDOW
N_BUCKETS = 32
MAX_DISTANCE = 128
POOL_SIZES = (2, 4, 8, 16)
EPS = 1e-6
NEG_INF = -1e30
ATTN_SCALE = HEAD_DIM ** -0.5

ADAM_LR = 0.001
ADAM_B1 = 0.9
ADAM_B2 = 0.999
ADAM_EPS = 1e-08
ADAM_WD = 0.01
ADAM_STEP = 10

N_DEV = 8
MESH_AXES = ("x", "y", "c")
MESH = pl.DeviceIdType.MESH
POOL_HALO = 16
V7X_VMEM_LIMIT = 56 * 1024 * 1024

NT_DIMS = (((1,), (1,)), ((), ()))
TN_DIMS = (((0,), (0,)), ((), ()))


def _params(sem=None, vmem=V7X_VMEM_LIMIT):
    return pltpu.CompilerParams(dimension_semantics=sem, vmem_limit_bytes=vmem)


def _silu(v):
    return v * jax.nn.sigmoid(v)


def _dsilu(v):
    s = jax.nn.sigmoid(v)
    return s * (1.0 + v * (1.0 - s))


def _row_spec(tm, width):
    return pl.BlockSpec((tm, width), lambda i: (i, 0))


def _const_spec(shape):
    return pl.BlockSpec(shape, lambda *_: (0,) * len(shape))


ANY_SPEC = pl.BlockSpec(memory_space=pl.ANY)


def _load_blocked(w_blk, w_vmem, sems, axis):
    n = w_blk.shape[0]
    width = w_blk.shape[1 + axis]
    copies = []
    for d in range(n):
        dst = w_vmem.at[pl.ds(d * width, width), :] if axis == 0 else w_vmem.at[:, pl.ds(d * width, width)]
        copies.append(pltpu.make_async_copy(w_blk.at[d], dst, sems.at[d]))
    for cp in copies:
        cp.start()
    for cp in copies:
        cp.wait()


def _mesh_pos():
    return lax.axis_index("x"), lax.axis_index("y"), lax.axis_index("c")


def _allgather_small(blk, name):
    m_per, n = blk.shape

    def body(x_ref, out_ref, send_sems, recv_sems, local_sem):
        x, y, c = _mesh_pos()
        me, sibling = (x, y, c), (x, y, 1 - c)
        chips = [(1 - x, y), (x, 1 - y), (1 - x, 1 - y)]

        def rows(px, py, pc):
            return out_ref.at[pl.ds((4 * px + 2 * py + pc) * m_per, m_per), :]

        def copy(k, block, to, src=None):
            return pltpu.make_async_remote_copy(
                src_ref=rows(*block) if src is None else src, dst_ref=rows(*block),
                send_sem=send_sems.at[k], recv_sem=recv_sems.at[k], device_id=to, device_id_type=MESH)

        mine = pltpu.make_async_copy(x_ref, rows(*me), local_sem)
        mine.start()
        first = [copy(0, me, sibling, src=x_ref)]
        first += [copy(1 + j, me, (*chip, c), src=x_ref) for j, chip in enumerate(chips)]
        for cp in first:
            cp.start()
        passed = [copy(4 + j, (*chip, c), sibling) for j, chip in enumerate(chips)]
        for j, chip in enumerate(chips):
            copy(1 + j, (*chip, c), me).wait_recv()
            passed[j].start()
        copy(0, sibling, me).wait_recv()
        for j, chip in enumerate(chips):
            copy(4 + j, (*chip, 1 - c), me).wait_recv()
        for cp in first + passed:
            cp.wait_send()
        mine.wait()

    return pl.pallas_call(
        body, name=name,
        out_shape=jax.ShapeDtypeStruct((N_DEV * m_per, n), blk.dtype),
        in_specs=[pl.BlockSpec(memory_space=pltpu.VMEM)],
        out_specs=pl.BlockSpec(memory_space=pltpu.VMEM),
        scratch_shapes=[pltpu.SemaphoreType.DMA((7,)), pltpu.SemaphoreType.DMA((7,)), pltpu.SemaphoreType.DMA],
    )(blk)


def _gather_copy(outs, send_sems, recv_sems, i, k, block, to, src=None):
    slot = outs[i].at[4 * block[0] + 2 * block[1] + block[2]]
    return pltpu.make_async_remote_copy(
        src_ref=slot if src is None else src, dst_ref=slot,
        send_sem=send_sems.at[7 * i + k], recv_sem=recv_sems.at[7 * i + k], device_id=to, device_id_type=MESH)


def _gather_places():
    x, y, c = _mesh_pos()
    return (x, y, c), (x, y, 1 - c), [(1 - x, y), (x, 1 - y), (1 - x, 1 - y)]


def _gather_start(ins, outs, send_sems, recv_sems, local_sems):
    me, sibling, chips = _gather_places()
    for i, src in enumerate(ins):
        pltpu.make_async_copy(src, outs[i].at[4 * me[0] + 2 * me[1] + me[2]], local_sems.at[i]).start()
        _gather_copy(outs, send_sems, recv_sems, i, 0, me, sibling, src=src).start()
        for j, chip in enumerate(chips):
            _gather_copy(outs, send_sems, recv_sems, i, 1 + j, me, (*chip, me[2]), src=src).start()


def _gather_forward(ins, outs, send_sems, recv_sems):
    me, sibling, chips = _gather_places()
    for i in range(len(ins)):
        for j, chip in enumerate(chips):
            _gather_copy(outs, send_sems, recv_sems, i, 1 + j, (*chip, me[2]), me).wait_recv()
            _gather_copy(outs, send_sems, recv_sems, i, 4 + j, (*chip, me[2]), sibling).start()


def _gather_finish(ins, outs, send_sems, recv_sems, local_sems):
    me, sibling, chips = _gather_places()
    for i, src in enumerate(ins):
        _gather_copy(outs, send_sems, recv_sems, i, 0, sibling, me).wait_recv()
        for j, chip in enumerate(chips):
            _gather_copy(outs, send_sems, recv_sems, i, 4 + j, (*chip, 1 - me[2]), me).wait_recv()
        _gather_copy(outs, send_sems, recv_sems, i, 0, me, sibling, src=src).wait_send()
        for j, chip in enumerate(chips):
            _gather_copy(outs, send_sems, recv_sems, i, 1 + j, me, (*chip, me[2]), src=src).wait_send()
            _gather_copy(outs, send_sems, recv_sems, i, 4 + j, (*chip, me[2]), sibling).wait_send()
        pltpu.make_async_copy(src, outs[i].at[4 * me[0] + 2 * me[1] + me[2]], local_sems.at[i]).wait()


EXCHANGE_FLIPS = [(fx, fy, fc) for fx in (0, 1) for fy in (0, 1) for fc in (0, 1)][1:]


def _exchange_copy(ins, outs, send_sems, recv_sems, i, k, arriving):
    x, y, c = _mesh_pos()
    f = EXCHANGE_FLIPS[k]
    p = (1 - x if f[0] else x, 1 - y if f[1] else y, 1 - c if f[2] else c)
    p_idx = 4 * p[0] + 2 * p[1] + p[2]
    return pltpu.make_async_remote_copy(
        src_ref=ins[i].at[p_idx], dst_ref=outs[i].at[p_idx if arriving else 4 * x + 2 * y + c],
        send_sem=send_sems.at[7 * i + k], recv_sem=recv_sems.at[7 * i + k], device_id=p, device_id_type=MESH)


def _exchange_start(ins, outs, send_sems, recv_sems, local_sems):
    x, y, c = _mesh_pos()
    me_idx = 4 * x + 2 * y + c
    for i in range(len(ins)):
        pltpu.make_async_copy(ins[i].at[me_idx], outs[i].at[me_idx], local_sems.at[i]).start()
        for k in range(7):
            _exchange_copy(ins, outs, send_sems, recv_sems, i, k, False).start()


def _exchange_finish(ins, outs, send_sems, recv_sems, local_sems):
    x, y, c = _mesh_pos()
    me_idx = 4 * x + 2 * y + c
    for i in range(len(ins)):
        for k in range(7):
            _exchange_copy(ins, outs, send_sems, recv_sems, i, k, True).wait_recv()
            _exchange_copy(ins, outs, send_sems, recv_sems, i, k, False).wait_send()
        pltpu.make_async_copy(ins[i].at[me_idx], outs[i].at[me_idx], local_sems.at[i]).wait()


class _Carried:
    def __init__(self, kind, arrays):
        self.kind = kind
        self.arrays = list(arrays)
        self.n = len(self.arrays)

    def specs(self):
        return [ANY_SPEC] * self.n

    def out_shapes(self):
        lead = (N_DEV,) if self.kind == "gather" else ()
        return [jax.ShapeDtypeStruct(lead + a.shape, a.dtype) for a in self.arrays]

    def scratch(self):
        return [pltpu.SemaphoreType.DMA((7 * self.n,)), pltpu.SemaphoreType.DMA((7 * self.n,)),
                pltpu.SemaphoreType.DMA((self.n,))]

    def run(self, step, n_steps, ins, outs, sems):
        send_sems, recv_sems, local_sems = sems
        if self.n == 0:
            return

        @pl.when(step == 0)
        def _():
            if self.kind == "gather":
                _gather_start(ins, outs, send_sems, recv_sems, local_sems)
            else:
                _exchange_start(ins, outs, send_sems, recv_sems, local_sems)

        if self.kind == "gather":
            @pl.when(step == n_steps // 2)
            def _():
                _gather_forward(ins, outs, send_sems, recv_sems)

        @pl.when(step == n_steps - 1)
        def _():
            if self.kind == "gather":
                _gather_finish(ins, outs, send_sems, recv_sems, local_sems)
            else:
                _exchange_finish(ins, outs, send_sems, recv_sems, local_sems)


NO_COMM = _Carried("exchange", [])


def _split_refs(refs, n_in, n_out, n_scratch, comm):
    a = n_in
    b = a + comm.n
    c = b + n_out
    d = c + comm.n
    e = d + n_scratch
    return refs[:a], refs[b:c], refs[d:e], refs[a:b], refs[c:d], refs[e:]


def _allgather_hbm(shards, name):
    n_arr = len(shards)

    def body(*refs):
        ins, outs = refs[:n_arr], refs[n_arr:2 * n_arr]
        send_sems, recv_sems, local_sems = refs[2 * n_arr:]
        _gather_start(ins, outs, send_sems, recv_sems, local_sems)
        _gather_forward(ins, outs, send_sems, recv_sems)
        _gather_finish(ins, outs, send_sems, recv_sems, local_sems)

    return pl.pallas_call(
        body, name=name,
        out_shape=[jax.ShapeDtypeStruct((N_DEV,) + s.shape, s.dtype) for s in shards],
        in_specs=[ANY_SPEC] * n_arr,
        out_specs=[ANY_SPEC] * n_arr,
        scratch_shapes=[pltpu.SemaphoreType.DMA((7 * n_arr,)), pltpu.SemaphoreType.DMA((7 * n_arr,)),
                        pltpu.SemaphoreType.DMA((n_arr,))],
    )(*shards)


def _ada_fwd(c16, w_shard, b_cols):
    rows, d = c16.shape
    nc = w_shard.shape[1]

    def body(c_ref, w_ref, b_ref, o_ref):
        sc = _silu(c_ref[...]).astype(BF16)
        o_ref[...] = jnp.dot(sc, w_ref[...].astype(BF16), preferred_element_type=F32) + b_ref[...]

    return pl.pallas_call(
        body, name="ada_fwd", out_shape=jax.ShapeDtypeStruct((rows, nc), F32),
        in_specs=[pl.BlockSpec(memory_space=pltpu.VMEM)] * 3, out_specs=pl.BlockSpec(memory_space=pltpu.VMEM),
        compiler_params=_params(),
    )(c16, w_shard, b_cols)


def _adamw(w, g, m, v):
    m2 = ADAM_B1 * m + (1.0 - ADAM_B1) * g
    v2 = ADAM_B2 * v + (1.0 - ADAM_B2) * (g * g)
    m_hat = m2 * (1.0 / (1.0 - ADAM_B1 ** ADAM_STEP))
    v_hat = v2 * (1.0 / (1.0 - ADAM_B2 ** ADAM_STEP))
    delta = -ADAM_LR * (m_hat / (jnp.sqrt(v_hat) + ADAM_EPS) + ADAM_WD * w)
    return delta, m2, v2


def _ada_bwd(c16, dmod16, w, m, v):
    rows, d = c16.shape
    nc = w.shape[1]
    tr = min(512, d)

    def body(c_ref, dm_ref, w_ref, m_ref, v_ref, g_ref, dl_ref, m2_ref, v2_ref):
        sc = _silu(c_ref[...]).astype(BF16)
        g = lax.dot_general(sc, dm_ref[...].astype(BF16), TN_DIMS, preferred_element_type=F32)
        g_ref[...] = g
        dl_ref[...], m2_ref[...], v2_ref[...] = _adamw(w_ref[...], g, m_ref[...], v_ref[...])

    spec = _row_spec(tr, nc)
    return pl.pallas_call(
        body, name="ada_bwd", grid=(d // tr,),
        out_shape=[jax.ShapeDtypeStruct((d, nc), F32)] * 4,
        in_specs=[pl.BlockSpec((rows, tr), lambda i: (0, i)), _const_spec((rows, nc)), spec, spec, spec],
        out_specs=[spec] * 4, compiler_params=_params(("arbitrary",)),
    )(c16, dmod16, w, m, v)


def _split3(v):
    hi = v.astype(BF16)
    r1 = v - hi.astype(F32)
    mid = r1.astype(BF16)
    lo = (r1 - mid.astype(F32)).astype(BF16)
    return hi, mid, lo


BIAS_CHUNK = QBLK * SPAN // 8


def _bias_build(table_t, idx):
    n = idx.shape[1]

    def body(t_ref, i_ref, o_ref):
        ids = i_ref[...]
        onehot = (lax.broadcasted_iota(jnp.int32, (N_BUCKETS, BIAS_CHUNK), 0) == ids).astype(BF16)
        t16 = jnp.concatenate([t_ref[...], jnp.zeros_like(t_ref[...])], axis=0)
        acc = jnp.zeros((2 * N_Q_HEADS, BIAS_CHUNK), F32)
        for part in _split3(t16):
            acc = acc + jnp.dot(part, onehot, preferred_element_type=F32)
        o_ref[...] = jnp.where(ids >= 0, acc[:N_Q_HEADS], NEG_INF)

    return pl.pallas_call(
        body, name="bias_build", grid=(n // BIAS_CHUNK,),
        out_shape=jax.ShapeDtypeStruct((N_Q_HEADS, n), F32),
        in_specs=[_const_spec((N_Q_HEADS, N_BUCKETS)), pl.BlockSpec((1, BIAS_CHUNK), lambda i: (0, i))],
        out_specs=pl.BlockSpec((N_Q_HEADS, BIAS_CHUNK), lambda i: (0, i)),
        compiler_params=_params(("arbitrary",)),
    )(table_t, idx)


def _bias_reduce(dbias, idx):
    n = idx.shape[1]

    def body(d_ref, i_ref, o_ref):
        @pl.when(pl.program_id(0) == 0)
        def _():
            o_ref[...] = jnp.zeros_like(o_ref)

        onehot = (lax.broadcasted_iota(jnp.int32, (N_BUCKETS, BIAS_CHUNK), 0) == i_ref[...]).astype(BF16)
        d16 = jnp.concatenate([d_ref[...], jnp.zeros_like(d_ref[...])], axis=0)
        acc = jnp.zeros((2 * N_Q_HEADS, N_BUCKETS), F32)
        for part in _split3(d16):
            acc = acc + lax.dot_general(part, onehot, NT_DIMS, preferred_element_type=F32)
        o_ref[...] += acc[:N_Q_HEADS]

    return pl.pallas_call(
        body, name="bias_reduce", grid=(n // BIAS_CHUNK,),
        out_shape=jax.ShapeDtypeStruct((N_Q_HEADS, N_BUCKETS), F32),
        in_specs=[pl.BlockSpec((N_Q_HEADS, BIAS_CHUNK), lambda i: (0, i)), pl.BlockSpec((1, BIAS_CHUNK), lambda i: (0, i))],
        out_specs=_const_spec((N_Q_HEADS, N_BUCKETS)),
        compiler_params=_params(("arbitrary",)),
    )(dbias, idx)


def _t5_bucket(rel):
    half = N_BUCKETS // 2
    max_exact = half // 2
    ret = jnp.where(rel > 0, half, 0)
    n = jnp.abs(rel)
    nf = jnp.maximum(n, 1).astype(jnp.float32)
    large = max_exact + (jnp.log(nf / max_exact) / math.log(MAX_DISTANCE / max_exact)
                         * (half - max_exact)).astype(jnp.int32)
    large = jnp.minimum(large, half - 1)
    return ret + jnp.where(n < max_exact, n, large)


def _fwd_in(x2, vec, w_full, comm):
    s, d = x2.shape
    pw = d // 2
    tm = min(512, s)
    nt = s // tm
    widths = (ATTN_WIDTH + 2 * KV_WIDTH, ATTN_WIDTH, pw, pw)
    dtypes = (BF16, BF16, F32, BF16)

    def body(*refs):
        ins, outs, scratch, c_in, c_out, c_sems = _split_refs(refs, 3, 5, 2, comm)
        x_ref, vec_ref, w_hbm = ins
        h_ref, qkv_ref, ga_ref, u_ref, gp_ref = outs
        w_vmem, sem = scratch
        comm.run(pl.program_id(0), nt, c_in, c_out, c_sems)

        @pl.when(pl.program_id(0) == 0)
        def _():
            cp = pltpu.make_async_copy(w_hbm, w_vmem, sem)
            cp.start()
            cp.wait()

        xf = x_ref[...]
        r = lax.rsqrt(jnp.mean(xf * xf, axis=-1, keepdims=True) + EPS)
        h = ((xf * r) * vec_ref[2:3, :]) * vec_ref[0:1, :] + vec_ref[1:2, :]
        hb = h.astype(BF16)
        h_ref[...] = hb
        off = 0
        for o_ref, wd in zip((qkv_ref, ga_ref, u_ref, gp_ref), widths):
            for c0 in range(0, wd, 512):
                c1 = min(c0 + 512, wd)
                o_ref[:, c0:c1] = jnp.dot(hb, w_vmem[:, off + c0:off + c1],
                                          preferred_element_type=F32).astype(o_ref.dtype)
            off += wd

    res = pl.pallas_call(
        body, name="fwd_in", grid=(nt,),
        out_shape=[jax.ShapeDtypeStruct((s, d), BF16)] + [jax.ShapeDtypeStruct((s, wd), dt) for wd, dt in zip(widths, dtypes)]
                  + comm.out_shapes(),
        in_specs=[_row_spec(tm, d), _const_spec((8, d)), ANY_SPEC] + comm.specs(),
        out_specs=[_row_spec(tm, d)] + [_row_spec(tm, wd) for wd in widths] + comm.specs(),
        scratch_shapes=[pltpu.VMEM(w_full.shape, BF16), pltpu.SemaphoreType.DMA] + comm.scratch(),
        compiler_params=_params(("arbitrary",)),
    )(x2, vec, w_full, *comm.arrays)
    return res[:5], res[5:]


def _merge_fwd(h, w_blk, b_merge, comm):
    s, d = h.shape
    n = w_blk.shape[0] * w_blk.shape[2]
    tm = min(512, s)
    nt = s // tm

    def body(*refs):
        ins, outs, scratch, c_in, c_out, c_sems = _split_refs(refs, 3, 1, 2, comm)
        h_ref, w_hbm, b_ref = ins
        g_ref, = outs
        w_vmem, sems = scratch
        comm.run(pl.program_id(0), nt, c_in, c_out, c_sems)

        @pl.when(pl.program_id(0) == 0)
        def _():
            _load_blocked(w_hbm, w_vmem, sems, 1)

        hb = h_ref[...]
        for c0 in range(0, n, 512):
            z = jnp.dot(hb, w_vmem[:, c0:c0 + 512], preferred_element_type=F32) + b_ref[:, c0:c0 + 512]
            g_ref[:, c0:c0 + 512] = jax.nn.sigmoid(z).astype(BF16)

    res = pl.pallas_call(
        body, name="merge_fwd", grid=(nt,),
        out_shape=[jax.ShapeDtypeStruct((s, n), BF16)] + comm.out_shapes(),
        in_specs=[_row_spec(tm, d), ANY_SPEC, _const_spec((1, n))] + comm.specs(),
        out_specs=[_row_spec(tm, n)] + comm.specs(),
        scratch_shapes=[pltpu.VMEM((d, n), BF16), pltpu.SemaphoreType.DMA((N_DEV,))] + comm.scratch(),
        compiler_params=_params(("arbitrary",)),
    )(h, w_blk, b_merge, *comm.arrays)
    return res[0], res[1:]


def _bias_variants(bias):
    t = lax.broadcasted_iota(jnp.int32, (1, 1, SPAN), 2)
    return jnp.stack([jnp.where(t < WINDOW, NEG_INF, bias), bias, jnp.where(t >= WINDOW + QBLK, NEG_INF, bias)])


def _bias_spec(nblk):
    def index(n):
        return (jnp.where(n == 0, 0, jnp.where(n >= nblk - 1, 2, 1)), 0, 0, 0)
    return pl.BlockSpec((1, N_Q_HEADS, QBLK, SPAN), index)


def _attn_head(q_ref, kv, bias_ref, sink_ref, h):
    k = h // GQA_GROUP
    qh = q_ref[:, h * HEAD_DIM:(h + 1) * HEAD_DIM]
    kk = kv[:, k * HEAD_DIM:(k + 1) * HEAD_DIM]
    logits = lax.dot_general(qh, kk, NT_DIMS, preferred_element_type=F32) * ATTN_SCALE + bias_ref[0, h]
    sink = sink_ref[h]
    mx = jnp.maximum(jnp.max(logits, axis=-1, keepdims=True), sink)
    e = jnp.exp(logits - mx)
    e_sink = jnp.exp(sink - mx)
    inv = 1.0 / (jnp.sum(e, axis=-1, keepdims=True) + e_sink)
    return qh, e, inv, e_sink


def _kv_specs(nblk):
    col = ATTN_WIDTH // (2 * KV_WIDTH)

    def spec(j):
        return pl.BlockSpec((QBLK, 2 * KV_WIDTH),
                            lambda n: (jnp.clip(jnp.minimum(n, nblk - 1) + j - 1, 0, nblk - 1), col))
    return [spec(j) for j in range(3)]


def _attn_fwd(qkv, ga, bias3, sink):
    s = qkv.shape[0]
    nblk = s // QBLK
    kv_specs = _kv_specs(nblk)

    def body(sink_ref, q_ref, k0, k1, k2, ga_ref, bias_ref, a_ref, ya_ref):
        kv = jnp.concatenate([k0[...], k1[...], k2[...]], axis=0)
        for h in range(N_Q_HEADS):
            k = h // GQA_GROUP
            cols = slice(h * HEAD_DIM, (h + 1) * HEAD_DIM)
            vk = kv[:, KV_WIDTH + k * HEAD_DIM:KV_WIDTH + (k + 1) * HEAD_DIM]
            _, e, inv, _ = _attn_head(q_ref, kv, bias_ref, sink_ref, h)
            o = jnp.dot(e.astype(BF16), vk, preferred_element_type=F32) * inv
            a_ref[:, cols] = o.astype(BF16)
            ya_ref[:, cols] = (o * _silu(ga_ref[:, cols].astype(F32))).astype(BF16)

    return pl.pallas_call(
        body, name="attn_fwd", grid=(nblk,),
        out_shape=[jax.ShapeDtypeStruct((s, ATTN_WIDTH), BF16)] * 2,
        in_specs=[pl.BlockSpec(memory_space=pltpu.SMEM), _row_spec(QBLK, ATTN_WIDTH)] + kv_specs
                 + [_row_spec(QBLK, ATTN_WIDTH), _bias_spec(nblk)],
        out_specs=[_row_spec(QBLK, ATTN_WIDTH)] * 2,
        compiler_params=_params(("arbitrary",)),
    )(sink, qkv, qkv, qkv, qkv, ga, bias3)


def _roll_rows(a, k):
    return pltpu.roll(a, k % a.shape[0], axis=0)


def _window_sum(xe, w, mirrored):
    acc = xe + _roll_rows(xe, -1 if mirrored else 1)
    width = 2
    while width < w:
        half = width // 2
        acc = _roll_rows(acc, half) + _roll_rows(acc, -half)
        width *= 2
    return acc


def _halo_specs(tm, width, s):
    nh = tm // POOL_HALO
    last = s // POOL_HALO - 1
    prev = pl.BlockSpec((POOL_HALO, width), lambda i: (jnp.maximum(i * nh - 1, 0), 0))
    nxt = pl.BlockSpec((POOL_HALO, width), lambda i: (jnp.minimum((i + 1) * nh, last), 0))
    return [prev, _row_spec(tm, width), nxt]


def _with_halo(prev_ref, cur_ref, next_ref, i, n_tiles):
    prev = jnp.where(i > 0, prev_ref[...].astype(F32), 0.0)
    nxt = jnp.where(i < n_tiles - 1, next_ref[...].astype(F32), 0.0)
    return jnp.concatenate([prev, cur_ref[...].astype(F32), nxt], axis=0)


def _pool_counts(i, tm, s, w):
    pos = i * tm + lax.broadcasted_iota(jnp.int32, (tm, 1), 0)
    lo = jnp.maximum(pos - w // 2, 0)
    hi = jnp.minimum(pos + w // 2, s)
    return (hi - lo).astype(F32)


def _pool_fwd(u, gp, wp, scale):
    s, pw = u.shape
    c = pw // len(POOL_SIZES)
    tm = min(512, s)
    nt = s // tm

    def body(up, uc, un, gp_ref, wp_ref, sc_ref, yp_ref):
        i = pl.program_id(0)
        ue = _with_halo(up, uc, un, i, nt)
        for g, w in enumerate(POOL_SIZES):
            cols = slice(g * c, (g + 1) * c)
            xg = ue[:, cols]
            ws = _window_sum(xg, w, False)[POOL_HALO:POOL_HALO + tm]
            pooled = ws * (1.0 / _pool_counts(i, tm, s, w)) - xg[POOL_HALO:POOL_HALO + tm]
            mixed = jnp.dot(pooled.astype(BF16), wp_ref[g], preferred_element_type=F32)
            yp_ref[:, cols] = (mixed * sc_ref[:, cols] * _silu(gp_ref[:, cols].astype(F32))).astype(BF16)

    return pl.pallas_call(
        body, name="pool_fwd", grid=(nt,),
        out_shape=jax.ShapeDtypeStruct((s, pw), BF16),
        in_specs=_halo_specs(tm, pw, s) + [_row_spec(tm, pw), _const_spec(wp.shape), _const_spec((1, pw))],
        out_specs=_row_spec(tm, pw),
        compiler_params=_params(("arbitrary",)),
    )(u, u, u, gp, wp, scale)


def _branch_fwd(ya, yp, g, wa_blk, wp_blk):
    s = ya.shape[0]
    d = g.shape[1] // 2
    ka, kp = ya.shape[1], yp.shape[1]
    tm = min(512, s)

    def body(ya_ref, yp_ref, g_ref, wa_hbm, wp_hbm, ba_ref, bp_ref, m_ref, wa, wp, sems):
        @pl.when(pl.program_id(0) == 0)
        def _():
            _load_blocked(wa_hbm, wa, sems.at[pl.ds(0, N_DEV)], 1)
            _load_blocked(wp_hbm, wp, sems.at[pl.ds(N_DEV, N_DEV)], 1)

        for c0 in range(0, d, 512):
            cols = slice(c0, c0 + 512)
            ba = jnp.dot(ya_ref[...], wa[:, cols], preferred_element_type=F32)
            bp = jnp.dot(yp_ref[...], wp[:, cols], preferred_element_type=F32)
            ba_ref[:, cols] = ba.astype(BF16)
            bp_ref[:, cols] = bp.astype(BF16)
            ga = g_ref[:, c0:c0 + 512].astype(F32)
            gb = g_ref[:, d + c0:d + c0 + 512].astype(F32)
            m_ref[:, cols] = (ga * ba + gb * bp).astype(BF16)

    return pl.pallas_call(
        body, name="branch_fwd", grid=(s // tm,),
        out_shape=[jax.ShapeDtypeStruct((s, d), BF16)] * 3,
        in_specs=[_row_spec(tm, ka), _row_spec(tm, kp), _row_spec(tm, 2 * d), ANY_SPEC, ANY_SPEC],
        out_specs=[_row_spec(tm, d)] * 3,
        scratch_shapes=[pltpu.VMEM((ka, d), BF16), pltpu.VMEM((kp, d), BF16), pltpu.SemaphoreType.DMA((2 * N_DEV,))],
        compiler_params=_params(("arbitrary",)),
    )(ya, yp, g, wa_blk, wp_blk)


def _out_fused(merged, x2, t2, g, ba, bp, vec, wo_blk):
    s, d = x2.shape
    tm = min(256, s)

    def body(m_ref, x_ref, t_ref, g_ref, ba_ref, bp_ref, vec_ref, wo_hbm,
             do_ref, dgm_ref, dba_ref, dbp_ref, dy_ref, acc_ref, dbm_ref, wo, sems):
        @pl.when(pl.program_id(0) == 0)
        def _():
            _load_blocked(wo_hbm, wo, sems, 0)
            acc_ref[...] = jnp.zeros_like(acc_ref)
            dbm_ref[...] = jnp.zeros_like(dbm_ref)

        gate = vec_ref[0:1, :]
        post_g = vec_ref[1:2, :]
        o = jnp.dot(m_ref[...], wo[...], preferred_element_type=F32)
        r2 = lax.rsqrt(jnp.mean(o * o, axis=-1, keepdims=True) + EPS)
        on = o * r2
        z = on * post_g
        diff = x_ref[...] + gate * z - t_ref[...]
        dy = diff * (1.0 / d)
        dy_ref[...] = dy
        dz = dy * gate
        don = dz * post_g
        do = r2 * (don - on * jnp.mean(don * on, axis=-1, keepdims=True))
        acc_ref[0:1, :] += jnp.sum(diff * diff, axis=0, keepdims=True)
        acc_ref[1:2, :] += jnp.sum(dy * z, axis=0, keepdims=True)
        acc_ref[2:3, :] += jnp.sum(dz * on, axis=0, keepdims=True)
        dob = do.astype(BF16)
        do_ref[...] = dob
        for c0 in range(0, d, 512):
            cols = slice(c0, c0 + 512)
            dm = lax.dot_general(dob, wo[c0:c0 + 512, :], NT_DIMS, preferred_element_type=F32)
            ga = g_ref[:, c0:c0 + 512].astype(F32)
            gb = g_ref[:, d + c0:d + c0 + 512].astype(F32)
            b_a = ba_ref[:, cols].astype(F32)
            b_p = bp_ref[:, cols].astype(F32)
            dba_ref[:, cols] = (dm * ga).astype(BF16)
            dbp_ref[:, cols] = (dm * gb).astype(BF16)
            dga = dm * b_a * ga * (1.0 - ga)
            dgb = dm * b_p * gb * (1.0 - gb)
            dgm_ref[:, c0:c0 + 512] = dga.astype(BF16)
            dgm_ref[:, d + c0:d + c0 + 512] = dgb.astype(BF16)
            dbm_ref[:, c0:c0 + 512] += jnp.sum(dga, axis=0, keepdims=True)
            dbm_ref[:, d + c0:d + c0 + 512] += jnp.sum(dgb, axis=0, keepdims=True)

    return pl.pallas_call(
        body, name="out_fused", grid=(s // tm,),
        out_shape=[jax.ShapeDtypeStruct((s, d), BF16), jax.ShapeDtypeStruct((s, 2 * d), BF16),
                   jax.ShapeDtypeStruct((s, d), BF16), jax.ShapeDtypeStruct((s, d), BF16),
                   jax.ShapeDtypeStruct((s, d), F32), jax.ShapeDtypeStruct((8, d), F32),
                   jax.ShapeDtypeStruct((1, 2 * d), F32)],
        in_specs=[_row_spec(tm, d), _row_spec(tm, d), _row_spec(tm, d), _row_spec(tm, 2 * d),
                  _row_spec(tm, d), _row_spec(tm, d), _const_spec((8, d)), ANY_SPEC],
        out_specs=[_row_spec(tm, d), _row_spec(tm, 2 * d), _row_spec(tm, d), _row_spec(tm, d),
                   _row_spec(tm, d), _const_spec((8, d)), _const_spec((1, 2 * d))],
        scratch_shapes=[pltpu.VMEM((d, d), BF16), pltpu.SemaphoreType.DMA((N_DEV,))],
        compiler_params=_params(("arbitrary",)),
    )(merged, x2, t2, g, ba, bp, vec, wo_blk)


def _wgrad(a, b, tn, shard_w, name, comm=NO_COMM):
    s, k = a.shape
    n = b.shape[1]
    tk = min(512, s)
    ns = s // tk
    per = tn // shard_w

    def body(*refs):
        ins, outs, scratch, c_in, c_out, c_sems = _split_refs(refs, 2, 1, 1, comm)
        a_ref, b_ref = ins
        o_ref, = outs
        acc, = scratch
        step = pl.program_id(1)
        comm.run(pl.program_id(0) * ns + step, (n // tn) * ns, c_in, c_out, c_sems)

        @pl.when(step == 0)
        def _():
            acc[...] = jnp.zeros_like(acc)

        acc[...] += lax.dot_general(a_ref[...], b_ref[...], TN_DIMS, preferred_element_type=F32)

        @pl.when(step == ns - 1)
        def _():
            for r in range(per):
                o_ref[r] = acc[:, r * shard_w:(r + 1) * shard_w].astype(BF16)

    res = pl.pallas_call(
        body, name=name, grid=(n // tn, ns),
        out_shape=[jax.ShapeDtypeStruct((n // shard_w, k, shard_w), BF16)] + comm.out_shapes(),
        in_specs=[pl.BlockSpec((tk, k), lambda j, t: (t, 0)), pl.BlockSpec((tk, tn), lambda j, t: (t, j))] + comm.specs(),
        out_specs=[pl.BlockSpec((per, k, shard_w), lambda j, t: (j, 0, 0))] + comm.specs(),
        scratch_shapes=[pltpu.VMEM((k, tn), F32)] + comm.scratch(),
        compiler_params=_params(("arbitrary", "arbitrary")),
    )(a, b, *comm.arrays)
    return res[0] if comm.n == 0 else (res[0], res[1:])


def _branch_bwd(dba, dbp, a, ga, wa_blk, wp_blk, comm):
    s, d = dba.shape
    ka = wa_blk.shape[1]
    kp = wp_blk.shape[1]
    tm = min(512, s)
    nt = s // tm

    def body(*refs):
        ins, outs, scratch, c_in, c_out, c_sems = _split_refs(refs, 6, 3, 3, comm)
        dba_ref, dbp_ref, a_ref, ga_ref, wa_hbm, wp_hbm = ins
        da_ref, dga_ref, dyp_ref = outs
        wa, wp, sems = scratch
        comm.run(pl.program_id(0), nt, c_in, c_out, c_sems)

        @pl.when(pl.program_id(0) == 0)
        def _():
            _load_blocked(wa_hbm, wa, sems.at[pl.ds(0, N_DEV)], 1)
            _load_blocked(wp_hbm, wp, sems.at[pl.ds(N_DEV, N_DEV)], 1)

        for c0 in range(0, ka, 512):
            cols = slice(c0, c0 + 512)
            dya = lax.dot_general(dba_ref[...], wa[c0:c0 + 512, :], NT_DIMS, preferred_element_type=F32)
            gaf = ga_ref[:, cols].astype(F32)
            da_ref[:, cols] = (dya * _silu(gaf)).astype(BF16)
            dga_ref[:, cols] = (dya * a_ref[:, cols].astype(F32) * _dsilu(gaf)).astype(BF16)
        for c0 in range(0, kp, 512):
            dyp = lax.dot_general(dbp_ref[...], wp[c0:c0 + 512, :], NT_DIMS, preferred_element_type=F32)
            dyp_ref[:, c0:c0 + 512] = dyp.astype(BF16)

    res = pl.pallas_call(
        body, name="branch_bwd", grid=(nt,),
        out_shape=[jax.ShapeDtypeStruct((s, ka), BF16), jax.ShapeDtypeStruct((s, ka), BF16),
                   jax.ShapeDtypeStruct((s, kp), BF16)] + comm.out_shapes(),
        in_specs=[_row_spec(tm, d), _row_spec(tm, d), _row_spec(tm, ka), _row_spec(tm, ka), ANY_SPEC, ANY_SPEC]
                 + comm.specs(),
        out_specs=[_row_spec(tm, ka), _row_spec(tm, ka), _row_spec(tm, kp)] + comm.specs(),
        scratch_shapes=[pltpu.VMEM((ka, d), BF16), pltpu.VMEM((kp, d), BF16), pltpu.SemaphoreType.DMA((2 * N_DEV,))]
                       + comm.scratch(),
        compiler_params=_params(("arbitrary",)),
    )(dba, dbp, a, ga, wa_blk, wp_blk, *comm.arrays)
    return res[:3], res[3:]


def _attn_bwd(qkv, a, da, bias3, sink, comm):
    s = qkv.shape[0]
    nblk = s // QBLK
    clamp = lambda n: jnp.minimum(n, nblk - 1)
    q_spec = pl.BlockSpec((QBLK, ATTN_WIDTH), lambda n: (clamp(n), 0))
    kv_specs = _kv_specs(nblk)

    def body(*refs):
        ins, outs, scratch, c_in, c_out, c_sems = _split_refs(refs, 8, 4, 2, comm)
        sink_ref, q_ref, k0, k1, k2, a_ref, da_ref, bias_ref = ins
        dq_ref, dkv_ref, db_ref, ds_ref = outs
        acc, sacc = scratch
        n = pl.program_id(0)
        comm.run(n, nblk + 1, c_in, c_out, c_sems)

        @pl.when(n == 0)
        def _():
            acc[...] = jnp.zeros_like(acc)
            sacc[...] = jnp.zeros_like(sacc)
            db_ref[...] = jnp.zeros_like(db_ref)

        @pl.when(n < nblk)
        def _():
            kv = jnp.concatenate([k0[...], k1[...], k2[...]], axis=0)
            dks, dvs = [], []
            for k in range(N_KV_HEADS):
                kk = kv[:, k * HEAD_DIM:(k + 1) * HEAD_DIM]
                vk = kv[:, KV_WIDTH + k * HEAD_DIM:KV_WIDTH + (k + 1) * HEAD_DIM]
                heads = list(range(GQA_GROUP * k, GQA_GROUP * (k + 1)))
                stack = lambda ref: jnp.concatenate([ref[:, h * HEAD_DIM:(h + 1) * HEAD_DIM] for h in heads], axis=0)
                qk, do, o = stack(q_ref), stack(da_ref), stack(a_ref)
                bias = jnp.concatenate([bias_ref[0, h] for h in heads], axis=0)
                sink = jnp.concatenate([jnp.full((QBLK, 1), sink_ref[h], F32) for h in heads], axis=0)
                logits = lax.dot_general(qk, kk, NT_DIMS, preferred_element_type=F32) * ATTN_SCALE + bias
                mx = jnp.maximum(jnp.max(logits, axis=-1, keepdims=True), sink)
                e = jnp.exp(logits - mx)
                e_sink = jnp.exp(sink - mx)
                inv = 1.0 / (jnp.sum(e, axis=-1, keepdims=True) + e_sink)
                p = e * inv
                dp = lax.dot_general(do, vk, NT_DIMS, preferred_element_type=F32)
                delta = jnp.sum(do.astype(F32) * o.astype(F32), axis=-1, keepdims=True)
                dlog = p * (dp - delta)
                d_sink = -(e_sink * inv * delta)
                dlb = dlog.astype(BF16)
                dq = jnp.dot(dlb, kk, preferred_element_type=F32) * ATTN_SCALE
                for g, h in enumerate(heads):
                    rows = slice(g * QBLK, (g + 1) * QBLK)
                    sacc[h] += d_sink[rows]
                    db_ref[h] += dlog[rows]
                    dq_ref[:, h * HEAD_DIM:(h + 1) * HEAD_DIM] = dq[rows].astype(BF16)
                dks.append(lax.dot_general(dlb, qk, TN_DIMS, preferred_element_type=F32))
                dvs.append(lax.dot_general(p.astype(BF16), do, TN_DIMS, preferred_element_type=F32))
            contrib = jnp.concatenate(dks + dvs, axis=1)
            acc[n % 3] += contrib[0:QBLK]
            acc[(n + 1) % 3] += contrib[QBLK:2 * QBLK]
            acc[(n + 2) % 3] = contrib[2 * QBLK:3 * QBLK]

        col = lax.broadcasted_iota(jnp.int32, (1, 2 * KV_WIDTH), 1)
        dkv_ref[...] = (acc[n % 3] * jnp.where(col < KV_WIDTH, ATTN_SCALE, 1.0)).astype(BF16)

        @pl.when(n == nblk)
        def _():
            for h in range(N_Q_HEADS):
                ds_ref[h:h + 1, :] = jnp.full((1, 128), jnp.sum(sacc[h]), F32)

    res = pl.pallas_call(
        body, name="attn_bwd", grid=(nblk + 1,),
        out_shape=[jax.ShapeDtypeStruct((s, ATTN_WIDTH), BF16), jax.ShapeDtypeStruct((s, 2 * KV_WIDTH), BF16),
                   jax.ShapeDtypeStruct((N_Q_HEADS, QBLK, SPAN), F32),
                   jax.ShapeDtypeStruct((N_Q_HEADS, 128), F32)] + comm.out_shapes(),
        in_specs=[pl.BlockSpec(memory_space=pltpu.SMEM), q_spec] + kv_specs
                 + [q_spec, q_spec, _bias_spec(nblk)] + comm.specs(),
        out_specs=[q_spec, pl.BlockSpec((QBLK, 2 * KV_WIDTH), lambda n: (jnp.maximum(n - 1, 0), 0)),
                   _const_spec((N_Q_HEADS, QBLK, SPAN)), _const_spec((N_Q_HEADS, 128))] + comm.specs(),
        scratch_shapes=[pltpu.VMEM((3, QBLK, 2 * KV_WIDTH), F32), pltpu.VMEM((N_Q_HEADS, QBLK, 1), F32)]
                       + comm.scratch(),
        compiler_params=_params(("arbitrary",)),
    )(sink, qkv, qkv, qkv, qkv, a, da, bias3, *comm.arrays)
    return res[:4], res[4:]


def _pool_bwd_rows(u, gp, dyp, wp, scale):
    s, pw = u.shape
    c = pw // len(POOL_SIZES)
    tm = min(512, s)
    nt = s // tm

    def body(up, uc, un, gp_ref, dyp_ref, wp_ref, sc_ref, e_ref, dgp_ref, gwp_ref, dsc_ref):
        i = pl.program_id(0)

        @pl.when(i == 0)
        def _():
            gwp_ref[...] = jnp.zeros_like(gwp_ref)
            dsc_ref[...] = jnp.zeros_like(dsc_ref)

        ue = _with_halo(up, uc, un, i, nt)
        for g, w in enumerate(POOL_SIZES):
            cols = slice(g * c, (g + 1) * c)
            xg = ue[:, cols]
            inv_cnt = 1.0 / _pool_counts(i, tm, s, w)
            ws = _window_sum(xg, w, False)[POOL_HALO:POOL_HALO + tm]
            pooled = (ws * inv_cnt - xg[POOL_HALO:POOL_HALO + tm]).astype(BF16)
            mixed = jnp.dot(pooled, wp_ref[g], preferred_element_type=F32)
            sc = sc_ref[:, cols]
            gpf = gp_ref[:, cols].astype(F32)
            dyp_g = dyp_ref[:, cols].astype(F32)
            dp = dyp_g * _silu(gpf)
            dgp_ref[:, cols] = (dyp_g * (mixed * sc) * _dsilu(gpf)).astype(BF16)
            dsc_ref[:, cols] += jnp.sum(dp * mixed, axis=0, keepdims=True)
            dmixed = (dp * sc).astype(BF16)
            dpooled = lax.dot_general(dmixed, wp_ref[g], NT_DIMS, preferred_element_type=F32)
            gwp_ref[g] += lax.dot_general(pooled, dmixed, TN_DIMS, preferred_element_type=F32)
            e_ref[:, cols] = dpooled * inv_cnt

    return pl.pallas_call(
        body, name="pool_bwd_rows", grid=(nt,),
        out_shape=[jax.ShapeDtypeStruct((s, pw), F32), jax.ShapeDtypeStruct((s, pw), BF16),
                   jax.ShapeDtypeStruct(wp.shape, F32), jax.ShapeDtypeStruct((1, pw), F32)],
        in_specs=_halo_specs(tm, pw, s) + [_row_spec(tm, pw), _row_spec(tm, pw), _const_spec(wp.shape), _const_spec((1, pw))],
        out_specs=[_row_spec(tm, pw), _row_spec(tm, pw), _const_spec(wp.shape), _const_spec((1, pw))],
        compiler_params=_params(("arbitrary",)),
    )(u, u, u, gp, dyp, wp, scale)


def _pool_bwd_window(e):
    s, pw = e.shape
    c = pw // len(POOL_SIZES)
    tm = min(512, s)
    nt = s // tm

    def body(ep, ec, en, du_ref):
        i = pl.program_id(0)
        ee = _with_halo(ep, ec, en, i, nt)
        for g, w in enumerate(POOL_SIZES):
            cols = slice(g * c, (g + 1) * c)
            eg = ee[:, cols]
            ws = _window_sum(eg, w, True)[POOL_HALO:POOL_HALO + tm]
            du_ref[:, cols] = (ws - eg[POOL_HALO:POOL_HALO + tm] * _pool_counts(i, tm, s, w)).astype(BF16)

    return pl.pallas_call(
        body, name="pool_bwd_window", grid=(nt,),
        out_shape=jax.ShapeDtypeStruct((s, pw), BF16),
        in_specs=_halo_specs(tm, pw, s), out_specs=_row_spec(tm, pw),
        compiler_params=_params(("arbitrary",)),
    )(e, e, e)


def _dh_merge(dgm, pieces, w_blk, w_full, comm):
    s, n = dgm.shape
    d = w_blk.shape[1]
    tm = min(512, s)
    nt = s // tm
    widths = [p.shape[1] for p in pieces]
    n_p = len(pieces)
    wcols = sum(widths)

    def body(*refs):
        ins, outs, scratch, c_in, c_out, c_sems = _split_refs(refs, 3 + n_p, 1, 4, comm)
        dg_ref = ins[0]
        p_refs = ins[1:1 + n_p]
        w_hbm, wi_hbm = ins[1 + n_p:]
        o_ref, = outs
        w_vmem, wi_vmem, sems, sem = scratch
        comm.run(pl.program_id(0), nt, c_in, c_out, c_sems)

        @pl.when(pl.program_id(0) == 0)
        def _():
            cp = pltpu.make_async_copy(wi_hbm.at[:, pl.ds(0, wcols)], wi_vmem, sem)
            cp.start()
            _load_blocked(w_hbm, w_vmem, sems, 1)
            cp.wait()

        for r0 in range(0, d, 512):
            dh = lax.dot_general(dg_ref[...], w_vmem[r0:r0 + 512, :], NT_DIMS, preferred_element_type=F32)
            off = 0
            for p_ref, wd in zip(p_refs, widths):
                dh = dh + lax.dot_general(p_ref[...], wi_vmem[r0:r0 + 512, off:off + wd], NT_DIMS,
                                          preferred_element_type=F32)
                off += wd
            o_ref[:, r0:r0 + 512] = dh

    res = pl.pallas_call(
        body, name="dh_merge", grid=(nt,),
        out_shape=[jax.ShapeDtypeStruct((s, d), F32)] + comm.out_shapes(),
        in_specs=[_row_spec(tm, n)] + [_row_spec(tm, wd) for wd in widths] + [ANY_SPEC, ANY_SPEC] + comm.specs(),
        out_specs=[_row_spec(tm, d)] + comm.specs(),
        scratch_shapes=[pltpu.VMEM((d, n), BF16), pltpu.VMEM((d, wcols), BF16), pltpu.SemaphoreType.DMA((N_DEV,)),
                        pltpu.SemaphoreType.DMA] + comm.scratch(),
        compiler_params=_params(("arbitrary",)),
    )(dgm, *pieces, w_blk, w_full, *comm.arrays)
    return res[0], res[1:]


def _dh_final(dh1, pieces, w_full, x2, dy, vec):
    s, d = x2.shape
    tm = min(256, s)
    widths = [p.shape[1] for p in pieces]
    n_p = len(pieces)
    wcols = sum(widths)
    comm = NO_COMM

    def body(*refs):
        ins, outs, scratch, _, _, _ = _split_refs(refs, 5 + n_p, 2, 2, comm)
        dh1_ref = ins[0]
        p_refs = ins[1:1 + n_p]
        w_hbm, x_ref, dy_ref, vec_ref = ins[1 + n_p:]
        gx_ref, acc_ref = outs
        w_vmem, sem = scratch

        @pl.when(pl.program_id(0) == 0)
        def _():
            cp = pltpu.make_async_copy(w_hbm.at[:, pl.ds(w_full.shape[1] - wcols, wcols)], w_vmem, sem)
            cp.start()
            cp.wait()
            acc_ref[...] = jnp.zeros_like(acc_ref)

        scale1 = vec_ref[0:1, :]
        pre_g = vec_ref[2:3, :]
        xf = x_ref[...]
        r = lax.rsqrt(jnp.mean(xf * xf, axis=-1, keepdims=True) + EPS)
        xn = xf * r
        hn = xn * pre_g
        for r0 in range(0, d, 512):
            rows = slice(r0, r0 + 512)
            dh = dh1_ref[:, rows]
            off = 0
            for p_ref, wd in zip(p_refs, widths):
                dh = dh + lax.dot_general(p_ref[...], w_vmem[r0:r0 + 512, off:off + wd], NT_DIMS,
                                          preferred_element_type=F32)
                off += wd
            acc_ref[0:1, rows] += jnp.sum(dh, axis=0, keepdims=True)
            acc_ref[1:2, rows] += jnp.sum(dh * hn[:, rows], axis=0, keepdims=True)
            dhn = dh * scale1[:, rows]
            acc_ref[2:3, rows] += jnp.sum(dhn * xn[:, rows], axis=0, keepdims=True)
            gx_ref[:, rows] = dhn * pre_g[:, rows]
        dxn = gx_ref[...]
        dx = r * (dxn - xn * jnp.mean(dxn * xn, axis=-1, keepdims=True))
        gx_ref[...] = dy_ref[...] + dx

    return pl.pallas_call(
        body, name="dh_final", grid=(s // tm,),
        out_shape=[jax.ShapeDtypeStruct((s, d), F32), jax.ShapeDtypeStruct((8, d), F32)],
        in_specs=[_row_spec(tm, d)] + [_row_spec(tm, wd) for wd in widths]
                 + [ANY_SPEC, _row_spec(tm, d), _row_spec(tm, d), _const_spec((8, d))],
        out_specs=[_row_spec(tm, d), _const_spec((8, d))],
        scratch_shapes=[pltpu.VMEM((d, wcols), BF16), pltpu.SemaphoreType.DMA],
        compiler_params=_params(("arbitrary",)),
    )(dh1, *pieces, w_full, x2, dy, vec)


def _reduce_adamw(parts, w, m, v, name):
    r, c = w.shape
    n_parts = len(parts)
    rp = r // n_parts
    tr = min(256, rp)
    per = rp // tr

    def body(*refs):
        p_refs = refs[:n_parts]
        w_ref, m_ref, v_ref, g_ref, dl_ref, m2_ref, v2_ref = refs[n_parts:]
        i = pl.program_id(0)
        for k, p_ref in enumerate(p_refs):
            @pl.when((i >= k * per) & (i < (k + 1) * per))
            def _(p_ref=p_ref):
                g = p_ref[0].astype(F32)
                for dev in range(1, N_DEV):
                    g = g + p_ref[dev].astype(F32)
                g_ref[...] = g
                dl_ref[...], m2_ref[...], v2_ref[...] = _adamw(w_ref[...], g, m_ref[...], v_ref[...])

    spec = _row_spec(tr, c)
    part_specs = [pl.BlockSpec((N_DEV, tr, c), (lambda i, k=k: (0, jnp.clip(i - k * per, 0, per - 1), 0)))
                  for k in range(n_parts)]
    return pl.pallas_call(
        body, name=name, grid=(r // tr,),
        out_shape=[jax.ShapeDtypeStruct((r, c), F32)] * 4,
        in_specs=part_specs + [spec, spec, spec],
        out_specs=[spec] * 4, compiler_params=_params(("arbitrary",)),
    )(*parts, w, m, v)


def _small_reduce_adamw(gathered, w, m, v):
    rows, c = w.shape

    def body(p_ref, w_ref, m_ref, v_ref, g_ref, dl_ref, m2_ref, v2_ref):
        g = p_ref[0:rows, :]
        for dev in range(1, N_DEV):
            g = g + p_ref[dev * rows:(dev + 1) * rows, :]
        g_ref[...] = g
        dl_ref[...], m2_ref[...], v2_ref[...] = _adamw(w_ref[...], g, m_ref[...], v_ref[...])

    vm = pl.BlockSpec(memory_space=pltpu.VMEM)
    return pl.pallas_call(
        body, name="small_reduce_adamw", out_shape=[jax.ShapeDtypeStruct((rows, c), F32)] * 4,
        in_specs=[vm] * 4, out_specs=[vm] * 4, compiler_params=_params(),
    )(gathered, w, m, v)


def _pack_small(parts, total):
    flat = jnp.concatenate([p.reshape(-1).astype(F32) for p in parts])
    flat = jnp.pad(flat, (0, total - flat.shape[0]))
    return flat.reshape(8, total // 8)


def _unpack_small(packed, shapes):
    flat = packed.reshape(-1)
    out, off = [], 0
    for shp in shapes:
        size = math.prod(shp)
        out.append(flat[off:off + size].reshape(shp))
        off += size
    return out


def kernel(x, c, rel_bias_table, w_ada, b_ada, pre_norm_g, post_norm_g, w_in, attn_sink, w_pool_group, pool_scale, w_branch_attn, w_branch_pool, w_merge, b_merge, w_out, loss_target, m_rel_bias_table, m_w_ada, m_b_ada, m_pre_norm_g, m_post_norm_g, m_w_in, m_attn_sink, m_w_pool_group, m_pool_scale, m_w_branch_attn, m_w_branch_pool, m_w_merge, m_b_merge, m_w_out, v_rel_bias_table, v_w_ada, v_b_ada, v_pre_norm_g, v_post_norm_g, v_w_in, v_attn_sink, v_w_pool_group, v_pool_scale, v_w_branch_attn, v_w_branch_pool, v_w_merge, v_b_merge, v_w_out):
    xi, yi, ci = lax.axis_index("x"), lax.axis_index("y"), lax.axis_index("c")
    me = 4 * xi + 2 * yi + ci
    x2, t2 = x[0], loss_target[0]
    s, d = x2.shape
    pw = d // 2
    n_groups = len(POOL_SIZES)
    cg = pw // n_groups
    in_w = w_in.shape[2] * N_DEV
    ada_cols = w_ada.shape[2]

    c_all = _allgather_small(jnp.broadcast_to(c, (8, d)), "gather_c").reshape(N_DEV, 8, d)[:, 0, :]
    c16 = jnp.concatenate([c_all, jnp.zeros_like(c_all)], axis=0)
    b_cols = lax.dynamic_slice_in_dim(b_ada, me * ada_cols, ada_cols, axis=1)
    mod_part = _ada_fwd(c16, w_ada[0], b_cols)[:8]
    mod_all = _allgather_small(mod_part, "gather_mod").reshape(N_DEV, 8, ada_cols)
    mod = lax.dynamic_index_in_dim(mod_all, me, axis=1, keepdims=False).reshape(3 * d)
    shift, scale, gate = mod[:d], mod[d:2 * d], mod[2 * d:]
    zrow = jnp.zeros((d,), F32)
    vec_in = jnp.stack([1.0 + scale, shift, pre_norm_g[0], zrow, zrow, zrow, zrow, zrow])
    vec_out = jnp.stack([gate, post_norm_g[0], zrow, zrow, zrow, zrow, zrow, zrow])

    rows_per = w_pool_group.shape[2]
    late_shards = [w.astype(BF16) for w in (w_branch_attn[0], w_branch_pool[0], w_out[0],
                                            w_pool_group[0].reshape(n_groups * rows_per, cg))]
    w_in_blk, = _allgather_hbm([w_in[0].astype(BF16)], "gather_w_in")
    w_in_full = w_in_blk.transpose(1, 0, 2).reshape(d, in_w)

    rel = jnp.arange(SPAN)[None, :] - WINDOW - jnp.arange(QBLK)[:, None]
    idx = jnp.where(jnp.abs(rel) <= WINDOW, _t5_bucket(rel), -1).astype(jnp.int32).reshape(1, QBLK * SPAN)
    bias = _bias_variants(_bias_build(rel_bias_table.T, idx).reshape(N_Q_HEADS, QBLK, SPAN))
    sink = attn_sink[0]

    (h, qkv, ga, u, gp), (w_merge_blk,) = _fwd_in(x2, vec_in, w_in_full, _Carried("gather", [w_merge[0].astype(BF16)]))
    g, (w_ba_blk, w_bp_blk, w_out_blk, w_pool_blk) = _merge_fwd(h, w_merge_blk, b_merge, _Carried("gather", late_shards))
    w_pool_full = w_pool_blk.reshape(N_DEV, n_groups, rows_per, cg).transpose(1, 0, 2, 3).reshape(n_groups, cg, cg)
    a, ya = _attn_fwd(qkv, ga, bias, sink)
    yp = _pool_fwd(u, gp, w_pool_full, pool_scale)
    ba, bp, merged = _branch_fwd(ya, yp, g, w_ba_blk, w_bp_blk)
    do, dgm, dba, dbp, dy, acc_out, db_merge = _out_fused(merged, x2, t2, g, ba, bp, vec_out, w_out_blk)
    loss = lax.psum(0.5 * jnp.sum(acc_out[0]) / d, MESH_AXES)

    gw_out = _wgrad(merged, do, d, d, "wgrad_out")[0].reshape(N_DEV, d // N_DEV, d)
    gw_ba = _wgrad(ya, dba, d, d // N_DEV, "wgrad_branch_attn")
    gw_bp = _wgrad(yp, dbp, d, d // N_DEV, "wgrad_branch_pool")
    gw_merge, (r_out,) = _wgrad(h, dgm, min(2 * d, 2048), 2 * d // N_DEV, "wgrad_merge",
                                _Carried("exchange", [gw_out]))
    (da, dga, dyp), (r_ba, r_bp) = _branch_bwd(dba, dbp, a, ga, w_ba_blk, w_bp_blk,
                                               _Carried("exchange", [gw_ba, gw_bp]))
    (dq, dkv, dbias, dsink), (r_merge,) = _attn_bwd(qkv, a, da, bias, sink, _Carried("exchange", [gw_merge]))
    e, dgp, gw_pool, dpool_scale = _pool_bwd_rows(u, gp, dyp, w_pool_full, pool_scale)
    du = _pool_bwd_window(e)
    pieces = [dq, dkv, dga, du, dgp]
    gw_in = jnp.concatenate([_wgrad(h, p, p.shape[1], p.shape[1], "wgrad_in_%d" % i)[0]
                             for i, p in enumerate(pieces)], axis=1)
    gw_in_blk = gw_in.reshape(d, N_DEV, in_w // N_DEV).transpose(1, 0, 2)
    gw_pool_blk = gw_pool.astype(BF16).reshape(n_groups, N_DEV, rows_per, cg).transpose(1, 0, 2, 3).reshape(
        N_DEV, n_groups * rows_per, cg)
    dh1, (r_in, r_pool) = _dh_merge(dgm, pieces[:3], w_merge_blk, w_in_full, _Carried("exchange", [gw_in_blk, gw_pool_blk]))
    gx, acc_in = _dh_final(dh1, pieces[3:], w_in_full, x2, dy, vec_in)
    d_table = _bias_reduce(dbias.reshape(N_Q_HEADS, QBLK * SPAN), idx).T

    recv = [[r_in], [r_merge], [r_ba], [r_bp], [r_out], [r_pool]]
    names = ["w_in", "w_merge", "w_branch_attn", "w_branch_pool", "w_out", "w_pool_group"]
    big_w = [w_in, w_merge, w_branch_attn, w_branch_pool, w_out, w_pool_group]
    big_m = [m_w_in, m_w_merge, m_w_branch_attn, m_w_branch_pool, m_w_out, m_w_pool_group]
    big_v = [v_w_in, v_w_merge, v_w_branch_attn, v_w_branch_pool, v_w_out, v_w_pool_group]
    big = {}
    for nm, rc, w_, m_, v_ in zip(names, recv, big_w, big_m, big_v):
        shp2 = (len(rc) * rc[0].shape[1], rc[0].shape[2])
        outs = _reduce_adamw(rc, w_.reshape(shp2), m_.reshape(shp2), v_.reshape(shp2), "adamw_" + nm)
        big[nm] = [o.reshape(w_.shape) for o in outs]

    dmod = jnp.concatenate([acc_in[0], acc_in[1], acc_out[1]])
    small_names = ["b_ada", "pre_norm_g", "post_norm_g", "pool_scale", "b_merge", "rel_bias_table", "attn_sink"]
    small_g = [dmod, acc_in[2], acc_out[2], dpool_scale, db_merge, d_table, dsink[:, 0]]
    small_w = [b_ada, pre_norm_g, post_norm_g, pool_scale, b_merge, rel_bias_table, attn_sink]
    small_m = [m_b_ada, m_pre_norm_g, m_post_norm_g, m_pool_scale, m_b_merge, m_rel_bias_table, m_attn_sink]
    small_v = [v_b_ada, v_pre_norm_g, v_post_norm_g, v_pool_scale, v_b_merge, v_rel_bias_table, v_attn_sink]
    n_small = sum(math.prod(w_.shape) for w_ in small_w)
    total = -(-n_small // 1024) * 1024
    gathered = _allgather_small(_pack_small(small_g, total), "gather_small_grads")
    sm = _small_reduce_adamw(gathered, _pack_small(small_w, total), _pack_small(small_m, total),
                             _pack_small(small_v, total))
    shapes = [w_.shape for w_ in small_w]
    small = {nm: vals for nm, vals in zip(small_names, zip(*[_unpack_small(o, shapes) for o in sm]))}

    dmod_all = gathered.reshape(N_DEV, -1)[:, :3 * d]
    dmod_mine = lax.dynamic_slice_in_dim(dmod_all, me * ada_cols, ada_cols, axis=1)
    dmod16 = jnp.concatenate([dmod_mine, jnp.zeros_like(dmod_mine)], axis=0)
    ada = [o[None] for o in _ada_bwd(c16, dmod16, w_ada[0], m_w_ada[0], v_w_ada[0])]

    order = ["rel_bias_table", "w_ada", "b_ada", "pre_norm_g", "post_norm_g", "w_in", "attn_sink", "w_pool_group",
             "pool_scale", "w_branch_attn", "w_branch_pool", "w_merge", "b_merge", "w_out"]
    results = dict(big)
    results.update({nm: list(vals) for nm, vals in small.items()})
    results["w_ada"] = ada
    outs = [loss, gx[None]]
    for k in range(4):
        outs += [results[nm][k] for nm in order]
    return tuple(outs)
```

```python
import functools
import math

import jax
import jax.numpy as jnp
from jax import lax
from jax.experimental import pallas as pl
from jax.experimental.pallas import tpu as pltpu

F32 = jnp.float32
BF16 = jnp.bfloat16

HEAD_DIM = 128
N_Q_HEADS = 8
N_KV_HEADS = 2
GQA_GROUP = N_Q_HEADS // N_KV_HEADS
ATTN_WIDTH = N_Q_HEADS * HEAD_DIM
KV_WIDTH = N_KV_HEADS * HEAD_DIM
WINDOW = 128
QBLK = 128
SPAN = QBLK + 2 * WINDOW
N_BUCKETS = 32
MAX_DISTANCE = 128
POOL_SIZES = (2, 4, 8, 16)
EPS = 1e-6
NEG_INF = -1e30
ATTN_SCALE = HEAD_DIM ** -0.5

ADAM_LR = 0.001
ADAM_B1 = 0.9
ADAM_B2 = 0.999
ADAM_EPS = 1e-08
ADAM_WD = 0.01
ADAM_STEP = 10

N_DEV = 8
MESH_AXES = ("x", "y", "c")
MESH = pl.DeviceIdType.MESH
POOL_HALO = 16
V7X_VMEM_LIMIT = 56 * 1024 * 1024

NT_DIMS = (((1,), (1,)), ((), ()))
TN_DIMS = (((0,), (0,)), ((), ()))


def _params(sem=None, vmem=V7X_VMEM_LIMIT):
    return pltpu.CompilerParams(dimension_semantics=sem, vmem_limit_bytes=vmem)


def _silu(v):
    return v * jax.nn.sigmoid(v)


def _dsilu(v):
    s = jax.nn.sigmoid(v)
    return s * (1.0 + v * (1.0 - s))


def _row_spec(tm, width):
    return pl.BlockSpec((tm, width), lambda i: (i, 0))


def _const_spec(shape):
    return pl.BlockSpec(shape, lambda *_: (0,) * len(shape))


ANY_SPEC = pl.BlockSpec(memory_space=pl.ANY)


def _load_blocked(w_blk, w_vmem, sems, axis):
    n = w_blk.shape[0]
    width = w_blk.shape[1 + axis]
    copies = []
    for d in range(n):
        dst = w_vmem.at[pl.ds(d * width, width), :] if axis == 0 else w_vmem.at[:, pl.ds(d * width, width)]
        copies.append(pltpu.make_async_copy(w_blk.at[d], dst, sems.at[d]))
    for cp in copies:
        cp.start()
    for cp in copies:
        cp.wait()


def _mesh_pos():
    return lax.axis_index("x"), lax.axis_index("y"), lax.axis_index("c")


def _allgather_small(blk, name):
    m_per, n = blk.shape

    def body(x_ref, out_ref, send_sems, recv_sems, local_sem):
        x, y, c = _mesh_pos()
        me, sibling = (x, y, c), (x, y, 1 - c)
        chips = [(1 - x, y), (x, 1 - y), (1 - x, 1 - y)]

        def rows(px, py, pc):
            return out_ref.at[pl.ds((4 * px + 2 * py + pc) * m_per, m_per), :]

        def copy(k, block, to, src=None):
            return pltpu.make_async_remote_copy(
                src_ref=rows(*block) if src is None else src, dst_ref=rows(*block),
                send_sem=send_sems.at[k], recv_sem=recv_sems.at[k], device_id=to, device_id_type=MESH)

        mine = pltpu.make_async_copy(x_ref, rows(*me), local_sem)
        mine.start()
        first = [copy(0, me, sibling, src=x_ref)]
        first += [copy(1 + j, me, (*chip, c), src=x_ref) for j, chip in enumerate(chips)]
        for cp in first:
            cp.start()
        passed = [copy(4 + j, (*chip, c), sibling) for j, chip in enumerate(chips)]
        for j, chip in enumerate(chips):
            copy(1 + j, (*chip, c), me).wait_recv()
            passed[j].start()
        copy(0, sibling, me).wait_recv()
        for j, chip in enumerate(chips):
            copy(4 + j, (*chip, 1 - c), me).wait_recv()
        for cp in first + passed:
            cp.wait_send()
        mine.wait()

    return pl.pallas_call(
        body, name=name,
        out_shape=jax.ShapeDtypeStruct((N_DEV * m_per, n), blk.dtype),
        in_specs=[pl.BlockSpec(memory_space=pltpu.VMEM)],
        out_specs=pl.BlockSpec(memory_space=pltpu.VMEM),
        scratch_shapes=[pltpu.SemaphoreType.DMA((7,)), pltpu.SemaphoreType.DMA((7,)), pltpu.SemaphoreType.DMA],
    )(blk)


def _gather_copy(outs, send_sems, recv_sems, i, k, block, to, src=None):
    slot = outs[i].at[4 * block[0] + 2 * block[1] + block[2]]
    return pltpu.make_async_remote_copy(
        src_ref=slot if src is None else src, dst_ref=slot,
        send_sem=send_sems.at[7 * i + k], recv_sem=recv_sems.at[7 * i + k], device_id=to, device_id_type=MESH)


def _gather_places():
    x, y, c = _mesh_pos()
    return (x, y, c), (x, y, 1 - c), [(1 - x, y), (x, 1 - y), (1 - x, 1 - y)]


def _gather_start(ins, outs, send_sems, recv_sems, local_sems):
    me, sibling, chips = _gather_places()
    for i, src in enumerate(ins):
        pltpu.make_async_copy(src, outs[i].at[4 * me[0] + 2 * me[1] + me[2]], local_sems.at[i]).start()
        _gather_copy(outs, send_sems, recv_sems, i, 0, me, sibling, src=src).start()
        for j, chip in enumerate(chips):
            _gather_copy(outs, send_sems, recv_sems, i, 1 + j, me, (*chip, me[2]), src=src).start()


def _gather_forward(ins, outs, send_sems, recv_sems):
    me, sibling, chips = _gather_places()
    for i in range(len(ins)):
        for j, chip in enumerate(chips):
            _gather_copy(outs, send_sems, recv_sems, i, 1 + j, (*chip, me[2]), me).wait_recv()
            _gather_copy(outs, send_sems, recv_sems, i, 4 + j, (*chip, me[2]), sibling).start()


def _gather_finish(ins, outs, send_sems, recv_sems, local_sems):
    me, sibling, chips = _gather_places()
    for i, src in enumerate(ins):
        _gather_copy(outs, send_sems, recv_sems, i, 0, sibling, me).wait_recv()
        for j, chip in enumerate(chips):
            _gather_copy(outs, send_sems, recv_sems, i, 4 + j, (*chip, 1 - me[2]), me).wait_recv()
        _gather_copy(outs, send_sems, recv_sems, i, 0, me, sibling, src=src).wait_send()
        for j, chip in enumerate(chips):
            _gather_copy(outs, send_sems, recv_sems, i, 1 + j, me, (*chip, me[2]), src=src).wait_send()
            _gather_copy(outs, send_sems, recv_sems, i, 4 + j, (*chip, me[2]), sibling).wait_send()
        pltpu.make_async_copy(src, outs[i].at[4 * me[0] + 2 * me[1] + me[2]], local_sems.at[i]).wait()


EXCHANGE_FLIPS = [(fx, fy, fc) for fx in (0, 1) for fy in (0, 1) for fc in (0, 1)][1:]


def _exchange_copy(ins, outs, send_sems, recv_sems, i, k, arriving):
    x, y, c = _mesh_pos()
    f = EXCHANGE_FLIPS[k]
    p = (1 - x if f[0] else x, 1 - y if f[1] else y, 1 - c if f[2] else c)
    p_idx = 4 * p[0] + 2 * p[1] + p[2]
    return pltpu.make_async_remote_copy(
        src_ref=ins[i].at[p_idx], dst_ref=outs[i].at[p_idx if arriving else 4 * x + 2 * y + c],
        send_sem=send_sems.at[7 * i + k], recv_sem=recv_sems.at[7 * i + k], device_id=p, device_id_type=MESH)


def _exchange_start(ins, outs, send_sems, recv_sems, local_sems):
    x, y, c = _mesh_pos()
    me_idx = 4 * x + 2 * y + c
    for i in range(len(ins)):
        pltpu.make_async_copy(ins[i].at[me_idx], outs[i].at[me_idx], local_sems.at[i]).start()
        for k in range(7):
            _exchange_copy(ins, outs, send_sems, recv_sems, i, k, False).start()


def _exchange_finish(ins, outs, send_sems, recv_sems, local_sems):
    x, y, c = _mesh_pos()
    me_idx = 4 * x + 2 * y + c
    for i in range(len(ins)):
        for k in range(7):
            _exchange_copy(ins, outs, send_sems, recv_sems, i, k, True).wait_recv()
            _exchange_copy(ins, outs, send_sems, recv_sems, i, k, False).wait_send()
        pltpu.make_async_copy(ins[i].at[me_idx], outs[i].at[me_idx], local_sems.at[i]).wait()


class _Carried:
    def __init__(self, kind, arrays):
        self.kind = kind
        self.arrays = list(arrays)
        self.n = len(self.arrays)

    def specs(self):
        return [ANY_SPEC] * self.n

    def out_shapes(self):
        lead = (N_DEV,) if self.kind == "gather" else ()
        return [jax.ShapeDtypeStruct(lead + a.shape, a.dtype) for a in self.arrays]

    def scratch(self):
        return [pltpu.SemaphoreType.DMA((7 * self.n,)), pltpu.SemaphoreType.DMA((7 * self.n,)),
                pltpu.SemaphoreType.DMA((self.n,))]

    def run(self, step, n_steps, ins, outs, sems):
        send_sems, recv_sems, local_sems = sems
        if self.n == 0:
            return

        @pl.when(step == 0)
        def _():
            if self.kind == "gather":
                _gather_start(ins, outs, send_sems, recv_sems, local_sems)
            else:
                _exchange_start(ins, outs, send_sems, recv_sems, local_sems)

        if self.kind == "gather":
            @pl.when(step == n_steps // 2)
            def _():
                _gather_forward(ins, outs, send_sems, recv_sems)

        @pl.when(step == n_steps - 1)
        def _():
            if self.kind == "gather":
                _gather_finish(ins, outs, send_sems, recv_sems, local_sems)
            else:
                _exchange_finish(ins, outs, send_sems, recv_sems, local_sems)


NO_COMM = _Carried("exchange", [])


def _split_refs(refs, n_in, n_out, n_scratch, comm):
    a = n_in
    b = a + comm.n
    c = b + n_out
    d = c + comm.n
    e = d + n_scratch
    return refs[:a], refs[b:c], refs[d:e], refs[a:b], refs[c:d], refs[e:]


def _allgather_hbm(shards, name):
    n_arr = len(shards)

    def body(*refs):
        ins, outs = refs[:n_arr], refs[n_arr:2 * n_arr]
        send_sems, recv_sems, local_sems = refs[2 * n_arr:]
        _gather_start(ins, outs, send_sems, recv_sems, local_sems)
        _gather_forward(ins, outs, send_sems, recv_sems)
        _gather_finish(ins, outs, send_sems, recv_sems, local_sems)

    return pl.pallas_call(
        body, name=name,
        out_shape=[jax.ShapeDtypeStruct((N_DEV,) + s.shape, s.dtype) for s in shards],
        in_specs=[ANY_SPEC] * n_arr,
        out_specs=[ANY_SPEC] * n_arr,
        scratch_shapes=[pltpu.SemaphoreType.DMA((7 * n_arr,)), pltpu.SemaphoreType.DMA((7 * n_arr,)),
                        pltpu.SemaphoreType.DMA((n_arr,))],
    )(*shards)


def _ada_fwd(c16, w_shard, b_cols):
    rows, d = c16.shape
    nc = w_shard.shape[1]

    def body(c_ref, w_ref, b_ref, o_ref):
        sc = _silu(c_ref[...]).astype(BF16)
        o_ref[...] = jnp.dot(sc, w_ref[...].astype(BF16), preferred_element_type=F32) + b_ref[...]

    return pl.pallas_call(
        body, name="ada_fwd", out_shape=jax.ShapeDtypeStruct((rows, nc), F32),
        in_specs=[pl.BlockSpec(memory_space=pltpu.VMEM)] * 3, out_specs=pl.BlockSpec(memory_space=pltpu.VMEM),
        compiler_params=_params(),
    )(c16, w_shard, b_cols)


def _adamw(w, g, m, v):
    m2 = ADAM_B1 * m + (1.0 - ADAM_B1) * g
    v2 = ADAM_B2 * v + (1.0 - ADAM_B2) * (g * g)
    m_hat = m2 * (1.0 / (1.0 - ADAM_B1 ** ADAM_STEP))
    v_hat = v2 * (1.0 / (1.0 - ADAM_B2 ** ADAM_STEP))
    delta = -ADAM_LR * (m_hat / (jnp.sqrt(v_hat) + ADAM_EPS) + ADAM_WD * w)
    return delta, m2, v2


def _ada_bwd(c16, dmod16, w, m, v):
    rows, d = c16.shape
    nc = w.shape[1]
    tr = min(512, d)

    def body(c_ref, dm_ref, w_ref, m_ref, v_ref, g_ref, dl_ref, m2_ref, v2_ref):
        sc = _silu(c_ref[...]).astype(BF16)
        g = lax.dot_general(sc, dm_ref[...].astype(BF16), TN_DIMS, preferred_element_type=F32)
        g_ref[...] = g
        dl_ref[...], m2_ref[...], v2_ref[...] = _adamw(w_ref[...], g, m_ref[...], v_ref[...])

    spec = _row_spec(tr, nc)
    return pl.pallas_call(
        body, name="ada_bwd", grid=(d // tr,),
        out_shape=[jax.ShapeDtypeStruct((d, nc), F32)] * 4,
        in_specs=[pl.BlockSpec((rows, tr), lambda i: (0, i)), _const_spec((rows, nc)), spec, spec, spec],
        out_specs=[spec] * 4, compiler_params=_params(("arbitrary",)),
    )(c16, dmod16, w, m, v)


def _split3(v):
    hi = v.astype(BF16)
    r1 = v - hi.astype(F32)
    mid = r1.astype(BF16)
    lo = (r1 - mid.astype(F32)).astype(BF16)
    return hi, mid, lo


BIAS_CHUNK = QBLK * SPAN // 8


def _bias_build(table_t, idx):
    n = idx.shape[1]

    def body(t_ref, i_ref, o_ref):
        ids = i_ref[...]
        onehot = (lax.broadcasted_iota(jnp.int32, (N_BUCKETS, BIAS_CHUNK), 0) == ids).astype(BF16)
        t16 = jnp.concatenate([t_ref[...], jnp.zeros_like(t_ref[...])], axis=0)
        acc = jnp.zeros((2 * N_Q_HEADS, BIAS_CHUNK), F32)
        for part in _split3(t16):
            acc = acc + jnp.dot(part, onehot, preferred_element_type=F32)
        o_ref[...] = jnp.where(ids >= 0, acc[:N_Q_HEADS], NEG_INF)

    return pl.pallas_call(
        body, name="bias_build", grid=(n // BIAS_CHUNK,),
        out_shape=jax.ShapeDtypeStruct((N_Q_HEADS, n), F32),
        in_specs=[_const_spec((N_Q_HEADS, N_BUCKETS)), pl.BlockSpec((1, BIAS_CHUNK), lambda i: (0, i))],
        out_specs=pl.BlockSpec((N_Q_HEADS, BIAS_CHUNK), lambda i: (0, i)),
        compiler_params=_params(("arbitrary",)),
    )(table_t, idx)


def _bias_reduce(dbias, idx):
    n = idx.shape[1]

    def body(d_ref, i_ref, o_ref):
        @pl.when(pl.program_id(0) == 0)
        def _():
            o_ref[...] = jnp.zeros_like(o_ref)

        onehot = (lax.broadcasted_iota(jnp.int32, (N_BUCKETS, BIAS_CHUNK), 0) == i_ref[...]).astype(BF16)
        d16 = jnp.concatenate([d_ref[...], jnp.zeros_like(d_ref[...])], axis=0)
        acc = jnp.zeros((2 * N_Q_HEADS, N_BUCKETS), F32)
        for part in _split3(d16):
            acc = acc + lax.dot_general(part, onehot, NT_DIMS, preferred_element_type=F32)
        o_ref[...] += acc[:N_Q_HEADS]

    return pl.pallas_call(
        body, name="bias_reduce", grid=(n // BIAS_CHUNK,),
        out_shape=jax.ShapeDtypeStruct((N_Q_HEADS, N_BUCKETS), F32),
        in_specs=[pl.BlockSpec((N_Q_HEADS, BIAS_CHUNK), lambda i: (0, i)), pl.BlockSpec((1, BIAS_CHUNK), lambda i: (0, i))],
        out_specs=_const_spec((N_Q_HEADS, N_BUCKETS)),
        compiler_params=_params(("arbitrary",)),
    )(dbias, idx)


def _t5_bucket(rel):
    half = N_BUCKETS // 2
    max_exact = half // 2
    ret = jnp.where(rel > 0, half, 0)
    n = jnp.abs(rel)
    nf = jnp.maximum(n, 1).astype(jnp.float32)
    large = max_exact + (jnp.log(nf / max_exact) / math.log(MAX_DISTANCE / max_exact)
                         * (half - max_exact)).astype(jnp.int32)
    large = jnp.minimum(large, half - 1)
    return ret + jnp.where(n < max_exact, n, large)


def _fwd_in(x2, vec, w_full, comm):
    s, d = x2.shape
    pw = d // 2
    tm = min(512, s)
    nt = s // tm
    widths = (ATTN_WIDTH + 2 * KV_WIDTH, ATTN_WIDTH, pw, pw)
    dtypes = (BF16, BF16, F32, BF16)

    def body(*refs):
        ins, outs, scratch, c_in, c_out, c_sems = _split_refs(refs, 3, 5, 2, comm)
        x_ref, vec_ref, w_hbm = ins
        h_ref, qkv_ref, ga_ref, u_ref, gp_ref = outs
        w_vmem, sem = scratch
        comm.run(pl.program_id(0), nt, c_in, c_out, c_sems)

        @pl.when(pl.program_id(0) == 0)
        def _():
            cp = pltpu.make_async_copy(w_hbm, w_vmem, sem)
            cp.start()
            cp.wait()

        xf = x_ref[...]
        r = lax.rsqrt(jnp.mean(xf * xf, axis=-1, keepdims=True) + EPS)
        h = ((xf * r) * vec_ref[2:3, :]) * vec_ref[0:1, :] + vec_ref[1:2, :]
        hb = h.astype(BF16)
        h_ref[...] = hb
        off = 0
        for o_ref, wd in zip((qkv_ref, ga_ref, u_ref, gp_ref), widths):
            for c0 in range(0, wd, 512):
                c1 = min(c0 + 512, wd)
                o_ref[:, c0:c1] = jnp.dot(hb, w_vmem[:, off + c0:off + c1],
                                          preferred_element_type=F32).astype(o_ref.dtype)
            off += wd

    res = pl.pallas_call(
        body, name="fwd_in", grid=(nt,),
        out_shape=[jax.ShapeDtypeStruct((s, d), BF16)] + [jax.ShapeDtypeStruct((s, wd), dt) for wd, dt in zip(widths, dtypes)]
                  + comm.out_shapes(),
        in_specs=[_row_spec(tm, d), _const_spec((8, d)), ANY_SPEC] + comm.specs(),
        out_specs=[_row_spec(tm, d)] + [_row_spec(tm, wd) for wd in widths] + comm.specs(),
        scratch_shapes=[pltpu.VMEM(w_full.shape, BF16), pltpu.SemaphoreType.DMA] + comm.scratch(),
        compiler_params=_params(("arbitrary",)),
    )(x2, vec, w_full, *comm.arrays)
    return res[:5], res[5:]


def _merge_fwd(h, w_blk, b_merge, comm):
    s, d = h.shape
    n = w_blk.shape[0] * w_blk.shape[2]
    tm = min(512, s)
    nt = s // tm

    def body(*refs):
        ins, outs, scratch, c_in, c_out, c_sems = _split_refs(refs, 3, 1, 2, comm)
        h_ref, w_hbm, b_ref = ins
        g_ref, = outs
        w_vmem, sems = scratch
        comm.run(pl.program_id(0), nt, c_in, c_out, c_sems)

        @pl.when(pl.program_id(0) == 0)
        def _():
            _load_blocked(w_hbm, w_vmem, sems, 1)

        hb = h_ref[...]
        for c0 in range(0, n, 512):
            z = jnp.dot(hb, w_vmem[:, c0:c0 + 512], preferred_element_type=F32) + b_ref[:, c0:c0 + 512]
            g_ref[:, c0:c0 + 512] = jax.nn.sigmoid(z).astype(BF16)

    res = pl.pallas_call(
        body, name="merge_fwd", grid=(nt,),
        out_shape=[jax.ShapeDtypeStruct((s, n), BF16)] + comm.out_shapes(),
        in_specs=[_row_spec(tm, d), ANY_SPEC, _const_spec((1, n))] + comm.specs(),
        out_specs=[_row_spec(tm, n)] + comm.specs(),
        scratch_shapes=[pltpu.VMEM((d, n), BF16), pltpu.SemaphoreType.DMA((N_DEV,))] + comm.scratch(),
        compiler_params=_params(("arbitrary",)),
    )(h, w_blk, b_merge, *comm.arrays)
    return res[0], res[1:]


def _bias_variants(bias):
    t = lax.broadcasted_iota(jnp.int32, (1, 1, SPAN), 2)
    return jnp.stack([jnp.where(t < WINDOW, NEG_INF, bias), bias, jnp.where(t >= WINDOW + QBLK, NEG_INF, bias)])


def _bias_spec(nblk):
    def index(n):
        return (jnp.where(n == 0, 0, jnp.where(n >= nblk - 1, 2, 1)), 0, 0, 0)
    return pl.BlockSpec((1, N_Q_HEADS, QBLK, SPAN), index)


def _attn_head(q_ref, kv, bias_ref, sink_ref, h):
    k = h // GQA_GROUP
    qh = q_ref[:, h * HEAD_DIM:(h + 1) * HEAD_DIM]
    kk = kv[:, k * HEAD_DIM:(k + 1) * HEAD_DIM]
    logits = lax.dot_general(qh, kk, NT_DIMS, preferred_element_type=F32) * ATTN_SCALE + bias_ref[0, h]
    sink = sink_ref[h]
    mx = jnp.maximum(jnp.max(logits, axis=-1, keepdims=True), sink)
    e = jnp.exp(logits - mx)
    e_sink = jnp.exp(sink - mx)
    inv = 1.0 / (jnp.sum(e, axis=-1, keepdims=True) + e_sink)
    return qh, e, inv, e_sink


def _kv_specs(nblk):
    col = ATTN_WIDTH // (2 * KV_WIDTH)

    def spec(j):
        return pl.BlockSpec((QBLK, 2 * KV_WIDTH),
                            lambda n: (jnp.clip(jnp.minimum(n, nblk - 1) + j - 1, 0, nblk - 1), col))
    return [spec(j) for j in range(3)]


def _attn_fwd(qkv, ga, bias3, sink):
    s = qkv.shape[0]
    nblk = s // QBLK
    kv_specs = _kv_specs(nblk)

    def body(sink_ref, q_ref, k0, k1, k2, ga_ref, bias_ref, a_ref, ya_ref):
        kv = jnp.concatenate([k0[...], k1[...], k2[...]], axis=0)
        for h in range(N_Q_HEADS):
            k = h // GQA_GROUP
            cols = slice(h * HEAD_DIM, (h + 1) * HEAD_DIM)
            vk = kv[:, KV_WIDTH + k * HEAD_DIM:KV_WIDTH + (k + 1) * HEAD_DIM]
            _, e, inv, _ = _attn_head(q_ref, kv, bias_ref, sink_ref, h)
            o = jnp.dot(e.astype(BF16), vk, preferred_element_type=F32) * inv
            a_ref[:, cols] = o.astype(BF16)
            ya_ref[:, cols] = (o * _silu(ga_ref[:, cols].astype(F32))).astype(BF16)

    return pl.pallas_call(
        body, name="attn_fwd", grid=(nblk,),
        out_shape=[jax.ShapeDtypeStruct((s, ATTN_WIDTH), BF16)] * 2,
        in_specs=[pl.BlockSpec(memory_space=pltpu.SMEM), _row_spec(QBLK, ATTN_WIDTH)] + kv_specs
                 + [_row_spec(QBLK, ATTN_WIDTH), _bias_spec(nblk)],
        out_specs=[_row_spec(QBLK, ATTN_WIDTH)] * 2,
        compiler_params=_params(("arbitrary",)),
    )(sink, qkv, qkv, qkv, qkv, ga, bias3)


def _roll_rows(a, k):
    return pltpu.roll(a, k % a.shape[0], axis=0)


def _window_sum(xe, w, mirrored):
    acc = xe + _roll_rows(xe, -1 if mirrored else 1)
    width = 2
    while width < w:
        half = width // 2
        acc = _roll_rows(acc, half) + _roll_rows(acc, -half)
        width *= 2
    return acc


def _halo_specs(tm, width, s):
    nh = tm // POOL_HALO
    last = s // POOL_HALO - 1
    prev = pl.BlockSpec((POOL_HALO, width), lambda i: (jnp.maximum(i * nh - 1, 0), 0))
    nxt = pl.BlockSpec((POOL_HALO, width), lambda i: (jnp.minimum((i + 1) * nh, last), 0))
    return [prev, _row_spec(tm, width), nxt]


def _with_halo(prev_ref, cur_ref, next_ref, i, n_tiles):
    prev = jnp.where(i > 0, prev_ref[...].astype(F32), 0.0)
    nxt = jnp.where(i < n_tiles - 1, next_ref[...].astype(F32), 0.0)
    return jnp.concatenate([prev, cur_ref[...].astype(F32), nxt], axis=0)


def _pool_counts(i, tm, s, w):
    pos = i * tm + lax.broadcasted_iota(jnp.int32, (tm, 1), 0)
    lo = jnp.maximum(pos - w // 2, 0)
    hi = jnp.minimum(pos + w // 2, s)
    return (hi - lo).astype(F32)


def _pool_fwd(u, gp, wp, scale):
    s, pw = u.shape
    c = pw // len(POOL_SIZES)
    tm = min(512, s)
    nt = s // tm

    def body(up, uc, un, gp_ref, wp_ref, sc_ref, yp_ref):
        i = pl.program_id(0)
        ue = _with_halo(up, uc, un, i, nt)
        for g, w in enumerate(POOL_SIZES):
            cols = slice(g * c, (g + 1) * c)
            xg = ue[:, cols]
            ws = _window_sum(xg, w, False)[POOL_HALO:POOL_HALO + tm]
            pooled = ws * (1.0 / _pool_counts(i, tm, s, w)) - xg[POOL_HALO:POOL_HALO + tm]
            mixed = jnp.dot(pooled.astype(BF16), wp_ref[g], preferred_element_type=F32)
            yp_ref[:, cols] = (mixed * sc_ref[:, cols] * _silu(gp_ref[:, cols].astype(F32))).astype(BF16)

    return pl.pallas_call(
        body, name="pool_fwd", grid=(nt,),
        out_shape=jax.ShapeDtypeStruct((s, pw), BF16),
        in_specs=_halo_specs(tm, pw, s) + [_row_spec(tm, pw), _const_spec(wp.shape), _const_spec((1, pw))],
        out_specs=_row_spec(tm, pw),
        compiler_params=_params(("arbitrary",)),
    )(u, u, u, gp, wp, scale)


def _branch_fwd(ya, yp, g, wa_blk, wp_blk):
    s = ya.shape[0]
    d = g.shape[1] // 2
    ka, kp = ya.shape[1], yp.shape[1]
    tm = min(512, s)

    def body(ya_ref, yp_ref, g_ref, wa_hbm, wp_hbm, ba_ref, bp_ref, m_ref, wa, wp, sems):
        @pl.when(pl.program_id(0) == 0)
        def _():
            _load_blocked(wa_hbm, wa, sems.at[pl.ds(0, N_DEV)], 1)
            _load_blocked(wp_hbm, wp, sems.at[pl.ds(N_DEV, N_DEV)], 1)

        for c0 in range(0, d, 512):
            cols = slice(c0, c0 + 512)
            ba = jnp.dot(ya_ref[...], wa[:, cols], preferred_element_type=F32)
            bp = jnp.dot(yp_ref[...], wp[:, cols], preferred_element_type=F32)
            ba_ref[:, cols] = ba.astype(BF16)
            bp_ref[:, cols] = bp.astype(BF16)
            ga = g_ref[:, c0:c0 + 512].astype(F32)
            gb = g_ref[:, d + c0:d + c0 + 512].astype(F32)
            m_ref[:, cols] = (ga * ba + gb * bp).astype(BF16)

    return pl.pallas_call(
        body, name="branch_fwd", grid=(s // tm,),
        out_shape=[jax.ShapeDtypeStruct((s, d), BF16)] * 3,
        in_specs=[_row_spec(tm, ka), _row_spec(tm, kp), _row_spec(tm, 2 * d), ANY_SPEC, ANY_SPEC],
        out_specs=[_row_spec(tm, d)] * 3,
        scratch_shapes=[pltpu.VMEM((ka, d), BF16), pltpu.VMEM((kp, d), BF16), pltpu.SemaphoreType.DMA((2 * N_DEV,))],
        compiler_params=_params(("arbitrary",)),
    )(ya, yp, g, wa_blk, wp_blk)


def _out_fused(merged, x2, t2, g, ba, bp, vec, wo_blk):
    s, d = x2.shape
    tm = min(256, s)

    def body(m_ref, x_ref, t_ref, g_ref, ba_ref, bp_ref, vec_ref, wo_hbm,
             do_ref, dgm_ref, dba_ref, dbp_ref, dy_ref, acc_ref, dbm_ref, wo, sems):
        @pl.when(pl.program_id(0) == 0)
        def _():
            _load_blocked(wo_hbm, wo, sems, 0)
            acc_ref[...] = jnp.zeros_like(acc_ref)
            dbm_ref[...] = jnp.zeros_like(dbm_ref)

        gate = vec_ref[0:1, :]
        post_g = vec_ref[1:2, :]
        o = jnp.dot(m_ref[...], wo[...], preferred_element_type=F32)
        r2 = lax.rsqrt(jnp.mean(o * o, axis=-1, keepdims=True) + EPS)
        on = o * r2
        z = on * post_g
        diff = x_ref[...] + gate * z - t_ref[...]
        dy = diff * (1.0 / d)
        dy_ref[...] = dy
        dz = dy * gate
        don = dz * post_g
        do = r2 * (don - on * jnp.mean(don * on, axis=-1, keepdims=True))
        acc_ref[0:1, :] += jnp.sum(diff * diff, axis=0, keepdims=True)
        acc_ref[1:2, :] += jnp.sum(dy * z, axis=0, keepdims=True)
        acc_ref[2:3, :] += jnp.sum(dz * on, axis=0, keepdims=True)
        dob = do.astype(BF16)
        do_ref[...] = dob
        for c0 in range(0, d, 512):
            cols = slice(c0, c0 + 512)
            dm = lax.dot_general(dob, wo[c0:c0 + 512, :], NT_DIMS, preferred_element_type=F32)
            ga = g_ref[:, c0:c0 + 512].astype(F32)
            gb = g_ref[:, d + c0:d + c0 + 512].astype(F32)
            b_a = ba_ref[:, cols].astype(F32)
            b_p = bp_ref[:, cols].astype(F32)
            dba_ref[:, cols] = (dm * ga).astype(BF16)
            dbp_ref[:, cols] = (dm * gb).astype(BF16)
            dga = dm * b_a * ga * (1.0 - ga)
            dgb = dm * b_p * gb * (1.0 - gb)
            dgm_ref[:, c0:c0 + 512] = dga.astype(BF16)
            dgm_ref[:, d + c0:d + c0 + 512] = dgb.astype(BF16)
            dbm_ref[:, c0:c0 + 512] += jnp.sum(dga, axis=0, keepdims=True)
            dbm_ref[:, d + c0:d + c0 + 512] += jnp.sum(dgb, axis=0, keepdims=True)

    return pl.pallas_call(
        body, name="out_fused", grid=(s // tm,),
        out_shape=[jax.ShapeDtypeStruct((s, d), BF16), jax.ShapeDtypeStruct((s, 2 * d), BF16),
                   jax.ShapeDtypeStruct((s, d), BF16), jax.ShapeDtypeStruct((s, d), BF16),
                   jax.ShapeDtypeStruct((s, d), F32), jax.ShapeDtypeStruct((8, d), F32),
                   jax.ShapeDtypeStruct((1, 2 * d), F32)],
        in_specs=[_row_spec(tm, d), _row_spec(tm, d), _row_spec(tm, d), _row_spec(tm, 2 * d),
                  _row_spec(tm, d), _row_spec(tm, d), _const_spec((8, d)), ANY_SPEC],
        out_specs=[_row_spec(tm, d), _row_spec(tm, 2 * d), _row_spec(tm, d), _row_spec(tm, d),
                   _row_spec(tm, d), _const_spec((8, d)), _const_spec((1, 2 * d))],
        scratch_shapes=[pltpu.VMEM((d, d), BF16), pltpu.SemaphoreType.DMA((N_DEV,))],
        compiler_params=_params(("arbitrary",)),
    )(merged, x2, t2, g, ba, bp, vec, wo_blk)


def _wgrad(a, b, tn, shard_w, name, comm=NO_COMM):
    s, k = a.shape
    n = b.shape[1]
    tk = min(512, s)
    ns = s // tk
    per = tn // shard_w

    def body(*refs):
        ins, outs, scratch, c_in, c_out, c_sems = _split_refs(refs, 2, 1, 1, comm)
        a_ref, b_ref = ins
        o_ref, = outs
        acc, = scratch
        step = pl.program_id(1)
        comm.run(pl.program_id(0) * ns + step, (n // tn) * ns, c_in, c_out, c_sems)

        @pl.when(step == 0)
        def _():
            acc[...] = jnp.zeros_like(acc)

        acc[...] += lax.dot_general(a_ref[...], b_ref[...], TN_DIMS, preferred_element_type=F32)

        @pl.when(step == ns - 1)
        def _():
            for r in range(per):
                o_ref[r] = acc[:, r * shard_w:(r + 1) * shard_w].astype(BF16)

    res = pl.pallas_call(
        body, name=name, grid=(n // tn, ns),
        out_shape=[jax.ShapeDtypeStruct((n // shard_w, k, shard_w), BF16)] + comm.out_shapes(),
        in_specs=[pl.BlockSpec((tk, k), lambda j, t: (t, 0)), pl.BlockSpec((tk, tn), lambda j, t: (t, j))] + comm.specs(),
        out_specs=[pl.BlockSpec((per, k, shard_w), lambda j, t: (j, 0, 0))] + comm.specs(),
        scratch_shapes=[pltpu.VMEM((k, tn), F32)] + comm.scratch(),
        compiler_params=_params(("arbitrary", "arbitrary")),
    )(a, b, *comm.arrays)
    return res[0] if comm.n == 0 else (res[0], res[1:])


def _branch_bwd(dba, dbp, a, ga, wa_blk, wp_blk, comm):
    s, d = dba.shape
    ka = wa_blk.shape[1]
    kp = wp_blk.shape[1]
    tm = min(512, s)
    nt = s // tm

    def body(*refs):
        ins, outs, scratch, c_in, c_out, c_sems = _split_refs(refs, 6, 3, 3, comm)
        dba_ref, dbp_ref, a_ref, ga_ref, wa_hbm, wp_hbm = ins
        da_ref, dga_ref, dyp_ref = outs
        wa, wp, sems = scratch
        comm.run(pl.program_id(0), nt, c_in, c_out, c_sems)

        @pl.when(pl.program_id(0) == 0)
        def _():
            _load_blocked(wa_hbm, wa, sems.at[pl.ds(0, N_DEV)], 1)
            _load_blocked(wp_hbm, wp, sems.at[pl.ds(N_DEV, N_DEV)], 1)

        for c0 in range(0, ka, 512):
            cols = slice(c0, c0 + 512)
            dya = lax.dot_general(dba_ref[...], wa[c0:c0 + 512, :], NT_DIMS, preferred_element_type=F32)
            gaf = ga_ref[:, cols].astype(F32)
            da_ref[:, cols] = (dya * _silu(gaf)).astype(BF16)
            dga_ref[:, cols] = (dya * a_ref[:, cols].astype(F32) * _dsilu(gaf)).astype(BF16)
        for c0 in range(0, kp, 512):
            dyp = lax.dot_general(dbp_ref[...], wp[c0:c0 + 512, :], NT_DIMS, preferred_element_type=F32)
            dyp_ref[:, c0:c0 + 512] = dyp.astype(BF16)

    res = pl.pallas_call(
        body, name="branch_bwd", grid=(nt,),
        out_shape=[jax.ShapeDtypeStruct((s, ka), BF16), jax.ShapeDtypeStruct((s, ka), BF16),
                   jax.ShapeDtypeStruct((s, kp), BF16)] + comm.out_shapes(),
        in_specs=[_row_spec(tm, d), _row_spec(tm, d), _row_spec(tm, ka), _row_spec(tm, ka), ANY_SPEC, ANY_SPEC]
                 + comm.specs(),
        out_specs=[_row_spec(tm, ka), _row_spec(tm, ka), _row_spec(tm, kp)] + comm.specs(),
        scratch_shapes=[pltpu.VMEM((ka, d), BF16), pltpu.VMEM((kp, d), BF16), pltpu.SemaphoreType.DMA((2 * N_DEV,))]
                       + comm.scratch(),
        compiler_params=_params(("arbitrary",)),
    )(dba, dbp, a, ga, wa_blk, wp_blk, *comm.arrays)
    return res[:3], res[3:]


def _attn_bwd(qkv, a, da, bias3, sink, comm):
    s = qkv.shape[0]
    nblk = s // QBLK
    clamp = lambda n: jnp.minimum(n, nblk - 1)
    q_spec = pl.BlockSpec((QBLK, ATTN_WIDTH), lambda n: (clamp(n), 0))
    kv_specs = _kv_specs(nblk)

    def body(*refs):
        ins, outs, scratch, c_in, c_out, c_sems = _split_refs(refs, 8, 4, 2, comm)
        sink_ref, q_ref, k0, k1, k2, a_ref, da_ref, bias_ref = ins
        dq_ref, dkv_ref, db_ref, ds_ref = outs
        acc, sacc = scratch
        n = pl.program_id(0)
        comm.run(n, nblk + 1, c_in, c_out, c_sems)

        @pl.when(n == 0)
        def _():
            acc[...] = jnp.zeros_like(acc)
            sacc[...] = jnp.zeros_like(sacc)
            db_ref[...] = jnp.zeros_like(db_ref)

        @pl.when(n < nblk)
        def _():
            kv = jnp.concatenate([k0[...], k1[...], k2[...]], axis=0)
            dks, dvs = [], []
            for k in range(N_KV_HEADS):
                kk = kv[:, k * HEAD_DIM:(k + 1) * HEAD_DIM]
                vk = kv[:, KV_WIDTH + k * HEAD_DIM:KV_WIDTH + (k + 1) * HEAD_DIM]
                heads = list(range(GQA_GROUP * k, GQA_GROUP * (k + 1)))
                stack = lambda ref: jnp.concatenate([ref[:, h * HEAD_DIM:(h + 1) * HEAD_DIM] for h in heads], axis=0)
                qk, do, o = stack(q_ref), stack(da_ref), stack(a_ref)
                bias = jnp.concatenate([bias_ref[0, h] for h in heads], axis=0)
                sink = jnp.concatenate([jnp.full((QBLK, 1), sink_ref[h], F32) for h in heads], axis=0)
                logits = lax.dot_general(qk, kk, NT_DIMS, preferred_element_type=F32) * ATTN_SCALE + bias
                mx = jnp.maximum(jnp.max(logits, axis=-1, keepdims=True), sink)
                e = jnp.exp(logits - mx)
                e_sink = jnp.exp(sink - mx)
                inv = 1.0 / (jnp.sum(e, axis=-1, keepdims=True) + e_sink)
                p = e * inv
                dp = lax.dot_general(do, vk, NT_DIMS, preferred_element_type=F32)
                delta = jnp.sum(do.astype(F32) * o.astype(F32), axis=-1, keepdims=True)
                dlog = p * (dp - delta)
                d_sink = -(e_sink * inv * delta)
                dlb = dlog.astype(BF16)
                dq = jnp.dot(dlb, kk, preferred_element_type=F32) * ATTN_SCALE
                for g, h in enumerate(heads):
                    rows = slice(g * QBLK, (g + 1) * QBLK)
                    sacc[h] += d_sink[rows]
                    db_ref[h] += dlog[rows]
                    dq_ref[:, h * HEAD_DIM:(h + 1) * HEAD_DIM] = dq[rows].astype(BF16)
                dks.append(lax.dot_general(dlb, qk, TN_DIMS, preferred_element_type=F32))
                dvs.append(lax.dot_general(p.astype(BF16), do, TN_DIMS, preferred_element_type=F32))
            contrib = jnp.concatenate(dks + dvs, axis=1)
            acc[n % 3] += contrib[0:QBLK]
            acc[(n + 1) % 3] += contrib[QBLK:2 * QBLK]
            acc[(n + 2) % 3] = contrib[2 * QBLK:3 * QBLK]

        col = lax.broadcasted_iota(jnp.int32, (1, 2 * KV_WIDTH), 1)
        dkv_ref[...] = (acc[n % 3] * jnp.where(col < KV_WIDTH, ATTN_SCALE, 1.0)).astype(BF16)

        @pl.when(n == nblk)
        def _():
            for h in range(N_Q_HEADS):
                ds_ref[h:h + 1, :] = jnp.full((1, 128), jnp.sum(sacc[h]), F32)

    res = pl.pallas_call(
        body, name="attn_bwd", grid=(nblk + 1,),
        out_shape=[jax.ShapeDtypeStruct((s, ATTN_WIDTH), BF16), jax.ShapeDtypeStruct((s, 2 * KV_WIDTH), BF16),
                   jax.ShapeDtypeStruct((N_Q_HEADS, QBLK, SPAN), F32),
                   jax.ShapeDtypeStruct((N_Q_HEADS, 128), F32)] + comm.out_shapes(),
        in_specs=[pl.BlockSpec(memory_space=pltpu.SMEM), q_spec] + kv_specs
                 + [q_spec, q_spec, _bias_spec(nblk)] + comm.specs(),
        out_specs=[q_spec, pl.BlockSpec((QBLK, 2 * KV_WIDTH), lambda n: (jnp.maximum(n - 1, 0), 0)),
                   _const_spec((N_Q_HEADS, QBLK, SPAN)), _const_spec((N_Q_HEADS, 128))] + comm.specs(),
        scratch_shapes=[pltpu.VMEM((3, QBLK, 2 * KV_WIDTH), F32), pltpu.VMEM((N_Q_HEADS, QBLK, 1), F32)]
                       + comm.scratch(),
        compiler_params=_params(("arbitrary",)),
    )(sink, qkv, qkv, qkv, qkv, a, da, bias3, *comm.arrays)
    return res[:4], res[4:]


def _pool_bwd_rows(u, gp, dyp, wp, scale):
    s, pw = u.shape
    c = pw // len(POOL_SIZES)
    tm = min(512, s)
    nt = s // tm

    def body(up, uc, un, gp_ref, dyp_ref, wp_ref, sc_ref, e_ref, dgp_ref, gwp_ref, dsc_ref):
        i = pl.program_id(0)

        @pl.when(i == 0)
        def _():
            gwp_ref[...] = jnp.zeros_like(gwp_ref)
            dsc_ref[...] = jnp.zeros_like(dsc_ref)

        ue = _with_halo(up, uc, un, i, nt)
        for g, w in enumerate(POOL_SIZES):
            cols = slice(g * c, (g + 1) * c)
            xg = ue[:, cols]
            inv_cnt = 1.0 / _pool_counts(i, tm, s, w)
            ws = _window_sum(xg, w, False)[POOL_HALO:POOL_HALO + tm]
            pooled = (ws * inv_cnt - xg[POOL_HALO:POOL_HALO + tm]).astype(BF16)
            mixed = jnp.dot(pooled, wp_ref[g], preferred_element_type=F32)
            sc = sc_ref[:, cols]
            gpf = gp_ref[:, cols].astype(F32)
            dyp_g = dyp_ref[:, cols].astype(F32)
            dp = dyp_g * _silu(gpf)
            dgp_ref[:, cols] = (dyp_g * (mixed * sc) * _dsilu(gpf)).astype(BF16)
            dsc_ref[:, cols] += jnp.sum(dp * mixed, axis=0, keepdims=True)
            dmixed = (dp * sc).astype(BF16)
            dpooled = lax.dot_general(dmixed, wp_ref[g], NT_DIMS, preferred_element_type=F32)
            gwp_ref[g] += lax.dot_general(pooled, dmixed, TN_DIMS, preferred_element_type=F32)
            e_ref[:, cols] = dpooled * inv_cnt

    return pl.pallas_call(
        body, name="pool_bwd_rows", grid=(nt,),
        out_shape=[jax.ShapeDtypeStruct((s, pw), F32), jax.ShapeDtypeStruct((s, pw), BF16),
                   jax.ShapeDtypeStruct(wp.shape, F32), jax.ShapeDtypeStruct((1, pw), F32)],
        in_specs=_halo_specs(tm, pw, s) + [_row_spec(tm, pw), _row_spec(tm, pw), _const_spec(wp.shape), _const_spec((1, pw))],
        out_specs=[_row_spec(tm, pw), _row_spec(tm, pw), _const_spec(wp.shape), _const_spec((1, pw))],
        compiler_params=_params(("arbitrary",)),
    )(u, u, u, gp, dyp, wp, scale)


def _pool_bwd_window(e):
    s, pw = e.shape
    c = pw // len(POOL_SIZES)
    tm = min(512, s)
    nt = s // tm

    def body(ep, ec, en, du_ref):
        i = pl.program_id(0)
        ee = _with_halo(ep, ec, en, i, nt)
        for g, w in enumerate(POOL_SIZES):
            cols = slice(g * c, (g + 1) * c)
            eg = ee[:, cols]
            ws = _window_sum(eg, w, True)[POOL_HALO:POOL_HALO + tm]
            du_ref[:, cols] = (ws - eg[POOL_HALO:POOL_HALO + tm] * _pool_counts(i, tm, s, w)).astype(BF16)

    return pl.pallas_call(
        body, name="pool_bwd_window", grid=(nt,),
        out_shape=jax.ShapeDtypeStruct((s, pw), BF16),
        in_specs=_halo_specs(tm, pw, s), out_specs=_row_spec(tm, pw),
        compiler_params=_params(("arbitrary",)),
    )(e, e, e)


def _dh_merge(dgm, pieces, w_blk, w_full, comm):
    s, n = dgm.shape
    d = w_blk.shape[1]
    tm = min(512, s)
    nt = s // tm
    widths = [p.shape[1] for p in pieces]
    n_p = len(pieces)
    wcols = sum(widths)

    def body(*refs):
        ins, outs, scratch, c_in, c_out, c_sems = _split_refs(refs, 3 + n_p, 1, 4, comm)
        dg_ref = ins[0]
        p_refs = ins[1:1 + n_p]
        w_hbm, wi_hbm = ins[1 + n_p:]
        o_ref, = outs
        w_vmem, wi_vmem, sems, sem = scratch
        comm.run(pl.program_id(0), nt, c_in, c_out, c_sems)

        @pl.when(pl.program_id(0) == 0)
        def _():
            cp = pltpu.make_async_copy(wi_hbm.at[:, pl.ds(0, wcols)], wi_vmem, sem)
            cp.start()
            _load_blocked(w_hbm, w_vmem, sems, 1)
            cp.wait()

        for r0 in range(0, d, 512):
            dh = lax.dot_general(dg_ref[...], w_vmem[r0:r0 + 512, :], NT_DIMS, preferred_element_type=F32)
            off = 0
            for p_ref, wd in zip(p_refs, widths):
                dh = dh + lax.dot_general(p_ref[...], wi_vmem[r0:r0 + 512, off:off + wd], NT_DIMS,
                                          preferred_element_type=F32)
                off += wd
            o_ref[:, r0:r0 + 512] = dh

    res = pl.pallas_call(
        body, name="dh_merge", grid=(nt,),
        out_shape=[jax.ShapeDtypeStruct((s, d), F32)] + comm.out_shapes(),
        in_specs=[_row_spec(tm, n)] + [_row_spec(tm, wd) for wd in widths] + [ANY_SPEC, ANY_SPEC] + comm.specs(),
        out_specs=[_row_spec(tm, d)] + comm.specs(),
        scratch_shapes=[pltpu.VMEM((d, n), BF16), pltpu.VMEM((d, wcols), BF16), pltpu.SemaphoreType.DMA((N_DEV,)),
                        pltpu.SemaphoreType.DMA] + comm.scratch(),
        compiler_params=_params(("arbitrary",)),
    )(dgm, *pieces, w_blk, w_full, *comm.arrays)
    return res[0], res[1:]


def _dh_final(dh1, pieces, w_full, x2, dy, vec):
    s, d = x2.shape
    tm = min(256, s)
    widths = [p.shape[1] for p in pieces]
    n_p = len(pieces)
    wcols = sum(widths)
    comm = NO_COMM

    def body(*refs):
        ins, outs, scratch, _, _, _ = _split_refs(refs, 5 + n_p, 2, 2, comm)
        dh1_ref = ins[0]
        p_refs = ins[1:1 + n_p]
        w_hbm, x_ref, dy_ref, vec_ref = ins[1 + n_p:]
        gx_ref, acc_ref = outs
        w_vmem, sem = scratch

        @pl.when(pl.program_id(0) == 0)
        def _():
            cp = pltpu.make_async_copy(w_hbm.at[:, pl.ds(w_full.shape[1] - wcols, wcols)], w_vmem, sem)
            cp.start()
            cp.wait()
            acc_ref[...] = jnp.zeros_like(acc_ref)

        scale1 = vec_ref[0:1, :]
        pre_g = vec_ref[2:3, :]
        xf = x_ref[...]
        r = lax.rsqrt(jnp.mean(xf * xf, axis=-1, keepdims=True) + EPS)
        xn = xf * r
        hn = xn * pre_g
        for r0 in range(0, d, 512):
            rows = slice(r0, r0 + 512)
            dh = dh1_ref[:, rows]
            off = 0
            for p_ref, wd in zip(p_refs, widths):
                dh = dh + lax.dot_general(p_ref[...], w_vmem[r0:r0 + 512, off:off + wd], NT_DIMS,
                                          preferred_element_type=F32)
                off += wd
            acc_ref[0:1, rows] += jnp.sum(dh, axis=0, keepdims=True)
            acc_ref[1:2, rows] += jnp.sum(dh * hn[:, rows], axis=0, keepdims=True)
            dhn = dh * scale1[:, rows]
            acc_ref[2:3, rows] += jnp.sum(dhn * xn[:, rows], axis=0, keepdims=True)
            gx_ref[:, rows] = dhn * pre_g[:, rows]
        dxn = gx_ref[...]
        dx = r * (dxn - xn * jnp.mean(dxn * xn, axis=-1, keepdims=True))
        gx_ref[...] = dy_ref[...] + dx

    return pl.pallas_call(
        body, name="dh_final", grid=(s // tm,),
        out_shape=[jax.ShapeDtypeStruct((s, d), F32), jax.ShapeDtypeStruct((8, d), F32)],
        in_specs=[_row_spec(tm, d)] + [_row_spec(tm, wd) for wd in widths]
                 + [ANY_SPEC, _row_spec(tm, d), _row_spec(tm, d), _const_spec((8, d))],
        out_specs=[_row_spec(tm, d), _const_spec((8, d))],
        scratch_shapes=[pltpu.VMEM((d, wcols), BF16), pltpu.SemaphoreType.DMA],
        compiler_params=_params(("arbitrary",)),
    )(dh1, *pieces, w_full, x2, dy, vec)


def _reduce_adamw(parts, w, m, v, name):
    r, c = w.shape
    n_parts = len(parts)
    rp = r // n_parts
    tr = min(256, rp)
    per = rp // tr

    def body(*refs):
        p_refs = refs[:n_parts]
        w_ref, m_ref, v_ref, g_ref, dl_ref, m2_ref, v2_ref = refs[n_parts:]
        i = pl.program_id(0)
        for k, p_ref in enumerate(p_refs):
            @pl.when((i >= k * per) & (i < (k + 1) * per))
            def _(p_ref=p_ref):
                g = p_ref[0].astype(F32)
                for dev in range(1, N_DEV):
                    g = g + p_ref[dev].astype(F32)
                g_ref[...] = g
                dl_ref[...], m2_ref[...], v2_ref[...] = _adamw(w_ref[...], g, m_ref[...], v_ref[...])

    spec = _row_spec(tr, c)
    part_specs = [pl.BlockSpec((N_DEV, tr, c), (lambda i, k=k: (0, jnp.clip(i - k * per, 0, per - 1), 0)))
                  for k in range(n_parts)]
    return pl.pallas_call(
        body, name=name, grid=(r // tr,),
        out_shape=[jax.ShapeDtypeStruct((r, c), F32)] * 4,
        in_specs=part_specs + [spec, spec, spec],
        out_specs=[spec] * 4, compiler_params=_params(("arbitrary",)),
    )(*parts, w, m, v)


def _small_reduce_adamw(gathered, w, m, v):
    rows, c = w.shape

    def body(p_ref, w_ref, m_ref, v_ref, g_ref, dl_ref, m2_ref, v2_ref):
        g = p_ref[0:rows, :]
        for dev in range(1, N_DEV):
            g = g + p_ref[dev * rows:(dev + 1) * rows, :]
        g_ref[...] = g
        dl_ref[...], m2_ref[...], v2_ref[...] = _adamw(w_ref[...], g, m_ref[...], v_ref[...])

    vm = pl.BlockSpec(memory_space=pltpu.VMEM)
    return pl.pallas_call(
        body, name="small_reduce_adamw", out_shape=[jax.ShapeDtypeStruct((rows, c), F32)] * 4,
        in_specs=[vm] * 4, out_specs=[vm] * 4, compiler_params=_params(),
    )(gathered, w, m, v)


def _pack_small(parts, total):
    flat = jnp.concatenate([p.reshape(-1).astype(F32) for p in parts])
    flat = jnp.pad(flat, (0, total - flat.shape[0]))
    return flat.reshape(8, total // 8)


def _unpack_small(packed, shapes):
    flat = packed.reshape(-1)
    out, off = [], 0
    for shp in shapes:
        size = math.prod(shp)
        out.append(flat[off:off + size].reshape(shp))
        off += size
    return out


def kernel(x, c, rel_bias_table, w_ada, b_ada, pre_norm_g, post_norm_g, w_in, attn_sink, w_pool_group, pool_scale, w_branch_attn, w_branch_pool, w_merge, b_merge, w_out, loss_target, m_rel_bias_table, m_w_ada, m_b_ada, m_pre_norm_g, m_post_norm_g, m_w_in, m_attn_sink, m_w_pool_group, m_pool_scale, m_w_branch_attn, m_w_branch_pool, m_w_merge, m_b_merge, m_w_out, v_rel_bias_table, v_w_ada, v_b_ada, v_pre_norm_g, v_post_norm_g, v_w_in, v_attn_sink, v_w_pool_group, v_pool_scale, v_w_branch_attn, v_w_branch_pool, v_w_merge, v_b_merge, v_w_out):
    xi, yi, ci = lax.axis_index("x"), lax.axis_index("y"), lax.axis_index("c")
    me = 4 * xi + 2 * yi + ci
    x2, t2 = x[0], loss_target[0]
    s, d = x2.shape
    pw = d // 2
    n_groups = len(POOL_SIZES)
    cg = pw // n_groups
    in_w = w_in.shape[2] * N_DEV
    ada_cols = w_ada.shape[2]

    c_all = _allgather_small(jnp.broadcast_to(c, (8, d)), "gather_c").reshape(N_DEV, 8, d)[:, 0, :]
    c16 = jnp.concatenate([c_all, jnp.zeros_like(c_all)], axis=0)
    b_cols = lax.dynamic_slice_in_dim(b_ada, me * ada_cols, ada_cols, axis=1)
    mod_part = _ada_fwd(c16, w_ada[0], b_cols)[:8]
    mod_all = _allgather_small(mod_part, "gather_mod").reshape(N_DEV, 8, ada_cols)
    mod = lax.dynamic_index_in_dim(mod_all, me, axis=1, keepdims=False).reshape(3 * d)
    shift, scale, gate = mod[:d], mod[d:2 * d], mod[2 * d:]
    zrow = jnp.zeros((d,), F32)
    vec_in = jnp.stack([1.0 + scale, shift, pre_norm_g[0], zrow, zrow, zrow, zrow, zrow])
    vec_out = jnp.stack([gate, post_norm_g[0], zrow, zrow, zrow, zrow, zrow, zrow])

    rows_per = w_pool_group.shape[2]
    late_shards = [w.astype(BF16) for w in (w_branch_attn[0], w_branch_pool[0], w_out[0],
                                            w_pool_group[0].reshape(n_groups * rows_per, cg))]
    w_in_bf = w_in[0].astype(BF16)
    n_chunk = 4
    rows_c = d // n_chunk
    w_in_parts = _allgather_hbm([w_in_bf[k * rows_c:(k + 1) * rows_c] for k in range(n_chunk)], "gather_w_in")
    w_in_full = jnp.concatenate([p.transpose(1, 0, 2).reshape(rows_c, in_w) for p in w_in_parts], axis=0)

    rel = jnp.arange(SPAN)[None, :] - WINDOW - jnp.arange(QBLK)[:, None]
    idx = jnp.where(jnp.abs(rel) <= WINDOW, _t5_bucket(rel), -1).astype(jnp.int32).reshape(1, QBLK * SPAN)
    bias = _bias_variants(_bias_build(rel_bias_table.T, idx).reshape(N_Q_HEADS, QBLK, SPAN))
    sink = attn_sink[0]

    (h, qkv, ga, u, gp), (w_merge_blk,) = _fwd_in(x2, vec_in, w_in_full, _Carried("gather", [w_merge[0].astype(BF16)]))
    g, (w_ba_blk, w_bp_blk, w_out_blk, w_pool_blk) = _merge_fwd(h, w_merge_blk, b_merge, _Carried("gather", late_shards))
    w_pool_full = w_pool_blk.reshape(N_DEV, n_groups, rows_per, cg).transpose(1, 0, 2, 3).reshape(n_groups, cg, cg)
    a, ya = _attn_fwd(qkv, ga, bias, sink)
    yp = _pool_fwd(u, gp, w_pool_full, pool_scale)
    ba, bp, merged = _branch_fwd(ya, yp, g, w_ba_blk, w_bp_blk)
    do, dgm, dba, dbp, dy, acc_out, db_merge = _out_fused(merged, x2, t2, g, ba, bp, vec_out, w_out_blk)
    loss = lax.psum(0.5 * jnp.sum(acc_out[0]) / d, MESH_AXES)

    gw_out = _wgrad(merged, do, d, d, "wgrad_out")[0].reshape(N_DEV, d // N_DEV, d)
    gw_ba = _wgrad(ya, dba, d, d // N_DEV, "wgrad_branch_attn")
    gw_bp = _wgrad(yp, dbp, d, d // N_DEV, "wgrad_branch_pool")
    gw_merge, (r_out,) = _wgrad(h, dgm, min(2 * d, 2048), 2 * d // N_DEV, "wgrad_merge",
                                _Carried("exchange", [gw_out]))
    (da, dga, dyp), (r_ba, r_bp) = _branch_bwd(dba, dbp, a, ga, w_ba_blk, w_bp_blk,
                                               _Carried("exchange", [gw_ba, gw_bp]))
    (dq, dkv, dbias, dsink), (r_merge,) = _attn_bwd(qkv, a, da, bias, sink, _Carried("exchange", [gw_merge]))
    e, dgp, gw_pool, dpool_scale = _pool_bwd_rows(u, gp, dyp, w_pool_full, pool_scale)
    du = _pool_bwd_window(e)
    pieces = [dq, dkv, dga, du, dgp]
    gw_in = jnp.concatenate([_wgrad(h, p, p.shape[1], p.shape[1], "wgrad_in_%d" % i)[0]
                             for i, p in enumerate(pieces)], axis=1)
    gw_in_blk = gw_in.reshape(d, N_DEV, in_w // N_DEV).transpose(1, 0, 2)
    gw_pool_blk = gw_pool.astype(BF16).reshape(n_groups, N_DEV, rows_per, cg).transpose(1, 0, 2, 3).reshape(
        N_DEV, n_groups * rows_per, cg)
    dh1, (r_in, r_pool) = _dh_merge(dgm, pieces[:3], w_merge_blk, w_in_full, _Carried("exchange", [gw_in_blk, gw_pool_blk]))
    gx, acc_in = _dh_final(dh1, pieces[3:], w_in_full, x2, dy, vec_in)
    d_table = _bias_reduce(dbias.reshape(N_Q_HEADS, QBLK * SPAN), idx).T

    recv = [[r_in], [r_merge], [r_ba], [r_bp], [r_out], [r_pool]]
    names = ["w_in", "w_merge", "w_branch_attn", "w_branch_pool", "w_out", "w_pool_group"]
    big_w = [w_in, w_merge, w_branch_attn, w_branch_pool, w_out, w_pool_group]
    big_m = [m_w_in, m_w_merge, m_w_branch_attn, m_w_branch_pool, m_w_out, m_w_pool_group]
    big_v = [v_w_in, v_w_merge, v_w_branch_attn, v_w_branch_pool, v_w_out, v_w_pool_group]
    big = {}
    for nm, rc, w_, m_, v_ in zip(names, recv, big_w, big_m, big_v):
        shp2 = (len(rc) * rc[0].shape[1], rc[0].shape[2])
        outs = _reduce_adamw(rc, w_.reshape(shp2), m_.reshape(shp2), v_.reshape(shp2), "adamw_" + nm)
        big[nm] = [o.reshape(w_.shape) for o in outs]

    dmod = jnp.concatenate([acc_in[0], acc_in[1], acc_out[1]])
    small_names = ["b_ada", "pre_norm_g", "post_norm_g", "pool_scale", "b_merge", "rel_bias_table", "attn_sink"]
    small_g = [dmod, acc_in[2], acc_out[2], dpool_scale, db_merge, d_table, dsink[:, 0]]
    small_w = [b_ada, pre_norm_g, post_norm_g, pool_scale, b_merge, rel_bias_table, attn_sink]
    small_m = [m_b_ada, m_pre_norm_g, m_post_norm_g, m_pool_scale, m_b_merge, m_rel_bias_table, m_attn_sink]
    small_v = [v_b_ada, v_pre_norm_g, v_post_norm_g, v_pool_scale, v_b_merge, v_rel_bias_table, v_attn_sink]
    n_small = sum(math.prod(w_.shape) for w_ in small_w)
    total = -(-n_small // 1024) * 1024
    gathered = _allgather_small(_pack_small(small_g, total), "gather_small_grads")
    sm = _small_reduce_adamw(gathered, _pack_small(small_w, total), _pack_small(small_m, total),
                             _pack_small(small_v, total))
    shapes = [w_.shape for w_ in small_w]
    small = {nm: vals for nm, vals in zip(small_names, zip(*[_unpack_small(o, shapes) for o in sm]))}

    dmod_all = gathered.reshape(N_DEV, -1)[:, :3 * d]
    dmod_mine = lax.dynamic_slice_in_dim(dmod_all, me * ada_cols, ada_cols, axis=1)
    dmod16 = jnp.concatenate([dmod_mine, jnp.zeros_like(dmod_mine)], axis=0)
    ada = [o[None] for o in _ada_bwd(c16, dmod16, w_ada[0], m_w_ada[0], v_w_ada[0])]

    order = ["rel_bias_table", "w_ada", "b_ada", "pre_norm_g", "post_norm_g", "w_in", "attn_sink", "w_pool_group",
             "pool_scale", "w_branch_attn", "w_branch_pool", "w_merge", "b_merge", "w_out"]
    results = dict(big)
    results.update({nm: list(vals) for nm, vals in small.items()})
    results["w_ada"] = ada
    outs = [loss, gx[None]]
    for k in range(4):
        outs += [results[nm][k] for nm in order]
    return tuple(outs)
```
